```python
import jax
import jax.numpy as jnp
from jax import lax
import numpy as np

D_MODEL = 2048
BATCH = 4
SEQ = 8192
DEPTH = 1

ATTN_HEAD_DIM = 128
ATTN_WIDTH = D_MODEL // 2
ATTN_HEADS = ATTN_WIDTH // ATTN_HEAD_DIM
RET_HEAD_DIM = 256
RET_WIDTH = D_MODEL - ATTN_WIDTH
RET_HEADS = RET_WIDTH // RET_HEAD_DIM
MIX_WIDTH = ATTN_WIDTH + RET_WIDTH
IN_COLS = 3 * ATTN_WIDTH + 4 * RET_WIDTH
DILATED_PATTERNS = ((128, 1), (512, 4), (2048, 16))
ROPE_THETA = 500000.0
ROPE_DIM = ATTN_HEAD_DIM // 4
RET_THETA = 10000.0
RET_CHUNK = 128
N_GROUPS = 4
EXPERTS_PER_GROUP = 8
N_EXPERTS = N_GROUPS * EXPERTS_PER_GROUP
TOP_K_IN_GROUP = 2
EXPERT_FF = D_MODEL // 4
MOE_BLOCK = 128
DEEPNORM_ALPHA = (2 * DEPTH) ** 0.25
DEEPNORM_BETA = (8 * DEPTH) ** -0.25
LN_EPS = 1e-5
MASK_VALUE = -1e30

kernel_name = 'hybrid_dilated_retention_hmoe_encoder'


def layer_norm(x, g=None, b=None):
    xf = x.astype(jnp.float32)
    mu = jnp.mean(xf, -1, keepdims=True)
    var = jnp.mean(jnp.square(xf - mu), -1, keepdims=True)
    y = (xf - mu) * lax.rsqrt(var + LN_EPS)
    if g is not None:
        y = y * g + b
    return y.astype(x.dtype)


def rotary(x, pos, inv_freq):
    n = inv_freq.shape[0]
    ang = pos.astype(jnp.float32)[:, None, :, None] * inv_freq
    cos, sin = jnp.cos(ang), jnp.sin(ang)
    xf = x.astype(jnp.float32)
    x1, x2, rest = xf[..., :n], xf[..., n:2 * n], xf[..., 2 * n:]
    out = jnp.concatenate([x1 * cos - x2 * sin, x2 * cos + x1 * sin, rest], axis=-1)
    return out.astype(x.dtype)


def banded_softmax_attention(q, k, v, half):
    lead = q.shape[:-2]
    L, hd = q.shape[-2], q.shape[-1]
    W = half
    nb = -(-L // W)
    Lp = nb * W
    nl = len(lead)
    qp = jnp.pad(q, [(0, 0)] * nl + [(0, Lp - L), (0, 0)]).reshape(*lead, nb, W, hd)
    kv_pad = [(0, 0)] * nl + [(W, Lp - L + W), (0, 0)]
    kp = jnp.pad(k, kv_pad)
    vp = jnp.pad(v, kv_pad)

    def key_blocks(t):
        return jnp.concatenate([t[..., o * W:(o + nb) * W, :].reshape(*lead, nb, W, hd) for o in range(3)], axis=-2)

    kb, vb = key_blocks(kp), key_blocks(vp)
    qpos = jnp.arange(Lp).reshape(nb, W, 1)
    kpos = (jnp.arange(nb) * W - W)[:, None, None] + jnp.arange(3 * W)[None, None, :]
    mask = (jnp.abs(qpos - kpos) <= half) & (kpos >= 0) & (kpos < L)
    s = jnp.einsum('...nqd,...nkd->...nqk', qp, kb).astype(jnp.float32)
    s = jnp.where(mask, s, MASK_VALUE)
    m = jnp.max(s, -1, keepdims=True)
    p = jnp.exp(s - m)
    den = jnp.sum(p, -1)
    o = jnp.einsum('...nqk,...nkd->...nqd', p, vb.astype(jnp.float32)) / den[..., None]
    lse = m[..., 0] + jnp.log(den)
    o = o.reshape(*lead, Lp, hd)[..., :L, :]
    lse = lse.reshape(*lead, Lp)[..., :L]
    return o, lse


def dilated_mixture_attention(q, k, v):
    B, H, S, hd = q.shape
    outs, lses = [], []
    for window, dil in DILATED_PATTERNS:
        half = window // (2 * dil)

        def to_res(t):
            return t.reshape(B, H, S // dil, dil, hd).transpose(0, 1, 3, 2, 4)

        o, lse = banded_softmax_attention(to_res(q), to_res(k), to_res(v), half)
        outs.append(o.transpose(0, 1, 3, 2, 4).reshape(B, H, S, hd))
        lses.append(lse.transpose(0, 1, 3, 2).reshape(B, H, S))
    w = jax.nn.softmax(jnp.stack(lses), axis=0)
    out = jnp.einsum('pbhs,pbhsd->bhsd', w, jnp.stack(outs))
    return out.astype(q.dtype)


def retention_one_direction(q, k, v, log_decay, inclusive):
    B, H, S, dh = q.shape
    C = RET_CHUNK
    N = S // C
    lg = log_decay.astype(jnp.float32)
    idx = jnp.arange(C, dtype=jnp.float32)
    diff = idx[:, None] - idx[None, :]
    keep = diff >= 0 if inclusive else diff > 0
    dmat = jnp.where(keep, jnp.exp(lg[:, None, None] * jnp.maximum(diff, 0.0)), 0.0)
    qc, kc, vc = [t.astype(jnp.float32).reshape(B, H, N, C, dh) for t in (q, k, v)]
    scores = jnp.einsum('bhnid,bhnjd->bhnij', qc, kc) * dmat[None, :, None]
    y_intra = jnp.einsum('bhnij,bhnjd->bhnid', scores, vc)
    zeta = jnp.exp(lg[:, None] * (C - 1 - idx))
    xi = jnp.exp(lg[:, None] * (idx + 1))
    chunk_decay = jnp.exp(lg * C)

    def step(state, xs):
        qn, kn, vn = xs
        y = jnp.einsum('bhcd,bhde->bhce', qn, state) * xi[None, :, :, None]
        state = state * chunk_decay[None, :, None, None] + jnp.einsum('bhcd,bhce->bhde', kn * zeta[None, :, :, None], vn)
        return state, y

    xs = tuple(jnp.moveaxis(t, 2, 0) for t in (qc, kc, vc))
    _, y_cross = lax.scan(step, jnp.zeros((B, H, dh, dh), jnp.float32), xs)
    return (y_intra + jnp.moveaxis(y_cross, 0, 2)).reshape(B, H, S, dh)


def bidirectional_retention(q, k, v, log_decay_f, log_decay_b):
    y_f = retention_one_direction(q, k, v, log_decay_f, True)
    flip = lambda t: jnp.flip(t, axis=2)
    y_b = flip(retention_one_direction(flip(q), flip(k), flip(v), log_decay_b, False))
    return layer_norm(y_f + y_b)


def hierarchical_moe(h, w_group, b_group, w_sub, b_sub, w1, w3, w2):
    B, S, D = h.shape
    T = B * S
    A = T * TOP_K_IN_GROUP
    ht = h.reshape(T, D)
    glog = (ht @ w_group + b_group).astype(jnp.float32)
    gprob = jax.nn.softmax(glog, axis=-1)
    gsel = jnp.argmax(glog, axis=-1)
    pg = jnp.take_along_axis(gprob, gsel[:, None], axis=1)[:, 0]
    slog_all = jnp.einsum('td,gde->tge', ht, w_sub) + b_sub
    slog = jnp.take_along_axis(slog_all, gsel[:, None, None], axis=1)[:, 0].astype(jnp.float32)
    top_v, top_i = lax.top_k(slog, TOP_K_IN_GROUP)
    top_w = jax.nn.softmax(top_v, axis=-1) * pg[:, None]
    eid = (gsel[:, None] * EXPERTS_PER_GROUP + top_i).reshape(-1).astype(jnp.int32)
    wt = top_w.reshape(-1)
    tok = jnp.repeat(jnp.arange(T, dtype=jnp.int32), TOP_K_IN_GROUP)
    order = jnp.argsort(eid)
    e_sorted = eid[order]
    counts = jnp.bincount(eid, length=N_EXPERTS)
    padded = (counts + MOE_BLOCK - 1) // MOE_BLOCK * MOE_BLOCK
    pend = jnp.cumsum(padded)
    pstart = pend - padded
    start = jnp.cumsum(counts) - counts
    rank = jnp.arange(A) - start[e_sorted]
    dest = pstart[e_sorted] + rank
    P = A + N_EXPERTS * MOE_BLOCK
    nblk = P // MOE_BLOCK
    buf_tok = jnp.zeros((P,), jnp.int32).at[dest].set(tok[order])
    buf_w = jnp.zeros((P,), jnp.float32).at[dest].set(wt[order])
    blk_e = jnp.minimum(jnp.searchsorted(pend, jnp.arange(nblk) * MOE_BLOCK, side='right'), N_EXPERTS - 1)
    xb = ht[buf_tok].reshape(nblk, MOE_BLOCK, D)

    def expert_block(args):
        xblk, e = args
        return (jax.nn.silu(xblk @ w1[e]) * (xblk @ w3[e])) @ w2[e]

    yb = lax.map(expert_block, (xb, blk_e)).reshape(P, D)
    out = jnp.zeros((T, D), jnp.float32).at[buf_tok].add(yb.astype(jnp.float32) * buf_w[:, None])
    return out.reshape(B, S, D).astype(h.dtype)


def setup_inputs(seed: int = 0) -> dict:
    key = jax.random.key(seed)
    ks = jax.random.split(key, 24)
    f32 = jnp.float32
    D = D_MODEL

    def nrm(k, shape, scale):
        return jax.random.normal(k, shape, f32) * scale

    x = nrm(ks[0], (BATCH, SEQ, D), 1.0)
    c = nrm(ks[1], (BATCH, D), 1.0)
    positions = (jnp.cumsum(jax.random.randint(ks[2], (BATCH, SEQ), 1, 3), axis=1) - 1).astype(jnp.int32)
    ln0_g = 1.0 + nrm(ks[3], (D,), 0.02)
    ln0_b = nrm(ks[4], (D,), 0.02)
    w_ada = nrm(ks[5], (DEPTH, D, 6 * D), 0.2 * D ** -0.5)
    b_ada = nrm(ks[6], (DEPTH, 6 * D), 0.01)
    col_scale = jnp.concatenate([
        jnp.ones((2 * ATTN_WIDTH,), f32), jnp.full((ATTN_WIDTH,), DEEPNORM_BETA, f32),
        jnp.ones((2 * RET_WIDTH,), f32), jnp.full((RET_WIDTH,), DEEPNORM_BETA, f32),
        jnp.ones((RET_WIDTH,), f32)])
    w_in = nrm(ks[7], (DEPTH, D, IN_COLS), D ** -0.5) * col_scale
    w_out = nrm(ks[8], (DEPTH, MIX_WIDTH, D), DEEPNORM_BETA * MIX_WIDTH ** -0.5)
    base = jnp.log(1.0 - 2.0 ** (-5.0 - jnp.arange(RET_HEADS, dtype=f32)))
    ret_log_decay_f = base * jnp.exp(nrm(ks[9], (DEPTH, RET_HEADS), 0.1))
    ret_log_decay_b = base * jnp.exp(nrm(ks[10], (DEPTH, RET_HEADS), 0.1))
    ln1_g = 1.0 + nrm(ks[11], (DEPTH, D), 0.02)
    ln1_b = nrm(ks[12], (DEPTH, D), 0.02)
    w_group = nrm(ks[13], (DEPTH, D, N_GROUPS), D ** -0.5)
    b_group = nrm(ks[14], (DEPTH, N_GROUPS), 0.01)
    w_sub = nrm(ks[15], (DEPTH, N_GROUPS, D, EXPERTS_PER_GROUP), D ** -0.5)
    b_sub = nrm(ks[16], (DEPTH, N_GROUPS, EXPERTS_PER_GROUP), 0.01)
    w1 = nrm(ks[17], (DEPTH, N_EXPERTS, D, EXPERT_FF), D ** -0.5)
    w3 = nrm(ks[18], (DEPTH, N_EXPERTS, D, EXPERT_FF), DEEPNORM_BETA * D ** -0.5)
    w2 = nrm(ks[19], (DEPTH, N_EXPERTS, EXPERT_FF, D), DEEPNORM_BETA * EXPERT_FF ** -0.5)
    ln2_g = 1.0 + nrm(ks[20], (DEPTH, D), 0.02)
    ln2_b = nrm(ks[21], (DEPTH, D), 0.02)
    return {'x': x, 'c': c, 'positions': positions, 'ln0_g': ln0_g, 'ln0_b': ln0_b,
            'w_ada': w_ada, 'b_ada': b_ada, 'w_in': w_in, 'w_out': w_out,
            'ret_log_decay_f': ret_log_decay_f, 'ret_log_decay_b': ret_log_decay_b,
            'ln1_g': ln1_g, 'ln1_b': ln1_b, 'w_group': w_group, 'b_group': b_group,
            'w_sub': w_sub, 'b_sub': b_sub, 'w1': w1, 'w3': w3, 'w2': w2,
            'ln2_g': ln2_g, 'ln2_b': ln2_b}


def reference(x, c, positions, ln0_g, ln0_b, w_ada, b_ada, w_in, w_out, ret_log_decay_f, ret_log_decay_b,
              ln1_g, ln1_b, w_group, b_group, w_sub, b_sub, w1, w3, w2, ln2_g, ln2_b):
    B, S, _ = x.shape
    x = layer_norm(x, ln0_g, ln0_b)
    inv_rope = ROPE_THETA ** (-jnp.arange(0, ROPE_DIM, 2, dtype=jnp.float32) / ROPE_DIM)
    inv_ret = RET_THETA ** (-jnp.linspace(0.0, 1.0, RET_HEAD_DIM // 2, dtype=jnp.float32))
    cs = jax.nn.silu(c)
    A, R = ATTN_WIDTH, RET_WIDTH

    def heads(t, n, d):
        return t.reshape(B, S, n, d).transpose(0, 2, 1, 3)

    for l in range(DEPTH):
        mod = cs @ w_ada[l] + b_ada[l]
        sh1, sc1, g1, sh2, sc2, g2 = [m[:, None, :] for m in jnp.split(mod, 6, axis=-1)]
        h = x * (1.0 + sc1) + sh1
        z = h @ w_in[l]
        aq, ak, av, rq, rk, rv, rg = jnp.split(z, [A, 2 * A, 3 * A, 3 * A + R, 3 * A + 2 * R, 3 * A + 3 * R], axis=-1)
        aq = rotary(heads(aq, ATTN_HEADS, ATTN_HEAD_DIM), positions, inv_rope) * ATTN_HEAD_DIM ** -0.5
        ak = rotary(heads(ak, ATTN_HEADS, ATTN_HEAD_DIM), positions, inv_rope)
        attn = dilated_mixture_attention(aq, ak, heads(av, ATTN_HEADS, ATTN_HEAD_DIM))
        attn = attn.transpose(0, 2, 1, 3).reshape(B, S, A)
        rq = rotary(heads(rq, RET_HEADS, RET_HEAD_DIM), positions, inv_ret)
        rk = rotary(heads(rk, RET_HEADS, RET_HEAD_DIM), positions, inv_ret) * RET_HEAD_DIM ** -0.5
        ret = bidirectional_retention(rq, rk, heads(rv, RET_HEADS, RET_HEAD_DIM), ret_log_decay_f[l], ret_log_decay_b[l])
        ret = jax.nn.silu(rg) * ret.transpose(0, 2, 1, 3).reshape(B, S, R).astype(rg.dtype)
        mix = jnp.concatenate([attn, ret], axis=-1) @ w_out[l]
        x = layer_norm(DEEPNORM_ALPHA * x + (1.0 + g1) * mix, ln1_g[l], ln1_b[l])
        h = x * (1.0 + sc2) + sh2
        ffn = hierarchical_moe(h, w_group[l], b_group[l], w_sub[l], b_sub[l], w1[l], w3[l], w2[l])
        x = layer_norm(DEEPNORM_ALPHA * x + (1.0 + g2) * ffn, ln2_g[l], ln2_b[l])
    return x
```

```python
import functools

import jax
import jax.numpy as jnp
from jax import lax
from jax.experimental import pallas as pl
from jax.experimental.pallas import tpu as pltpu

F32 = jnp.float32
BF16 = jnp.bfloat16
I32 = jnp.int32

LANES = 128
ATTN_HEAD_DIM = 128
RET_HEAD_DIM = 256
PAIR = 2 * ATTN_HEAD_DIM
DILATIONS = (1, 4, 16)
BAND = 64
QT = 128
KW = QT + 2 * BAND
ROPE_THETA = 500000.0
ROPE_DIM = ATTN_HEAD_DIM // 4
RET_THETA = 10000.0
RET_CHUNK = 128
N_GROUPS = 4
EXPERTS_PER_GROUP = 8
N_EXPERTS = N_GROUPS * EXPERTS_PER_GROUP
MOE_BLOCK = 128
LN_EPS = 1e-5
MASK_VALUE = -1e30
N_SEG = 7
ROUTER_ROWS = 48
VMEM_LIMIT = 56 * 1024 * 1024


def _cparams(sem, vmem=VMEM_LIMIT):
    return pltpu.CompilerParams(dimension_semantics=sem, vmem_limit_bytes=vmem)


def _ln(x, g, b):
    mu = jnp.mean(x, axis=-1, keepdims=True)
    xc = x - mu
    var = jnp.mean(xc * xc, axis=-1, keepdims=True)
    return xc * lax.rsqrt(var + LN_EPS) * g + b


def _ada_kernel(c_ref, w_ref, b_ref, o_ref):
    cs = jax.nn.silu(c_ref[...])
    o_ref[...] = jnp.dot(cs.astype(BF16), w_ref[...].astype(BF16), preferred_element_type=F32) + b_ref[...]


def _ada(c8, w_ada, b_ada):
    d, n = w_ada.shape
    tn = min(n, 512)
    return pl.pallas_call(
        _ada_kernel,
        grid=(n // tn,),
        in_specs=[pl.BlockSpec((8, d), lambda j: (0, 0)),
                  pl.BlockSpec((d, tn), lambda j: (0, j)),
                  pl.BlockSpec((1, tn), lambda j: (0, j))],
        out_specs=pl.BlockSpec((8, tn), lambda j: (0, j)),
        out_shape=jax.ShapeDtypeStruct((8, n), F32),
        compiler_params=_cparams(("arbitrary",)),
        name="ada",
    )(c8, w_ada, b_ada)


def _inproj_kernel(x_ref, pos_ref, mod_ref, g_ref, b_ref, w_ref, fa_ref, fr_ref, z_ref,
                   h_ref, ca_ref, sa1_ref, sa2_ref, cr_ref, sr_ref, *, n_sub):
    j = pl.program_id(1)

    @pl.when(j == 0)
    def _():
        xn = _ln(x_ref[...], g_ref[...], b_ref[...])
        h = xn * (1.0 + mod_ref[0, 1:2, :]) + mod_ref[0, 0:1, :]
        h_ref[...] = h.astype(BF16)
        pos = pos_ref[...]
        ang = pos * fa_ref[...]
        c, s = jnp.cos(ang), jnp.sin(ang)
        lane = lax.broadcasted_iota(I32, ang.shape, 1)
        half = ROPE_DIM // 2
        ca_ref[...] = c
        sa1_ref[...] = jnp.where(lane < half, -s, 0.0)
        sa2_ref[...] = jnp.where((lane >= half) & (lane < 2 * half), s, 0.0)
        angr = pos * fr_ref[...]
        cr_ref[...] = jnp.cos(angr)
        sr_ref[...] = jnp.sin(angr)

    res = jnp.dot(h_ref[...], w_ref[...], preferred_element_type=F32)

    def attn_rot(scale):
        ca, sa1, sa2 = ca_ref[...], sa1_ref[...], sa2_ref[...]
        for hc in range(2 * n_sub):
            xh = res[:, hc * LANES:(hc + 1) * LANES]
            half = ROPE_DIM // 2
            r = xh * ca + pltpu.roll(xh, LANES - half, 1) * sa1 + pltpu.roll(xh, half, 1) * sa2
            if scale is not None:
                r = r * scale
            z_ref[hc // 2, :, (hc % 2) * LANES:(hc % 2 + 1) * LANES] = r.astype(BF16)

    def ret_rot(scale):
        cr, sr = cr_ref[...], sr_ref[...]
        for hc in range(n_sub):
            x1 = res[:, hc * PAIR:hc * PAIR + LANES]
            x2 = res[:, hc * PAIR + LANES:(hc + 1) * PAIR]
            o1 = x1 * cr - x2 * sr
            o2 = x2 * cr + x1 * sr
            if scale is not None:
                o1, o2 = o1 * scale, o2 * scale
            z_ref[hc, :, 0:LANES] = o1.astype(BF16)
            z_ref[hc, :, LANES:PAIR] = o2.astype(BF16)

    @pl.when(j == 0)
    def _():
        attn_rot(ATTN_HEAD_DIM ** -0.5)

    @pl.when(j == 1)
    def _():
        attn_rot(None)

    @pl.when(j == 3)
    def _():
        ret_rot(None)

    @pl.when(j == 4)
    def _():
        ret_rot(RET_HEAD_DIM ** -0.5)

    @pl.when((j == 2) | (j >= 5))
    def _():
        for hc in range(n_sub):
            z_ref[hc] = res[:, hc * PAIR:(hc + 1) * PAIR].astype(BF16)


def _inproj(x2, posb, mod3, ln0_g, ln0_b, w_in_b, fa, fr, seq, tm):
    t, d = x2.shape
    tn = d // 2
    n_sub = tn // PAIR
    per_b = seq // tm
    return pl.pallas_call(
        functools.partial(_inproj_kernel, n_sub=n_sub),
        grid=(t // tm, N_SEG),
        in_specs=[pl.BlockSpec((tm, d), lambda i, j: (i, 0)),
                  pl.BlockSpec((tm, LANES), lambda i, j: (i, 0)),
                  pl.BlockSpec((1, 6, d), lambda i, j: (i // per_b, 0, 0)),
                  pl.BlockSpec((1, d), lambda i, j: (0, 0)),
                  pl.BlockSpec((1, d), lambda i, j: (0, 0)),
                  pl.BlockSpec((d, tn), lambda i, j: (0, j)),
                  pl.BlockSpec((1, LANES), lambda i, j: (0, 0)),
                  pl.BlockSpec((1, LANES), lambda i, j: (0, 0))],
        out_specs=pl.BlockSpec((n_sub, tm, PAIR), lambda i, j: (j, i, 0)),
        out_shape=jax.ShapeDtypeStruct((N_SEG * n_sub, t, PAIR), BF16),
        scratch_shapes=[pltpu.VMEM((tm, d), BF16)] + [pltpu.VMEM((tm, LANES), F32)] * 5,
        compiler_params=_cparams(("arbitrary", "arbitrary")),
        name="inproj",
    )(x2, posb, mod3, ln0_g, ln0_b, w_in_b, fa, fr)


def _attn_kernel(q_ref, k_ref, v_ref, o_ref, l_ref, acc_ref, *, length):
    n_tiles = length // QT
    acc_ref[...] = jnp.zeros_like(acc_ref)
    rel = lax.broadcasted_iota(I32, (QT, KW), 0) - lax.broadcasted_iota(I32, (QT, KW), 1)
    lane = lax.broadcasted_iota(I32, (QT, LANES), 1)

    def body(n, carry):
        qs = pl.multiple_of(n * QT, QT)
        ks = pl.multiple_of(jnp.clip(n * QT - BAND, 0, length - KW), BAND)
        mask = jnp.abs(rel + (qs - ks)) <= BAND
        for hh in range(2):
            cols = slice(hh * ATTN_HEAD_DIM, (hh + 1) * ATTN_HEAD_DIM)
            q = q_ref[0, 0, pl.ds(qs, QT), cols]
            k = k_ref[0, 0, pl.ds(ks, KW), cols]
            v = v_ref[0, 0, pl.ds(ks, KW), cols]
            s = lax.dot_general(q, k, (((1,), (1,)), ((), ())), preferred_element_type=F32)
            s = jnp.where(mask, s, MASK_VALUE)
            m = jnp.max(s, axis=-1, keepdims=True)
            p = jnp.exp(s - m)
            den = jnp.sum(p, axis=-1, keepdims=True)
            o = jnp.dot(p.astype(BF16), v, preferred_element_type=F32) / den
            o_ref[0, 0, pl.ds(qs, QT), cols] = o.astype(BF16)
            lse = m + jnp.log(den)
            acc_ref[...] = jnp.where(lane == 2 * n + hh, lse, acc_ref[...])
        return carry

    lax.fori_loop(0, n_tiles, body, 0)
    l_ref[0, 0, 0] = acc_ref[...].T[:2 * n_tiles, :]


def _attn(z, batch, seq, dil, n_pair):
    length = seq // dil
    zv = z.reshape(z.shape[0], batch, length, dil * PAIR)
    blk = (1, 1, length, PAIR)
    n_tiles = length // QT
    o, lse = pl.pallas_call(
        functools.partial(_attn_kernel, length=length),
        grid=(batch, n_pair, dil),
        in_specs=[pl.BlockSpec(blk, lambda b, p, r: (p, b, 0, r)),
                  pl.BlockSpec(blk, lambda b, p, r: (n_pair + p, b, 0, r)),
                  pl.BlockSpec(blk, lambda b, p, r: (2 * n_pair + p, b, 0, r))],
        out_specs=[pl.BlockSpec(blk, lambda b, p, r: (p, b, 0, r)),
                   pl.BlockSpec((1, 1, 1, 2 * n_tiles, QT), lambda b, p, r: (b, p, r, 0, 0))],
        out_shape=[jax.ShapeDtypeStruct((n_pair, batch, length, dil * PAIR), BF16),
                   jax.ShapeDtypeStruct((batch, n_pair, dil, 2 * n_tiles, QT), F32)],
        scratch_shapes=[pltpu.VMEM((QT, LANES), F32)],
        compiler_params=_cparams(("arbitrary", "arbitrary", "arbitrary")),
        name=f"attn_d{dil}",
    )(zv, zv, zv)
    o = o.reshape(n_pair, batch * seq, PAIR)
    lse = lse.reshape(batch, n_pair, dil, n_tiles, 2, QT).transpose(0, 3, 5, 2, 1, 4)
    return o, lse.reshape(batch * seq, 2 * n_pair)


def _ret_kernel(lg_ref, q_ref, k_ref, v_ref, g_ref, o_ref, y_ref, sf_ref, sb_ref, *, n_tiles, tile):
    h = pl.program_id(1)
    t = pl.program_id(2)
    c = RET_CHUNK
    lgf = lg_ref[0, h]
    lgb = lg_ref[1, h]
    ii = lax.broadcasted_iota(I32, (c, c), 0)
    jj = lax.broadcasted_iota(I32, (c, c), 1)
    diff = (ii - jj).astype(F32)
    dmat = jnp.where(diff >= 0, jnp.exp(lgf * jnp.maximum(diff, 0.0)), 0.0) \
        + jnp.where(diff < 0, jnp.exp(lgb * jnp.maximum(-diff, 0.0)), 0.0)
    col = lax.broadcasted_iota(I32, (c, 1), 0).astype(F32)
    row = lax.broadcasted_iota(I32, (1, c), 1).astype(F32)
    n_chunks = tile // c

    @pl.when(t == 0)
    def _():
        sf_ref[...] = jnp.zeros_like(sf_ref)
        sb_ref[...] = jnp.zeros_like(sb_ref)

    @pl.when(t < n_tiles)
    def _():
        xi = jnp.exp(lgf * (col + 1.0))
        zeta = jnp.exp(lgf * (c - 1.0 - row))
        cdec = jnp.exp(lgf * jnp.full((1, 1), float(c), F32))
        base = t * tile

        def body(n, carry):
            r0 = pl.multiple_of(n * c, c)
            q = q_ref[0, pl.ds(r0, c), :]
            k = k_ref[0, pl.ds(r0, c), :]
            v = v_ref[0, pl.ds(r0, c), :]
            s = lax.dot_general(q, k, (((1,), (1,)), ((), ())), preferred_element_type=F32) * dmat
            y = jnp.dot(s.astype(BF16), v, preferred_element_type=F32)
            y = y + jnp.dot(q, sf_ref[...].astype(BF16), preferred_element_type=F32) * xi
            kt = (k.astype(F32).T * zeta).astype(BF16)
            sf_ref[...] = sf_ref[...] * cdec + jnp.dot(kt, v, preferred_element_type=F32)
            y_ref[pl.ds(pl.multiple_of(base + r0, c), c), :] = y
            return carry

        lax.fori_loop(0, n_chunks, body, 0)

    @pl.when(t >= n_tiles)
    def _():
        xi = jnp.exp(lgb * (c - col))
        zeta = jnp.exp(lgb * row)
        cdec = jnp.exp(lgb * jnp.full((1, 1), float(c), F32))
        base = (2 * n_tiles - 1 - t) * tile

        def body(m, carry):
            n = n_chunks - 1 - m
            r0 = pl.multiple_of(n * c, c)
            q = q_ref[0, pl.ds(r0, c), :]
            k = k_ref[0, pl.ds(r0, c), :]
            v = v_ref[0, pl.ds(r0, c), :]
            y = y_ref[pl.ds(pl.multiple_of(base + r0, c), c), :]
            y = y + jnp.dot(q, sb_ref[...].astype(BF16), preferred_element_type=F32) * xi
            kt = (k.astype(F32).T * zeta).astype(BF16)
            sb_ref[...] = sb_ref[...] * cdec + jnp.dot(kt, v, preferred_element_type=F32)
            mu = jnp.mean(y, axis=-1, keepdims=True)
            yc = y - mu
            var = jnp.mean(yc * yc, axis=-1, keepdims=True)
            yn = yc * lax.rsqrt(var + LN_EPS)
            gate = jax.nn.silu(g_ref[0, pl.ds(r0, c), :].astype(F32))
            o_ref[0, pl.ds(r0, c), :] = (gate * yn).astype(BF16)
            return carry

        lax.fori_loop(0, n_chunks, body, 0)


def _ret(z, lg, batch, seq, n_rh, seg0, tile):
    n_tiles = seq // tile
    per_b = seq // tile

    def rows(t):
        return jnp.where(t < n_tiles, t, 2 * n_tiles - 1 - t)

    def spec(seg):
        return pl.BlockSpec((1, tile, RET_HEAD_DIM), lambda b, h, t: (seg * n_rh + h, b * per_b + rows(t), 0))

    return pl.pallas_call(
        functools.partial(_ret_kernel, n_tiles=n_tiles, tile=tile),
        grid=(batch, n_rh, 2 * n_tiles),
        in_specs=[pl.BlockSpec(memory_space=pltpu.SMEM), spec(seg0), spec(seg0 + 1), spec(seg0 + 2), spec(seg0 + 3)],
        out_specs=pl.BlockSpec((1, tile, RET_HEAD_DIM),
                               lambda b, h, t: (h, b * per_b + jnp.where(t < n_tiles, n_tiles - 1, 2 * n_tiles - 1 - t), 0)),
        out_shape=jax.ShapeDtypeStruct((n_rh, batch * seq, RET_HEAD_DIM), BF16),
        scratch_shapes=[pltpu.VMEM((seq, RET_HEAD_DIM), F32),
                        pltpu.VMEM((RET_HEAD_DIM, RET_HEAD_DIM), F32),
                        pltpu.VMEM((RET_HEAD_DIM, RET_HEAD_DIM), F32)],
        compiler_params=_cparams(("arbitrary", "arbitrary", "arbitrary")),
        name="ret",
    )(lg, z, z, z, z)


def _outproj_kernel(o1_ref, o2_ref, o3_ref, l1_ref, l2_ref, l3_ref, r_ref, x_ref, mod_ref,
                    g0_ref, b0_ref, g1_ref, b1_ref, w_ref, x1_ref, h2_ref, *, n_pair, n_rh, alpha):
    l1, l2, l3 = l1_ref[...], l2_ref[...], l3_ref[...]
    mx = jnp.maximum(jnp.maximum(l1, l2), l3)
    e1, e2, e3 = jnp.exp(l1 - mx), jnp.exp(l2 - mx), jnp.exp(l3 - mx)
    tot = e1 + e2 + e3
    w1, w2, w3 = e1 / tot, e2 / tot, e3 / tot
    tm = x_ref.shape[0]
    acc = jnp.zeros((tm, w_ref.shape[1]), F32)
    for p in range(n_pair):
        parts = []
        for hh in range(2):
            hd = 2 * p + hh
            cols = slice(hh * ATTN_HEAD_DIM, (hh + 1) * ATTN_HEAD_DIM)
            a = (w1[:, hd:hd + 1] * o1_ref[p, :, cols].astype(F32)
                 + w2[:, hd:hd + 1] * o2_ref[p, :, cols].astype(F32)
                 + w3[:, hd:hd + 1] * o3_ref[p, :, cols].astype(F32))
            parts.append(a.astype(BF16))
        a2 = jnp.concatenate(parts, axis=-1)
        acc = acc + jnp.dot(a2, w_ref[p * PAIR:(p + 1) * PAIR, :], preferred_element_type=F32)
    base = n_pair * PAIR
    for hr in range(n_rh):
        acc = acc + jnp.dot(r_ref[hr], w_ref[base + hr * RET_HEAD_DIM:base + (hr + 1) * RET_HEAD_DIM, :],
                            preferred_element_type=F32)
    xn = _ln(x_ref[...], g0_ref[...], b0_ref[...])
    y = alpha * xn + (1.0 + mod_ref[0, 2:3, :]) * acc
    x1 = _ln(y, g1_ref[...], b1_ref[...])
    x1_ref[...] = x1
    h2_ref[...] = x1 * (1.0 + mod_ref[0, 4:5, :]) + mod_ref[0, 3:4, :]


def _outproj(o1, o2, o3, l1, l2, l3, r, x2, mod3, g0, b0, g1, b1, w_out_b, seq, tm, alpha):
    t, d = x2.shape
    n_pair, n_rh = o1.shape[0], r.shape[0]
    per_b = seq // tm
    nh = 2 * n_pair
    row = lambda i: (i, 0)
    vec = pl.BlockSpec((1, d), lambda i: (0, 0))
    return pl.pallas_call(
        functools.partial(_outproj_kernel, n_pair=n_pair, n_rh=n_rh, alpha=alpha),
        grid=(t // tm,),
        in_specs=[pl.BlockSpec((n_pair, tm, PAIR), lambda i: (0, i, 0))] * 3
        + [pl.BlockSpec((tm, nh), row)] * 3
        + [pl.BlockSpec((n_rh, tm, RET_HEAD_DIM), lambda i: (0, i, 0)),
           pl.BlockSpec((tm, d), row),
           pl.BlockSpec((1, 6, d), lambda i: (i // per_b, 0, 0)),
           vec, vec, vec, vec,
           pl.BlockSpec((d, d), lambda i: (0, 0))],
        out_specs=[pl.BlockSpec((tm, d), row), pl.BlockSpec((tm, d), row)],
        out_shape=[jax.ShapeDtypeStruct((t, d), F32), jax.ShapeDtypeStruct((t, d), F32)],
        compiler_params=_cparams(("arbitrary",)),
        name="outproj",
    )(o1, o2, o3, l1, l2, l3, r, x2, mod3, g0, b0, g1, b1, w_out_b)


def _first_argmax(rows):
    best, idx = rows[0], jnp.zeros(rows[0].shape, I32)
    for e in range(1, len(rows)):
        better = rows[e] > best
        idx = jnp.where(better, e, idx)
        best = jnp.maximum(best, rows[e])
    return best, idx


def _router_kernel(h_ref, w_ref, b_ref, u_ref, eid_ref, wt_ref, rank_ref, cnt_ref, carry_ref):
    i = pl.program_id(0)

    @pl.when(i == 0)
    def _():
        carry_ref[...] = jnp.zeros_like(carry_ref)

    lt = lax.dot_general(w_ref[...], h_ref[...].astype(BF16), (((1,), (1,)), ((), ())),
                         preferred_element_type=F32) + b_ref[...]
    grow = [lt[g:g + 1, :] for g in range(N_GROUPS)]
    gmax, gsel = _first_argmax(grow)
    gsum = grow[0] * 0.0
    for g in range(N_GROUPS):
        gsum = gsum + jnp.exp(grow[g] - gmax)
    pg = 1.0 / gsum
    srow = []
    for e in range(EXPERTS_PER_GROUP):
        r = lt[N_GROUPS + e:N_GROUPS + e + 1, :]
        for g in range(1, N_GROUPS):
            r = jnp.where(gsel == g, lt[N_GROUPS + g * EXPERTS_PER_GROUP + e:N_GROUPS + g * EXPERTS_PER_GROUP + e + 1, :], r)
        srow.append(r)
    v1, i1 = _first_argmax(srow)
    v2, i2 = _first_argmax([jnp.where(i1 == e, -jnp.inf, srow[e]) for e in range(EXPERTS_PER_GROUP)])
    e2 = jnp.exp(v2 - v1)
    den = 1.0 + e2
    wt_ref[0:1, :] = (1.0 / den) * pg
    wt_ref[1:2, :] = (e2 / den) * pg
    eid0 = gsel * EXPERTS_PER_GROUP + i1
    eid1 = gsel * EXPERTS_PER_GROUP + i2
    eid_ref[0:1, :] = eid0
    eid_ref[1:2, :] = eid1
    tm = lt.shape[1]
    erow = lax.broadcasted_iota(I32, (N_EXPERTS, tm), 0)
    oh0 = (erow == eid0).astype(F32)
    oh1 = (erow == eid1).astype(F32)
    oh = oh0 + oh1
    incl = jnp.dot(oh.astype(BF16), u_ref[...], preferred_element_type=F32)
    before = carry_ref[:, 0:1] + incl - oh
    rank_ref[0:1, :] = jnp.sum(oh0 * before, axis=0, keepdims=True).astype(I32)
    rank_ref[1:2, :] = jnp.sum(oh1 * before, axis=0, keepdims=True).astype(I32)
    carry = carry_ref[...] + jnp.sum(oh, axis=1, keepdims=True)
    carry_ref[...] = carry
    cnt_ref[...] = carry


def _router(h2, wr, br, tm):
    t, d = h2.shape
    tri = (lax.broadcasted_iota(I32, (tm, tm), 0) <= lax.broadcasted_iota(I32, (tm, tm), 1)).astype(BF16)
    tok = lambda i: (0, i)
    return pl.pallas_call(
        _router_kernel,
        grid=(t // tm,),
        in_specs=[pl.BlockSpec((tm, d), lambda i: (i, 0)),
                  pl.BlockSpec((ROUTER_ROWS, d), lambda i: (0, 0)),
                  pl.BlockSpec((ROUTER_ROWS, 1), lambda i: (0, 0)),
                  pl.BlockSpec((tm, tm), lambda i: (0, 0))],
        out_specs=[pl.BlockSpec((2, tm), tok), pl.BlockSpec((2, tm), tok), pl.BlockSpec((2, tm), tok),
                   pl.BlockSpec((N_EXPERTS, LANES), lambda i: (0, 0))],
        out_shape=[jax.ShapeDtypeStruct((2, t), I32), jax.ShapeDtypeStruct((2, t), F32),
                   jax.ShapeDtypeStruct((2, t), I32), jax.ShapeDtypeStruct((N_EXPERTS, LANES), F32)],
        scratch_shapes=[pltpu.VMEM((N_EXPERTS, LANES), F32)],
        compiler_params=_cparams(("arbitrary",)),
        name="router",
    )(h2, wr, br, tri)


def _gather_rows(idx_ref, idx_base, src_hbm, dst, sem, n_rows):
    def body(r, carry):
        row = idx_ref[idx_base + (r,)]
        pltpu.make_async_copy(src_hbm.at[pl.ds(row, 1), :], dst.at[pl.ds(r, 1), :], sem).start()
        return carry
    lax.fori_loop(0, n_rows, body, 0, unroll=8)


def _wait_rows(src_hbm, dst, sem, n_rows):
    def body(r, carry):
        pltpu.make_async_copy(src_hbm.at[pl.ds(0, 1), :], dst.at[pl.ds(r, 1), :], sem).wait()
        return carry
    lax.fori_loop(0, n_rows, body, 0, unroll=8)


def _expert_kernel(be_ref, tok_ref, nxt_ref, h_hbm, w1_ref, w3_ref, w2_ref, y_ref, xbuf, sem):
    i = pl.program_id(0)
    n = pl.num_programs(0)
    slot = i % 2

    @pl.when(i == 0)
    def _():
        _gather_rows(tok_ref, (0, 0), h_hbm, xbuf.at[0], sem.at[0], MOE_BLOCK)

    @pl.when(i + 1 < n)
    def _():
        _gather_rows(nxt_ref, (0, 0), h_hbm, xbuf.at[1 - slot], sem.at[1 - slot], MOE_BLOCK)

    _wait_rows(h_hbm, xbuf.at[slot], sem.at[slot], MOE_BLOCK)
    x = xbuf[slot].astype(BF16)
    a = jnp.dot(x, w1_ref[0], preferred_element_type=F32)
    b = jnp.dot(x, w3_ref[0], preferred_element_type=F32)
    mid = (jax.nn.silu(a) * b).astype(BF16)
    y_ref[...] = jnp.dot(mid, w2_ref[0], preferred_element_type=F32)


def _experts(blk_e, buf_tok, h2, w1b, w3b, w2b):
    t, d = h2.shape
    ff = w1b.shape[2]
    nblk = buf_tok.shape[0]
    smem_blk = lambda f: pl.BlockSpec((1, 1, MOE_BLOCK), f, memory_space=pltpu.SMEM)
    grid_spec = pltpu.PrefetchScalarGridSpec(
        num_scalar_prefetch=1,
        grid=(nblk,),
        in_specs=[smem_blk(lambda i, be: (i, 0, 0)),
                  smem_blk(lambda i, be: (jnp.minimum(i + 1, nblk - 1), 0, 0)),
                  pl.BlockSpec(memory_space=pl.ANY),
                  pl.BlockSpec((1, d, ff), lambda i, be: (be[i], 0, 0)),
                  pl.BlockSpec((1, d, ff), lambda i, be: (be[i], 0, 0)),
                  pl.BlockSpec((1, ff, d), lambda i, be: (be[i], 0, 0))],
        out_specs=pl.BlockSpec((MOE_BLOCK, d), lambda i, be: (i, 0)),
        scratch_shapes=[pltpu.VMEM((2, MOE_BLOCK, d), F32), pltpu.SemaphoreType.DMA((2,))])
    return pl.pallas_call(
        _expert_kernel,
        grid_spec=grid_spec,
        out_shape=jax.ShapeDtypeStruct((nblk * MOE_BLOCK, d), F32),
        compiler_params=_cparams(("arbitrary",)),
        name="experts",
    )(blk_e, buf_tok, buf_tok, h2, w1b, w3b, w2b)


def _combine_kernel(dst_ref, nxt_ref, y_hbm, wt_ref, x1_ref, mod_ref, g_ref, b_ref, o_ref, ybuf, sem, *, alpha):
    i = pl.program_id(0)
    n = pl.num_programs(0)
    slot = i % 2
    tm = x1_ref.shape[0]

    def start(ref, s):
        for k in range(2):
            _gather_rows(ref, (0, k), y_hbm, ybuf.at[s, k], sem.at[s], tm)

    @pl.when(i == 0)
    def _():
        start(dst_ref, 0)

    @pl.when(i + 1 < n)
    def _():
        start(nxt_ref, 1 - slot)

    for k in range(2):
        _wait_rows(y_hbm, ybuf.at[slot, k], sem.at[slot], tm)
    wt = wt_ref[...]
    ffn = wt[:, 0:1] * ybuf[slot, 0] + wt[:, 1:2] * ybuf[slot, 1]
    y = alpha * x1_ref[...] + (1.0 + mod_ref[0, 5:6, :]) * ffn
    o_ref[...] = _ln(y, g_ref[...], b_ref[...])


def _combine(dest3, yb, wt_t, x1, mod3, g2, b2, seq, tm, alpha):
    t, d = x1.shape
    n = t // tm
    per_b = seq // tm
    smem_blk = lambda f: pl.BlockSpec((1, 2, tm), f, memory_space=pltpu.SMEM)
    vec = pl.BlockSpec((1, d), lambda i: (0, 0))
    return pl.pallas_call(
        functools.partial(_combine_kernel, alpha=alpha),
        grid=(n,),
        in_specs=[smem_blk(lambda i: (i, 0, 0)),
                  smem_blk(lambda i: (jnp.minimum(i + 1, n - 1), 0, 0)),
                  pl.BlockSpec(memory_space=pl.ANY),
                  pl.BlockSpec((tm, 2), lambda i: (i, 0)),
                  pl.BlockSpec((tm, d), lambda i: (i, 0)),
                  pl.BlockSpec((1, 6, d), lambda i: (i // per_b, 0, 0)),
                  vec, vec],
        out_specs=pl.BlockSpec((tm, d), lambda i: (i, 0)),
        out_shape=jax.ShapeDtypeStruct((t, d), F32),
        scratch_shapes=[pltpu.VMEM((2, 2, tm, d), F32), pltpu.SemaphoreType.DMA((2,))],
        compiler_params=_cparams(("arbitrary",)),
        name="combine",
    )(dest3, dest3, yb, wt_t, x1, mod3, g2, b2)


def _pick_tile(seq, want):
    tm = min(seq, want)
    assert seq % tm == 0
    return tm


def kernel(x, c, positions, ln0_g, ln0_b, w_ada, b_ada, w_in, w_out, ret_log_decay_f, ret_log_decay_b,
           ln1_g, ln1_b, w_group, b_group, w_sub, b_sub, w1, w3, w2, ln2_g, ln2_b):
    batch, seq, d = x.shape
    depth = w_ada.shape[0]
    t = batch * seq
    a_width = d // 2
    n_pair = a_width // PAIR
    n_rh = (d - a_width) // RET_HEAD_DIM
    assert depth == 1 and d % (2 * PAIR) == 0 and seq % (max(DILATIONS) * KW) == 0 and batch <= 8
    alpha = (2 * depth) ** 0.25

    inv_rope = ROPE_THETA ** (-jnp.arange(0, ROPE_DIM, 2, dtype=F32) / ROPE_DIM)
    inv_ret = RET_THETA ** (-jnp.linspace(0.0, 1.0, RET_HEAD_DIM // 2, dtype=F32))
    fa = jnp.zeros((1, LANES), F32).at[0, :ROPE_DIM].set(jnp.concatenate([inv_rope, inv_rope]))
    fr = inv_ret.reshape(1, LANES)
    posb = jnp.broadcast_to(positions.astype(F32).reshape(t, 1), (t, LANES))
    c8 = jnp.zeros((8, d), F32).at[:batch].set(c)
    row = lambda v: v.reshape(1, d)

    xs = x.reshape(t, d)
    mod = _ada(c8, w_ada[0], b_ada[0].reshape(1, -1))
    mod3 = mod[:batch].reshape(batch, 6, d)
    z = _inproj(xs, posb, mod3, row(ln0_g), row(ln0_b), w_in[0].astype(BF16), fa, fr, seq, _pick_tile(seq, 512))
    outs = [_attn(z, batch, seq, dil, n_pair) for dil in DILATIONS]
    lg = jnp.stack([ret_log_decay_f[0], ret_log_decay_b[0]]).astype(F32)
    r = _ret(z, lg, batch, seq, n_rh, 3, _pick_tile(seq, 1024))
    x1, h2 = _outproj(outs[0][0], outs[1][0], outs[2][0], outs[0][1], outs[1][1], outs[2][1], r, xs, mod3,
                      row(ln0_g), row(ln0_b), row(ln1_g[0]), row(ln1_b[0]), w_out[0].astype(BF16),
                      seq, _pick_tile(seq, 256), alpha)
    wr = jnp.zeros((ROUTER_ROWS, d), F32)
    wr = wr.at[:N_GROUPS].set(w_group[0].T)
    wr = wr.at[N_GROUPS:N_GROUPS + N_EXPERTS].set(w_sub[0].transpose(0, 2, 1).reshape(N_EXPERTS, d))
    br = jnp.zeros((ROUTER_ROWS, 1), F32)
    br = br.at[:N_GROUPS, 0].set(b_group[0]).at[N_GROUPS:N_GROUPS + N_EXPERTS, 0].set(b_sub[0].reshape(-1))
    eid, wt, rank, cnt = _router(h2, wr.astype(BF16), br, _pick_tile(t, 512))
    counts = cnt[:, 0].astype(I32)
    padded = (counts + MOE_BLOCK - 1) // MOE_BLOCK * MOE_BLOCK
    pend = jnp.cumsum(padded)
    pstart = pend - padded
    dest = pstart[eid] + rank
    n_slots = 2 * t + N_EXPERTS * MOE_BLOCK
    nblk = n_slots // MOE_BLOCK
    tok = jnp.broadcast_to(jnp.arange(t, dtype=I32), (2, t))
    buf_tok = jnp.zeros((n_slots,), I32).at[dest.reshape(-1)].set(tok.reshape(-1))
    blk_e = jnp.minimum(jnp.searchsorted(pend, jnp.arange(nblk, dtype=I32) * MOE_BLOCK, side='right'),
                        N_EXPERTS - 1).astype(I32)
    yb = _experts(blk_e, buf_tok.reshape(nblk, 1, MOE_BLOCK), h2,
                  w1[0].astype(BF16), w3[0].astype(BF16), w2[0].astype(BF16))
    tmc = _pick_tile(seq, 256)
    dest3 = dest.reshape(2, t // tmc, tmc).transpose(1, 0, 2)
    out = _combine(dest3, yb, wt.T, x1, mod3, row(ln2_g[0]), row(ln2_b[0]), seq, tmc, alpha)
    return out.reshape(batch, seq, d)
```

```python
import functools

import jax
import jax.numpy as jnp
from jax import lax
from jax.experimental import pallas as pl
from jax.experimental.pallas import tpu as pltpu

F32 = jnp.float32
BF16 = jnp.bfloat16
I32 = jnp.int32
U32 = jnp.uint32

LANES = 128
ATTN_HEAD_DIM = 128
RET_HEAD_DIM = 256
DILATIONS = (1, 4, 16)
BAND = 64
QT = 128
KW = QT + 2 * BAND
CHAINS = 8
PERM_ROWS = 256
PERM_UNROLL = 2
ROPE_THETA = 500000.0
ROPE_DIM = ATTN_HEAD_DIM // 4
RET_THETA = 10000.0
RET_CHUNK = 128
N_GROUPS = 4
EXPERTS_PER_GROUP = 8
N_EXPERTS = N_GROUPS * EXPERTS_PER_GROUP
MOE_BLOCK = 128
LN_EPS = 1e-5
MASK_VALUE = -1e30
N_SEG = 7
ROUTER_ROWS = 48
HI_MASK = 0xFFFF0000
VMEM_LIMIT = 56 * 1024 * 1024


def _cparams(sem, vmem=VMEM_LIMIT):
    return pltpu.CompilerParams(dimension_semantics=sem, vmem_limit_bytes=vmem)


def _ln(x, g, b):
    mu = jnp.mean(x, axis=-1, keepdims=True)
    xc = x - mu
    var = jnp.mean(xc * xc, axis=-1, keepdims=True)
    return xc * lax.rsqrt(var + LN_EPS) * g + b


def _pack_bf16_pairs(h):
    bits = lax.bitcast_convert_type(h.astype(BF16).astype(F32), U32)
    n = h.shape[1] // 2
    return (bits[:, :n] >> 16) | (bits[:, n:] & jnp.uint32(HI_MASK))


def _unpack_bf16_pairs(p):
    lo = lax.bitcast_convert_type(p << 16, F32)
    hi = lax.bitcast_convert_type(p & jnp.uint32(HI_MASK), F32)
    return jnp.concatenate([lo, hi], axis=-1).astype(BF16)


def _ada_kernel(c_ref, w_ref, b_ref, o_ref):
    cs = jax.nn.silu(c_ref[...])
    o_ref[...] = jnp.dot(cs.astype(BF16), w_ref[...].astype(BF16), preferred_element_type=F32) + b_ref[...]


def _ada(c8, w_ada, b_ada):
    d, n = w_ada.shape
    tn = min(n, 512)
    return pl.pallas_call(
        _ada_kernel,
        grid=(n // tn,),
        in_specs=[pl.BlockSpec((8, d), lambda j: (0, 0)),
                  pl.BlockSpec((d, tn), lambda j: (0, j)),
                  pl.BlockSpec((1, tn), lambda j: (0, j))],
        out_specs=pl.BlockSpec((8, tn), lambda j: (0, j)),
        out_shape=jax.ShapeDtypeStruct((8, n), F32),
        compiler_params=_cparams(("arbitrary",)),
        name="ada",
    )(c8, w_ada, b_ada)


def _inproj_kernel(x_ref, pos_ref, mod_ref, g_ref, b_ref, w_ref, fa_ref, fr_ref, za_ref, zr_ref,
                   h_ref, ca_ref, sa1_ref, sa2_ref, cr_ref, sr_ref, *, n_ah, n_rh):
    j = pl.program_id(1)

    @pl.when(j == 0)
    def _():
        xn = _ln(x_ref[...], g_ref[...], b_ref[...])
        h = xn * (1.0 + mod_ref[0, 1:2, :]) + mod_ref[0, 0:1, :]
        h_ref[...] = h.astype(BF16)
        pos = pos_ref[...]
        ang = pos * fa_ref[...]
        c, s = jnp.cos(ang), jnp.sin(ang)
        lane = lax.broadcasted_iota(I32, ang.shape, 1)
        half = ROPE_DIM // 2
        ca_ref[...] = c
        sa1_ref[...] = jnp.where(lane < half, -s, 0.0)
        sa2_ref[...] = jnp.where((lane >= half) & (lane < 2 * half), s, 0.0)
        angr = pos * fr_ref[...]
        cr_ref[...] = jnp.cos(angr)
        sr_ref[...] = jnp.sin(angr)

    res = jnp.dot(h_ref[...], w_ref[...], preferred_element_type=F32)

    def attn_rot(scale):
        ca, sa1, sa2 = ca_ref[...], sa1_ref[...], sa2_ref[...]
        half = ROPE_DIM // 2
        for hc in range(n_ah):
            xh = res[:, hc * LANES:(hc + 1) * LANES]
            r = xh * ca + pltpu.roll(xh, LANES - half, 1) * sa1 + pltpu.roll(xh, half, 1) * sa2
            if scale is not None:
                r = r * scale
            za_ref[hc] = r.astype(BF16)

    def ret_rot(scale):
        cr, sr = cr_ref[...], sr_ref[...]
        for hc in range(n_rh):
            x1 = res[:, hc * RET_HEAD_DIM:hc * RET_HEAD_DIM + LANES]
            x2 = res[:, hc * RET_HEAD_DIM + LANES:(hc + 1) * RET_HEAD_DIM]
            o1 = x1 * cr - x2 * sr
            o2 = x2 * cr + x1 * sr
            if scale is not None:
                o1, o2 = o1 * scale, o2 * scale
            zr_ref[hc, :, 0:LANES] = o1.astype(BF16)
            zr_ref[hc, :, LANES:RET_HEAD_DIM] = o2.astype(BF16)

    @pl.when(j == 0)
    def _():
        attn_rot(ATTN_HEAD_DIM ** -0.5)

    @pl.when(j == 1)
    def _():
        attn_rot(None)

    @pl.when(j == 2)
    def _():
        for hc in range(n_ah):
            za_ref[hc] = res[:, hc * LANES:(hc + 1) * LANES].astype(BF16)

    @pl.when(j == 3)
    def _():
        ret_rot(None)

    @pl.when(j == 4)
    def _():
        ret_rot(RET_HEAD_DIM ** -0.5)

    @pl.when(j >= 5)
    def _():
        for hc in range(n_rh):
            zr_ref[hc] = res[:, hc * RET_HEAD_DIM:(hc + 1) * RET_HEAD_DIM].astype(BF16)


def _inproj(x2, posb, mod3, ln0_g, ln0_b, w_in_b, fa, fr, seq, tm):
    t, d = x2.shape
    tn = d // 2
    n_ah, n_rh = tn // ATTN_HEAD_DIM, tn // RET_HEAD_DIM
    per_b = seq // tm
    return pl.pallas_call(
        functools.partial(_inproj_kernel, n_ah=n_ah, n_rh=n_rh),
        grid=(t // tm, N_SEG),
        in_specs=[pl.BlockSpec((tm, d), lambda i, j: (i, 0)),
                  pl.BlockSpec((tm, LANES), lambda i, j: (i, 0)),
                  pl.BlockSpec((1, 6, d), lambda i, j: (i // per_b, 0, 0)),
                  pl.BlockSpec((1, d), lambda i, j: (0, 0)),
                  pl.BlockSpec((1, d), lambda i, j: (0, 0)),
                  pl.BlockSpec((d, tn), lambda i, j: (0, j)),
                  pl.BlockSpec((1, LANES), lambda i, j: (0, 0)),
                  pl.BlockSpec((1, LANES), lambda i, j: (0, 0))],
        out_specs=[pl.BlockSpec((n_ah, tm, ATTN_HEAD_DIM), lambda i, j: (jnp.minimum(j, 2), i, 0)),
                   pl.BlockSpec((n_rh, tm, RET_HEAD_DIM), lambda i, j: (jnp.maximum(j - 3, 0), i, 0))],
        out_shape=[jax.ShapeDtypeStruct((3 * n_ah, t, ATTN_HEAD_DIM), BF16),
                   jax.ShapeDtypeStruct((4 * n_rh, t, RET_HEAD_DIM), BF16)],
        scratch_shapes=[pltpu.VMEM((tm, d), BF16)] + [pltpu.VMEM((tm, LANES), F32)] * 5,
        compiler_params=_cparams(("arbitrary", "arbitrary")),
        name="inproj",
    )(x2, posb, mod3, ln0_g, ln0_b, w_in_b, fa, fr)


def _attn_tile(q, k, v, bias):
    s = lax.dot_general(q, k, (((1,), (1,)), ((), ())), preferred_element_type=F32) + bias
    m = jnp.max(jnp.maximum(s[:, :LANES], s[:, LANES:]), axis=-1, keepdims=True)
    p = jnp.exp(s - m)
    den = jnp.sum(p[:, :LANES] + p[:, LANES:], axis=-1, keepdims=True)
    acc = jnp.dot(p.astype(BF16), v, preferred_element_type=F32)
    return acc, jnp.broadcast_to(m, acc.shape), jnp.broadcast_to(den, acc.shape)


def _attn_kernel(q_ref, k_ref, v_ref, bias_ref, p4_ref, p16_ref, o_ref,
                 acc_ref, m_ref, d_ref, qp_ref, kp_ref, vp_ref, *, seq):
    n_items = seq // QT

    def window(n, length):
        qs = n * QT
        ks = jnp.clip(qs - BAND, 0, length - KW)
        return qs, ks, bias_ref[lax.div(qs - ks, BAND)]

    def perm_body(it, carry):
        for u in range(PERM_UNROLL):
            blk = it * PERM_UNROLL + u
            r0 = pl.multiple_of(blk * PERM_ROWS, PERM_ROWS)
            qk = jnp.concatenate([q_ref[0, 0, pl.ds(r0, PERM_ROWS), :], k_ref[0, 0, pl.ds(r0, PERM_ROWS), :]], axis=-1)
            v = v_ref[0, 0, pl.ds(r0, PERM_ROWS), :]
            for slot, (dil, p_ref) in enumerate(((4, p4_ref), (16, p16_ref))):
                length, w = seq // dil, PERM_ROWS // dil
                qkp = jnp.dot(p_ref[...], qk, preferred_element_type=F32).astype(BF16)
                vp = jnp.dot(p_ref[...], v, preferred_element_type=F32).astype(BF16)
                for r in range(dil):
                    rows = pl.ds(pl.multiple_of(r * length + blk * w, w), w)
                    qp_ref[slot, rows, :] = qkp[r * w:(r + 1) * w, :ATTN_HEAD_DIM]
                    kp_ref[slot, rows, :] = qkp[r * w:(r + 1) * w, ATTN_HEAD_DIM:]
                    vp_ref[slot, rows, :] = vp[r * w:(r + 1) * w, :]
        return carry

    lax.fori_loop(0, seq // PERM_ROWS // PERM_UNROLL, perm_body, 0)

    def dilated_tile(item, dil, slot):
        length = seq // dil
        n, r = lax.div(item, dil), lax.rem(item, dil)
        qs, ks, bias = window(n, length)
        qrow = pl.multiple_of(r * length + qs, QT)
        krow = pl.multiple_of(r * length + ks, BAND)
        out = _attn_tile(qp_ref[slot, pl.ds(qrow, QT), :], kp_ref[slot, pl.ds(krow, KW), :],
                         vp_ref[slot, pl.ds(krow, KW), :], bias)
        return out, pl.ds(qs * dil + r, QT, stride=dil)

    def body16(it, carry):
        for g in range(CHAINS):
            (acc, m, den), rows = dilated_tile(it * CHAINS + g, 16, 1)
            acc_ref[rows, :] = acc
            m_ref[rows, :] = m
            d_ref[rows, :] = den
        return carry

    lax.fori_loop(0, n_items // CHAINS, body16, 0)

    def body4(it, carry):
        for g in range(CHAINS):
            (acc, m, den), rows = dilated_tile(it * CHAINS + g, 4, 0)
            m0 = m_ref[rows, :]
            mn = jnp.maximum(m0, m)
            a, b = jnp.exp(m0 - mn), jnp.exp(m - mn)
            acc_ref[rows, :] = a * acc_ref[rows, :] + b * acc
            d_ref[rows, :] = a * d_ref[rows, :] + b * den
            m_ref[rows, :] = mn
        return carry

    lax.fori_loop(0, n_items // CHAINS, body4, 0)

    def body1(it, carry):
        for g in range(CHAINS):
            qs, ks, bias = window(it * CHAINS + g, seq)
            qs, ks = pl.multiple_of(qs, QT), pl.multiple_of(ks, BAND)
            acc, m, den = _attn_tile(q_ref[0, 0, pl.ds(qs, QT), :], k_ref[0, 0, pl.ds(ks, KW), :],
                                     v_ref[0, 0, pl.ds(ks, KW), :], bias)
            rows = pl.ds(qs, QT)
            m0 = m_ref[rows, :]
            mn = jnp.maximum(m0, m)
            a, b = jnp.exp(m0 - mn), jnp.exp(m - mn)
            num = a * acc_ref[rows, :] + b * acc
            o_ref[0, 0, rows, :] = (num / (a * d_ref[rows, :] + b * den)).astype(BF16)
        return carry

    lax.fori_loop(0, n_items // CHAINS, body1, 0)


def _perm_matrix(dil):
    w = PERM_ROWS // dil
    i = jnp.arange(PERM_ROWS)
    src = (i % w) * dil + i // w
    return (src[:, None] == jnp.arange(PERM_ROWS)[None, :]).astype(BF16)


def _attn(za, batch, seq, n_ah):
    zv = za.reshape(3 * n_ah, batch, seq, ATTN_HEAD_DIM)
    blk = (1, 1, seq, ATTN_HEAD_DIM)
    i = jnp.arange(QT)[:, None]
    jj = jnp.arange(KW)[None, :]
    bias = jnp.stack([jnp.where(jnp.abs(i + o * BAND - jj) <= BAND, 0.0, MASK_VALUE) for o in range(3)]).astype(F32)
    const = lambda shape: pl.BlockSpec(shape, lambda b, h: (0,) * len(shape))
    o = pl.pallas_call(
        functools.partial(_attn_kernel, seq=seq),
        grid=(batch, n_ah),
        in_specs=[pl.BlockSpec(blk, lambda b, h: (h, b, 0, 0)),
                  pl.BlockSpec(blk, lambda b, h: (n_ah + h, b, 0, 0)),
                  pl.BlockSpec(blk, lambda b, h: (2 * n_ah + h, b, 0, 0)),
                  const((3, QT, KW)), const((PERM_ROWS, PERM_ROWS)), const((PERM_ROWS, PERM_ROWS))],
        out_specs=pl.BlockSpec(blk, lambda b, h: (h, b, 0, 0)),
        out_shape=jax.ShapeDtypeStruct((n_ah, batch, seq, ATTN_HEAD_DIM), BF16),
        scratch_shapes=[pltpu.VMEM((seq, LANES), F32)] * 3
        + [pltpu.VMEM((2, seq, ATTN_HEAD_DIM), BF16)] * 3,
        compiler_params=_cparams(("arbitrary", "arbitrary")),
        name="attn",
    )(zv, zv, zv, bias, _perm_matrix(4), _perm_matrix(16))
    return o.reshape(n_ah, batch * seq, ATTN_HEAD_DIM)


def _ret_kernel(lg_ref, q_ref, k_ref, v_ref, g_ref, o_ref, y_ref, sf_ref, sb_ref, *, n_tiles, tile):
    h = pl.program_id(1)
    t = pl.program_id(2)
    c = RET_CHUNK
    lgf = lg_ref[0, h]
    lgb = lg_ref[1, h]
    ii = lax.broadcasted_iota(I32, (c, c), 0)
    jj = lax.broadcasted_iota(I32, (c, c), 1)
    diff = (ii - jj).astype(F32)
    dmat = jnp.where(diff >= 0, jnp.exp(lgf * jnp.maximum(diff, 0.0)), 0.0) \
        + jnp.where(diff < 0, jnp.exp(lgb * jnp.maximum(-diff, 0.0)), 0.0)
    col = lax.broadcasted_iota(I32, (c, 1), 0).astype(F32)
    row = lax.broadcasted_iota(I32, (1, c), 1).astype(F32)
    n_chunks = tile // c

    @pl.when(t == 0)
    def _():
        sf_ref[...] = jnp.zeros_like(sf_ref)
        sb_ref[...] = jnp.zeros_like(sb_ref)

    @pl.when(t < n_tiles)
    def _():
        xi = jnp.exp(lgf * (col + 1.0))
        zeta = jnp.exp(lgf * (c - 1.0 - row))
        cdec = jnp.exp(lgf * jnp.full((1, 1), float(c), F32))
        base = t * tile

        def body(n, carry):
            r0 = pl.multiple_of(n * c, c)
            q = q_ref[0, pl.ds(r0, c), :]
            k = k_ref[0, pl.ds(r0, c), :]
            v = v_ref[0, pl.ds(r0, c), :]
            s = lax.dot_general(q, k, (((1,), (1,)), ((), ())), preferred_element_type=F32) * dmat
            y = jnp.dot(s.astype(BF16), v, preferred_element_type=F32)
            y = y + jnp.dot(q, sf_ref[...].astype(BF16), preferred_element_type=F32) * xi
            kt = (k.astype(F32).T * zeta).astype(BF16)
            sf_ref[...] = sf_ref[...] * cdec + jnp.dot(kt, v, preferred_element_type=F32)
            y_ref[pl.ds(pl.multiple_of(base + r0, c), c), :] = y
            return carry

        lax.fori_loop(0, n_chunks, body, 0)

    @pl.when(t >= n_tiles)
    def _():
        xi = jnp.exp(lgb * (c - col))
        zeta = jnp.exp(lgb * row)
        cdec = jnp.exp(lgb * jnp.full((1, 1), float(c), F32))
        base = (2 * n_tiles - 1 - t) * tile

        def body(m, carry):
            n = n_chunks - 1 - m
            r0 = pl.multiple_of(n * c, c)
            q = q_ref[0, pl.ds(r0, c), :]
            k = k_ref[0, pl.ds(r0, c), :]
            v = v_ref[0, pl.ds(r0, c), :]
            y = y_ref[pl.ds(pl.multiple_of(base + r0, c), c), :]
            y = y + jnp.dot(q, sb_ref[...].astype(BF16), preferred_element_type=F32) * xi
            kt = (k.astype(F32).T * zeta).astype(BF16)
            sb_ref[...] = sb_ref[...] * cdec + jnp.dot(kt, v, preferred_element_type=F32)
            mu = jnp.mean(y, axis=-1, keepdims=True)
            yc = y - mu
            var = jnp.mean(yc * yc, axis=-1, keepdims=True)
            yn = yc * lax.rsqrt(var + LN_EPS)
            gate = jax.nn.silu(g_ref[0, pl.ds(r0, c), :].astype(F32))
            o_ref[0, pl.ds(r0, c), :] = (gate * yn).astype(BF16)
            return carry

        lax.fori_loop(0, n_chunks, body, 0)


def _ret(zr, lg, batch, seq, n_rh, tile):
    n_tiles = seq // tile
    per_b = seq // tile

    def rows(t):
        return jnp.where(t < n_tiles, t, 2 * n_tiles - 1 - t)

    def spec(seg):
        return pl.BlockSpec((1, tile, RET_HEAD_DIM), lambda b, h, t: (seg * n_rh + h, b * per_b + rows(t), 0))

    return pl.pallas_call(
        functools.partial(_ret_kernel, n_tiles=n_tiles, tile=tile),
        grid=(batch, n_rh, 2 * n_tiles),
        in_specs=[pl.BlockSpec(memory_space=pltpu.SMEM), spec(0), spec(1), spec(2), spec(3)],
        out_specs=pl.BlockSpec((1, tile, RET_HEAD_DIM),
                               lambda b, h, t: (h, b * per_b + jnp.where(t < n_tiles, n_tiles - 1, 2 * n_tiles - 1 - t), 0)),
        out_shape=jax.ShapeDtypeStruct((n_rh, batch * seq, RET_HEAD_DIM), BF16),
        scratch_shapes=[pltpu.VMEM((seq, RET_HEAD_DIM), F32),
                        pltpu.VMEM((RET_HEAD_DIM, RET_HEAD_DIM), F32),
                        pltpu.VMEM((RET_HEAD_DIM, RET_HEAD_DIM), F32)],
        compiler_params=_cparams(("arbitrary", "arbitrary", "arbitrary")),
        name="ret",
    )(lg, zr, zr, zr, zr)


def _outproj_kernel(a_ref, r_ref, x_ref, mod_ref, g0_ref, b0_ref, g1_ref, b1_ref, w_ref, x1_ref, hp_ref,
                    *, n_ah, n_rh, alpha):
    mix = jnp.concatenate([a_ref[h] for h in range(n_ah)] + [r_ref[h] for h in range(n_rh)], axis=-1)
    acc = jnp.dot(mix, w_ref[...], preferred_element_type=F32)
    xn = _ln(x_ref[...], g0_ref[...], b0_ref[...])
    y = alpha * xn + (1.0 + mod_ref[0, 2:3, :]) * acc
    x1 = _ln(y, g1_ref[...], b1_ref[...])
    x1_ref[...] = x1
    hp_ref[...] = _pack_bf16_pairs(x1 * (1.0 + mod_ref[0, 4:5, :]) + mod_ref[0, 3:4, :])


def _outproj(attn, r, x2, mod3, g0, b0, g1, b1, w_out_b, seq, tm, alpha):
    t, d = x2.shape
    n_ah, n_rh = attn.shape[0], r.shape[0]
    per_b = seq // tm
    row = lambda i: (i, 0)
    vec = pl.BlockSpec((1, d), lambda i: (0, 0))
    return pl.pallas_call(
        functools.partial(_outproj_kernel, n_ah=n_ah, n_rh=n_rh, alpha=alpha),
        grid=(t // tm,),
        in_specs=[pl.BlockSpec((n_ah, tm, ATTN_HEAD_DIM), lambda i: (0, i, 0)),
                  pl.BlockSpec((n_rh, tm, RET_HEAD_DIM), lambda i: (0, i, 0)),
                  pl.BlockSpec((tm, d), row),
                  pl.BlockSpec((1, 6, d), lambda i: (i // per_b, 0, 0)),
                  vec, vec, vec, vec,
                  pl.BlockSpec((d, d), lambda i: (0, 0))],
        out_specs=[pl.BlockSpec((tm, d), row), pl.BlockSpec((tm, d // 2), row)],
        out_shape=[jax.ShapeDtypeStruct((t, d), F32), jax.ShapeDtypeStruct((t, d // 2), U32)],
        compiler_params=_cparams(("arbitrary",)),
        name="outproj",
    )(attn, r, x2, mod3, g0, b0, g1, b1, w_out_b)


def _first_argmax(rows):
    best, idx = rows[0], jnp.zeros(rows[0].shape, I32)
    for e in range(1, len(rows)):
        better = rows[e] > best
        idx = jnp.where(better, e, idx)
        best = jnp.maximum(best, rows[e])
    return best, idx


def _router_kernel(h_ref, w_ref, b_ref, u_ref, eid_ref, wt_ref, rank_ref, cnt_ref, carry_ref):
    i = pl.program_id(0)

    @pl.when(i == 0)
    def _():
        carry_ref[...] = jnp.zeros_like(carry_ref)

    lt = lax.dot_general(w_ref[...], _unpack_bf16_pairs(h_ref[...]), (((1,), (1,)), ((), ())),
                         preferred_element_type=F32) + b_ref[...]
    grow = [lt[g:g + 1, :] for g in range(N_GROUPS)]
    gmax, gsel = _first_argmax(grow)
    gsum = grow[0] * 0.0
    for g in range(N_GROUPS):
        gsum = gsum + jnp.exp(grow[g] - gmax)
    pg = 1.0 / gsum
    srow = []
    for e in range(EXPERTS_PER_GROUP):
        r = lt[N_GROUPS + e:N_GROUPS + e + 1, :]
        for g in range(1, N_GROUPS):
            o = N_GROUPS + g * EXPERTS_PER_GROUP + e
            r = jnp.where(gsel == g, lt[o:o + 1, :], r)
        srow.append(r)
    v1, i1 = _first_argmax(srow)
    v2, i2 = _first_argmax([jnp.where(i1 == e, -jnp.inf, srow[e]) for e in range(EXPERTS_PER_GROUP)])
    e2 = jnp.exp(v2 - v1)
    den = 1.0 + e2
    wt_ref[0:1, :] = (1.0 / den) * pg
    wt_ref[1:2, :] = (e2 / den) * pg
    eid0 = gsel * EXPERTS_PER_GROUP + i1
    eid1 = gsel * EXPERTS_PER_GROUP + i2
    eid_ref[0, 0:1, :] = eid0
    eid_ref[0, 1:2, :] = eid1
    tm = lt.shape[1]
    erow = lax.broadcasted_iota(I32, (N_EXPERTS, tm), 0)
    oh0 = (erow == eid0).astype(F32)
    oh1 = (erow == eid1).astype(F32)
    oh = oh0 + oh1
    incl = jnp.dot(oh.astype(BF16), u_ref[...], preferred_element_type=F32)
    before = carry_ref[:, 0:1] + incl - oh
    rank_ref[0, 0:1, :] = jnp.sum(oh0 * before, axis=0, keepdims=True).astype(I32)
    rank_ref[0, 1:2, :] = jnp.sum(oh1 * before, axis=0, keepdims=True).astype(I32)
    carry = carry_ref[...] + jnp.sum(oh, axis=1, keepdims=True)
    carry_ref[...] = carry
    cnt_ref[...] = carry


def _router(hp, wr, br, tm):
    t, dh = hp.shape
    tri = (lax.broadcasted_iota(I32, (tm, tm), 0) <= lax.broadcasted_iota(I32, (tm, tm), 1)).astype(BF16)
    tile3 = pl.BlockSpec((1, 2, tm), lambda i: (i, 0, 0))
    return pl.pallas_call(
        _router_kernel,
        grid=(t // tm,),
        in_specs=[pl.BlockSpec((tm, dh), lambda i: (i, 0)),
                  pl.BlockSpec((ROUTER_ROWS, 2 * dh), lambda i: (0, 0)),
                  pl.BlockSpec((ROUTER_ROWS, 1), lambda i: (0, 0)),
                  pl.BlockSpec((tm, tm), lambda i: (0, 0))],
        out_specs=[tile3, pl.BlockSpec((2, tm), lambda i: (0, i)), tile3,
                   pl.BlockSpec((N_EXPERTS, LANES), lambda i: (0, 0))],
        out_shape=[jax.ShapeDtypeStruct((t // tm, 2, tm), I32), jax.ShapeDtypeStruct((2, t), F32),
                   jax.ShapeDtypeStruct((t // tm, 2, tm), I32), jax.ShapeDtypeStruct((N_EXPERTS, LANES), F32)],
        scratch_shapes=[pltpu.VMEM((N_EXPERTS, LANES), F32)],
        compiler_params=_cparams(("arbitrary",)),
        name="router",
    )(hp, wr, br, tri)


def _row_copy(src, s, dst, d, sem):
    return pltpu.make_async_copy(src.at[pl.ds(s, 1), :], dst.at[pl.ds(d, 1), :], sem)


def _dispatch_kernel(ps_ref, lo_ref, hi_ref, eid_ref, rank_ref, h_hbm, xb_hbm, sem):
    i = pl.program_id(0)
    n = pl.num_programs(0)
    slot = i % 2
    tm = eid_ref.shape[2]
    base = i * tm

    def body(r, carry):
        for k in range(2):
            dst = ps_ref[eid_ref[0, k, r]] + rank_ref[0, k, r]
            _row_copy(h_hbm, base + r, xb_hbm, dst, sem.at[slot]).start()
        return carry

    lax.fori_loop(0, tm, body, 0, unroll=8)

    def wait_tile(s):
        pltpu.make_async_copy(h_hbm.at[pl.ds(0, 2 * tm), :], xb_hbm.at[pl.ds(0, 2 * tm), :], sem.at[s]).wait()

    @pl.when(i > 0)
    def _():
        wait_tile(1 - slot)

    @pl.when(i == n - 1)
    def _():
        wait_tile(slot)

        def fill(e, carry):
            def one(s, c):
                _row_copy(h_hbm, 0, xb_hbm, s, sem.at[2]).start()
                return c
            lax.fori_loop(lo_ref[e], hi_ref[e], one, 0)

            def one_wait(s, c):
                _row_copy(h_hbm, 0, xb_hbm, s, sem.at[2]).wait()
                return c
            lax.fori_loop(lo_ref[e], hi_ref[e], one_wait, 0)
            return carry

        lax.fori_loop(0, N_EXPERTS + 1, fill, 0)


def _dispatch(pstart, fill_lo, fill_hi, eid3, rank3, hp, n_slots):
    n, _, tm = eid3.shape
    smem3 = pl.BlockSpec((1, 2, tm), lambda i, *_: (i, 0, 0), memory_space=pltpu.SMEM)
    grid_spec = pltpu.PrefetchScalarGridSpec(
        num_scalar_prefetch=3,
        grid=(n,),
        in_specs=[smem3, smem3, pl.BlockSpec(memory_space=pl.ANY)],
        out_specs=pl.BlockSpec(memory_space=pl.ANY),
        scratch_shapes=[pltpu.SemaphoreType.DMA((3,))])
    return pl.pallas_call(
        _dispatch_kernel,
        grid_spec=grid_spec,
        out_shape=jax.ShapeDtypeStruct((n_slots, hp.shape[1]), U32),
        compiler_params=_cparams(("arbitrary",)),
        name="dispatch",
    )(pstart, fill_lo, fill_hi, eid3, rank3, hp)


def _expert_kernel(be_ref, x_ref, w1_ref, w3_ref, w2_ref, y_ref, w1b, w3b, w2b):
    i = pl.program_id(0)
    changed = (i == 0) | (be_ref[i] != be_ref[jnp.maximum(i - 1, 0)])

    @pl.when(changed)
    def _():
        w1b[...] = w1_ref[0].astype(BF16)
        w3b[...] = w3_ref[0].astype(BF16)
        w2b[...] = w2_ref[0].astype(BF16)

    x = _unpack_bf16_pairs(x_ref[...])
    a = jnp.dot(x, w1b[...], preferred_element_type=F32)
    b = jnp.dot(x, w3b[...], preferred_element_type=F32)
    mid = (jax.nn.silu(a) * b).astype(BF16)
    y_ref[...] = jnp.dot(mid, w2b[...], preferred_element_type=F32)


def _experts(blk_e, xb, w1, w3, w2):
    n_slots, dh = xb.shape
    d, ff = w1.shape[1], w1.shape[2]
    nblk = n_slots // MOE_BLOCK
    grid_spec = pltpu.PrefetchScalarGridSpec(
        num_scalar_prefetch=1,
        grid=(nblk,),
        in_specs=[pl.BlockSpec((MOE_BLOCK, dh), lambda i, be: (i, 0)),
                  pl.BlockSpec((1, d, ff), lambda i, be: (be[i], 0, 0)),
                  pl.BlockSpec((1, d, ff), lambda i, be: (be[i], 0, 0)),
                  pl.BlockSpec((1, ff, d), lambda i, be: (be[i], 0, 0))],
        out_specs=pl.BlockSpec((MOE_BLOCK, d), lambda i, be: (i, 0)),
        scratch_shapes=[pltpu.VMEM((d, ff), BF16), pltpu.VMEM((d, ff), BF16), pltpu.VMEM((ff, d), BF16)])
    return pl.pallas_call(
        _expert_kernel,
        grid_spec=grid_spec,
        out_shape=jax.ShapeDtypeStruct((n_slots, d), F32),
        compiler_params=_cparams(("arbitrary",)),
        name="experts",
    )(blk_e, xb, w1, w3, w2)


def _combine_kernel(ps_ref, eid_ref, rank_ref, eidn_ref, rankn_ref, y_hbm, wt_ref, x1_ref, mod_ref, g_ref, b_ref,
                    o_ref, ybuf, sem, *, alpha):
    i = pl.program_id(0)
    n = pl.num_programs(0)
    slot = i % 2
    tm = x1_ref.shape[0]

    def start(e_ref, r_ref, s):
        def body(r, carry):
            for k in range(2):
                src = ps_ref[e_ref[0, k, r]] + r_ref[0, k, r]
                _row_copy(y_hbm, src, ybuf.at[s, k], r, sem.at[s]).start()
            return carry
        lax.fori_loop(0, tm, body, 0, unroll=8)

    @pl.when(i == 0)
    def _():
        start(eid_ref, rank_ref, 0)

    @pl.when(i + 1 < n)
    def _():
        start(eidn_ref, rankn_ref, 1 - slot)

    for k in range(2):
        pltpu.make_async_copy(y_hbm.at[pl.ds(0, tm), :], ybuf.at[slot, k], sem.at[slot]).wait()
    wt = wt_ref[...]
    ffn = wt[:, 0:1] * ybuf[slot, 0] + wt[:, 1:2] * ybuf[slot, 1]
    y = alpha * x1_ref[...] + (1.0 + mod_ref[0, 5:6, :]) * ffn
    o_ref[...] = _ln(y, g_ref[...], b_ref[...])


def _combine(pstart, eid3, rank3, yb, wt_t, x1, mod3, g2, b2, seq, alpha):
    t, d = x1.shape
    n, _, tm = eid3.shape
    per_b = seq // tm
    cur = pl.BlockSpec((1, 2, tm), lambda i, ps: (i, 0, 0), memory_space=pltpu.SMEM)
    nxt = pl.BlockSpec((1, 2, tm), lambda i, ps: (jnp.minimum(i + 1, n - 1), 0, 0), memory_space=pltpu.SMEM)
    vec = pl.BlockSpec((1, d), lambda i, ps: (0, 0))
    grid_spec = pltpu.PrefetchScalarGridSpec(
        num_scalar_prefetch=1,
        grid=(n,),
        in_specs=[cur, cur, nxt, nxt,
                  pl.BlockSpec(memory_space=pl.ANY),
                  pl.BlockSpec((tm, 2), lambda i, ps: (i, 0)),
                  pl.BlockSpec((tm, d), lambda i, ps: (i, 0)),
                  pl.BlockSpec((1, 6, d), lambda i, ps: (i // per_b, 0, 0)),
                  vec, vec],
        out_specs=pl.BlockSpec((tm, d), lambda i, ps: (i, 0)),
        scratch_shapes=[pltpu.VMEM((2, 2, tm, d), F32), pltpu.SemaphoreType.DMA((2,))])
    return pl.pallas_call(
        functools.partial(_combine_kernel, alpha=alpha),
        grid_spec=grid_spec,
        out_shape=jax.ShapeDtypeStruct((t, d), F32),
        compiler_params=_cparams(("arbitrary",)),
        name="combine",
    )(pstart, eid3, rank3, eid3, rank3, yb, wt_t, x1, mod3, g2, b2)


def _pick_tile(n, want):
    tm = min(n, want)
    assert n % tm == 0
    return tm


def kernel(x, c, positions, ln0_g, ln0_b, w_ada, b_ada, w_in, w_out, ret_log_decay_f, ret_log_decay_b,
           ln1_g, ln1_b, w_group, b_group, w_sub, b_sub, w1, w3, w2, ln2_g, ln2_b):
    batch, seq, d = x.shape
    depth = w_ada.shape[0]
    t = batch * seq
    n_ah = d // 2 // ATTN_HEAD_DIM
    n_rh = d // 2 // RET_HEAD_DIM
    assert depth == 1 and d % (2 * RET_HEAD_DIM) == 0 and seq % (max(DILATIONS) * KW) == 0 and batch <= 8
    assert seq % (QT * CHAINS) == 0
    alpha = (2 * depth) ** 0.25

    inv_rope = ROPE_THETA ** (-jnp.arange(0, ROPE_DIM, 2, dtype=F32) / ROPE_DIM)
    inv_ret = RET_THETA ** (-jnp.linspace(0.0, 1.0, RET_HEAD_DIM // 2, dtype=F32))
    fa = jnp.zeros((1, LANES), F32).at[0, :ROPE_DIM].set(jnp.concatenate([inv_rope, inv_rope]))
    fr = inv_ret.reshape(1, LANES)
    posb = jnp.broadcast_to(positions.astype(F32).reshape(t, 1), (t, LANES))
    c8 = jnp.zeros((8, d), F32).at[:batch].set(c)
    row = lambda v: v.reshape(1, d)

    xs = x.reshape(t, d)
    mod = _ada(c8, w_ada[0], b_ada[0].reshape(1, -1))
    mod3 = mod[:batch].reshape(batch, 6, d)
    za, zr = _inproj(xs, posb, mod3, row(ln0_g), row(ln0_b), w_in[0].astype(BF16), fa, fr, seq, _pick_tile(seq, 512))
    attn = _attn(za, batch, seq, n_ah)
    lg = jnp.stack([ret_log_decay_f[0], ret_log_decay_b[0]]).astype(F32)
    r = _ret(zr, lg, batch, seq, n_rh, _pick_tile(seq, 1024))
    x1, hp = _outproj(attn, r, xs, mod3, row(ln0_g), row(ln0_b), row(ln1_g[0]), row(ln1_b[0]),
                      w_out[0].astype(BF16), seq, _pick_tile(seq, 256), alpha)
    wr = jnp.zeros((ROUTER_ROWS, d), F32)
    wr = wr.at[:N_GROUPS].set(w_group[0].T)
    wr = wr.at[N_GROUPS:N_GROUPS + N_EXPERTS].set(w_sub[0].transpose(0, 2, 1).reshape(N_EXPERTS, d))
    br = jnp.zeros((ROUTER_ROWS, 1), F32)
    br = br.at[:N_GROUPS, 0].set(b_group[0]).at[N_GROUPS:N_GROUPS + N_EXPERTS, 0].set(b_sub[0].reshape(-1))
    eid3, wt, rank3, cnt = _router(hp, wr.astype(BF16), br, _pick_tile(seq, 512))
    counts = cnt[:, 0].astype(I32)
    padded = (counts + MOE_BLOCK - 1) // MOE_BLOCK * MOE_BLOCK
    pend = jnp.cumsum(padded)
    pstart = pend - padded
    n_slots = 2 * t + N_EXPERTS * MOE_BLOCK
    nblk = n_slots // MOE_BLOCK
    starts = jnp.arange(nblk, dtype=I32) * MOE_BLOCK
    blk_e = jnp.minimum(jnp.sum((pend[None, :] <= starts[:, None]).astype(I32), axis=1), N_EXPERTS - 1)
    fill_lo = jnp.concatenate([pstart + counts, pend[-1:]])
    fill_hi = jnp.concatenate([pend, jnp.full((1,), n_slots, I32)])
    xb = _dispatch(pstart, fill_lo, fill_hi, eid3, rank3, hp, n_slots)
    yb = _experts(blk_e, xb, w1[0], w3[0], w2[0])
    out = _combine(pstart, eid3, rank3, yb, wt.T, x1, mod3, row(ln2_g[0]), row(ln2_b[0]), seq, alpha)
    return out.reshape(batch, seq, d)
```

```python
import functools

import jax
import jax.numpy as jnp
from jax import lax
from jax.experimental import pallas as pl
from jax.experimental.pallas import tpu as pltpu

F32 = jnp.float32
BF16 = jnp.bfloat16
I32 = jnp.int32
U32 = jnp.uint32

LANES = 128
ATTN_HEAD_DIM = 128
RET_HEAD_DIM = 256
DILATIONS = (1, 4, 16)
BAND = 64
QT = 128
KW = QT + 2 * BAND
CHAINS = 8
PERM_ROWS = 256
PERM_UNROLL = 2
ROPE_THETA = 500000.0
ROPE_DIM = ATTN_HEAD_DIM // 4
RET_THETA = 10000.0
RET_CHUNK = 128
N_GROUPS = 4
EXPERTS_PER_GROUP = 8
N_EXPERTS = N_GROUPS * EXPERTS_PER_GROUP
MOE_BLOCK = 128
LN_EPS = 1e-5
MASK_VALUE = -1e30
N_SEG = 7
ROUTER_ROWS = 48
HI_MASK = 0xFFFF0000
VMEM_LIMIT = 56 * 1024 * 1024


def _cparams(sem, vmem=VMEM_LIMIT):
    return pltpu.CompilerParams(dimension_semantics=sem, vmem_limit_bytes=vmem)


def _ln(x, g, b):
    mu = jnp.mean(x, axis=-1, keepdims=True)
    xc = x - mu
    var = jnp.mean(xc * xc, axis=-1, keepdims=True)
    return xc * lax.rsqrt(var + LN_EPS) * g + b


def _pack_bf16_pairs(h):
    bits = lax.bitcast_convert_type(h.astype(BF16).astype(F32), U32)
    n = h.shape[1] // 2
    return (bits[:, :n] >> 16) | (bits[:, n:] & jnp.uint32(HI_MASK))


def _store_token_tiles(ref, packed):
    m, n = packed.shape
    per = n // LANES
    for s in range(per):
        ref[pl.ds(s, m, stride=per), :] = packed[:, s * LANES:(s + 1) * LANES]


def _load_token_tiles(ref, m, per):
    slabs = [ref[pl.ds(s, m, stride=per), :] for s in range(per)]
    lo = [lax.bitcast_convert_type(p << 16, F32) for p in slabs]
    hi = [lax.bitcast_convert_type(p & jnp.uint32(HI_MASK), F32) for p in slabs]
    return jnp.concatenate(lo + hi, axis=-1).astype(BF16)


def _ada_kernel(c_ref, w_ref, b_ref, o_ref):
    cs = jax.nn.silu(c_ref[...])
    o_ref[...] = jnp.dot(cs.astype(BF16), w_ref[...].astype(BF16), preferred_element_type=F32) + b_ref[...]


def _ada(c8, w_ada, b_ada):
    d, n = w_ada.shape
    tn = min(n, 512)
    return pl.pallas_call(
        _ada_kernel,
        grid=(n // tn,),
        in_specs=[pl.BlockSpec((8, d), lambda j: (0, 0)),
                  pl.BlockSpec((d, tn), lambda j: (0, j)),
                  pl.BlockSpec((1, tn), lambda j: (0, j))],
        out_specs=pl.BlockSpec((8, tn), lambda j: (0, j)),
        out_shape=jax.ShapeDtypeStruct((8, n), F32),
        compiler_params=_cparams(("arbitrary",)),
        name="ada",
    )(c8, w_ada, b_ada)


def _inproj_kernel(x_ref, pos_ref, mod_ref, g_ref, b_ref, w_ref, fa_ref, fr_ref, za_ref, zr_ref,
                   h_ref, ca_ref, sa1_ref, sa2_ref, cr_ref, sr_ref, *, n_ah, n_rh):
    j = pl.program_id(1)

    @pl.when(j == 0)
    def _():
        xn = _ln(x_ref[...], g_ref[...], b_ref[...])
        h = xn * (1.0 + mod_ref[0, 1:2, :]) + mod_ref[0, 0:1, :]
        h_ref[...] = h.astype(BF16)
        pos = pos_ref[...]
        ang = pos * fa_ref[...]
        c, s = jnp.cos(ang), jnp.sin(ang)
        lane = lax.broadcasted_iota(I32, ang.shape, 1)
        half = ROPE_DIM // 2
        ca_ref[...] = c
        sa1_ref[...] = jnp.where(lane < half, -s, 0.0)
        sa2_ref[...] = jnp.where((lane >= half) & (lane < 2 * half), s, 0.0)
        angr = pos * fr_ref[...]
        cr_ref[...] = jnp.cos(angr)
        sr_ref[...] = jnp.sin(angr)

    res = jnp.dot(h_ref[...], w_ref[...], preferred_element_type=F32)

    def attn_rot(scale):
        ca, sa1, sa2 = ca_ref[...], sa1_ref[...], sa2_ref[...]
        half = ROPE_DIM // 2
        for hc in range(n_ah):
            xh = res[:, hc * LANES:(hc + 1) * LANES]
            r = xh * ca + pltpu.roll(xh, LANES - half, 1) * sa1 + pltpu.roll(xh, half, 1) * sa2
            if scale is not None:
                r = r * scale
            za_ref[hc] = r.astype(BF16)

    def ret_rot(scale):
        cr, sr = cr_ref[...], sr_ref[...]
        for hc in range(n_rh):
            x1 = res[:, hc * RET_HEAD_DIM:hc * RET_HEAD_DIM + LANES]
            x2 = res[:, hc * RET_HEAD_DIM + LANES:(hc + 1) * RET_HEAD_DIM]
            o1 = x1 * cr - x2 * sr
            o2 = x2 * cr + x1 * sr
            if scale is not None:
                o1, o2 = o1 * scale, o2 * scale
            zr_ref[hc, :, 0:LANES] = o1.astype(BF16)
            zr_ref[hc, :, LANES:RET_HEAD_DIM] = o2.astype(BF16)

    @pl.when(j == 0)
    def _():
        attn_rot(ATTN_HEAD_DIM ** -0.5)

    @pl.when(j == 1)
    def _():
        attn_rot(None)

    @pl.when(j == 2)
    def _():
        for hc in range(n_ah):
            za_ref[hc] = res[:, hc * LANES:(hc + 1) * LANES].astype(BF16)

    @pl.when(j == 3)
    def _():
        ret_rot(None)

    @pl.when(j == 4)
    def _():
        ret_rot(RET_HEAD_DIM ** -0.5)

    @pl.when(j >= 5)
    def _():
        for hc in range(n_rh):
            zr_ref[hc] = res[:, hc * RET_HEAD_DIM:(hc + 1) * RET_HEAD_DIM].astype(BF16)


def _inproj(x2, posb, mod3, ln0_g, ln0_b, w_in_b, fa, fr, seq, tm):
    t, d = x2.shape
    tn = d // 2
    n_ah, n_rh = tn // ATTN_HEAD_DIM, tn // RET_HEAD_DIM
    per_b = seq // tm
    return pl.pallas_call(
        functools.partial(_inproj_kernel, n_ah=n_ah, n_rh=n_rh),
        grid=(t // tm, N_SEG),
        in_specs=[pl.BlockSpec((tm, d), lambda i, j: (i, 0)),
                  pl.BlockSpec((tm, LANES), lambda i, j: (i, 0)),
                  pl.BlockSpec((1, 6, d), lambda i, j: (i // per_b, 0, 0)),
                  pl.BlockSpec((1, d), lambda i, j: (0, 0)),
                  pl.BlockSpec((1, d), lambda i, j: (0, 0)),
                  pl.BlockSpec((d, tn), lambda i, j: (0, j)),
                  pl.BlockSpec((1, LANES), lambda i, j: (0, 0)),
                  pl.BlockSpec((1, LANES), lambda i, j: (0, 0))],
        out_specs=[pl.BlockSpec((n_ah, tm, ATTN_HEAD_DIM), lambda i, j: (jnp.minimum(j, 2), i, 0)),
                   pl.BlockSpec((n_rh, tm, RET_HEAD_DIM), lambda i, j: (jnp.maximum(j - 3, 0), i, 0))],
        out_shape=[jax.ShapeDtypeStruct((3 * n_ah, t, ATTN_HEAD_DIM), BF16),
                   jax.ShapeDtypeStruct((4 * n_rh, t, RET_HEAD_DIM), BF16)],
        scratch_shapes=[pltpu.VMEM((tm, d), BF16)] + [pltpu.VMEM((tm, LANES), F32)] * 5,
        compiler_params=_cparams(("arbitrary", "arbitrary")),
        name="inproj",
    )(x2, posb, mod3, ln0_g, ln0_b, w_in_b, fa, fr)


def _attn_tile(q, k, v, bias):
    s = lax.dot_general(q, k, (((1,), (1,)), ((), ())), preferred_element_type=F32) + bias
    m = jnp.max(jnp.maximum(s[:, :LANES], s[:, LANES:]), axis=-1, keepdims=True)
    p = jnp.exp(s - m)
    den = jnp.sum(p[:, :LANES] + p[:, LANES:], axis=-1, keepdims=True)
    acc = jnp.dot(p.astype(BF16), v, preferred_element_type=F32)
    return acc, jnp.broadcast_to(m, acc.shape), jnp.broadcast_to(den, acc.shape)


def _attn_kernel(q_ref, k_ref, v_ref, bias_ref, p4_ref, p16_ref, o_ref,
                 acc_ref, m_ref, d_ref, qp_ref, kp_ref, vp_ref, *, seq):
    n_items = seq // QT

    def window(n, length):
        qs = n * QT
        ks = jnp.clip(qs - BAND, 0, length - KW)
        return qs, ks, bias_ref[lax.div(qs - ks, BAND)]

    def perm_body(it, carry):
        for u in range(PERM_UNROLL):
            blk = it * PERM_UNROLL + u
            r0 = pl.multiple_of(blk * PERM_ROWS, PERM_ROWS)
            qk = jnp.concatenate([q_ref[0, 0, pl.ds(r0, PERM_ROWS), :], k_ref[0, 0, pl.ds(r0, PERM_ROWS), :]], axis=-1)
            v = v_ref[0, 0, pl.ds(r0, PERM_ROWS), :]
            for slot, (dil, p_ref) in enumerate(((4, p4_ref), (16, p16_ref))):
                length, w = seq // dil, PERM_ROWS // dil
                qkp = jnp.dot(p_ref[...], qk, preferred_element_type=F32).astype(BF16)
                vp = jnp.dot(p_ref[...], v, preferred_element_type=F32).astype(BF16)
                for r in range(dil):
                    rows = pl.ds(pl.multiple_of(r * length + blk * w, w), w)
                    qp_ref[slot, rows, :] = qkp[r * w:(r + 1) * w, :ATTN_HEAD_DIM]
                    kp_ref[slot, rows, :] = qkp[r * w:(r + 1) * w, ATTN_HEAD_DIM:]
                    vp_ref[slot, rows, :] = vp[r * w:(r + 1) * w, :]
        return carry

    lax.fori_loop(0, seq // PERM_ROWS // PERM_UNROLL, perm_body, 0)

    def dilated_tile(item, dil, slot):
        length = seq // dil
        n, r = lax.div(item, dil), lax.rem(item, dil)
        qs, ks, bias = window(n, length)
        qrow = pl.multiple_of(r * length + qs, QT)
        krow = pl.multiple_of(r * length + ks, BAND)
        out = _attn_tile(qp_ref[slot, pl.ds(qrow, QT), :], kp_ref[slot, pl.ds(krow, KW), :],
                         vp_ref[slot, pl.ds(krow, KW), :], bias)
        return out, pl.ds(qs * dil + r, QT, stride=dil)

    def body16(it, carry):
        for g in range(CHAINS):
            (acc, m, den), rows = dilated_tile(it * CHAINS + g, 16, 1)
            acc_ref[rows, :] = acc
            m_ref[rows, :] = m
            d_ref[rows, :] = den
        return carry

    lax.fori_loop(0, n_items // CHAINS, body16, 0)

    def body4(it, carry):
        for g in range(CHAINS):
            (acc, m, den), rows = dilated_tile(it * CHAINS + g, 4, 0)
            m0 = m_ref[rows, :]
            mn = jnp.maximum(m0, m)
            a, b = jnp.exp(m0 - mn), jnp.exp(m - mn)
            acc_ref[rows, :] = a * acc_ref[rows, :] + b * acc
            d_ref[rows, :] = a * d_ref[rows, :] + b * den
            m_ref[rows, :] = mn
        return carry

    lax.fori_loop(0, n_items // CHAINS, body4, 0)

    def body1(it, carry):
        for g in range(CHAINS):
            qs, ks, bias = window(it * CHAINS + g, seq)
            qs, ks = pl.multiple_of(qs, QT), pl.multiple_of(ks, BAND)
            acc, m, den = _attn_tile(q_ref[0, 0, pl.ds(qs, QT), :], k_ref[0, 0, pl.ds(ks, KW), :],
                                     v_ref[0, 0, pl.ds(ks, KW), :], bias)
            rows = pl.ds(qs, QT)
            m0 = m_ref[rows, :]
            mn = jnp.maximum(m0, m)
            a, b = jnp.exp(m0 - mn), jnp.exp(m - mn)
            num = a * acc_ref[rows, :] + b * acc
            o_ref[0, 0, rows, :] = (num / (a * d_ref[rows, :] + b * den)).astype(BF16)
        return carry

    lax.fori_loop(0, n_items // CHAINS, body1, 0)


def _perm_matrix(dil):
    w = PERM_ROWS // dil
    i = jnp.arange(PERM_ROWS)
    src = (i % w) * dil + i // w
    return (src[:, None] == jnp.arange(PERM_ROWS)[None, :]).astype(BF16)


def _attn(za, batch, seq, n_ah):
    zv = za.reshape(3 * n_ah, batch, seq, ATTN_HEAD_DIM)
    blk = (1, 1, seq, ATTN_HEAD_DIM)
    i = jnp.arange(QT)[:, None]
    jj = jnp.arange(KW)[None, :]
    bias = jnp.stack([jnp.where(jnp.abs(i + o * BAND - jj) <= BAND, 0.0, MASK_VALUE) for o in range(3)]).astype(F32)
    const = lambda shape: pl.BlockSpec(shape, lambda b, h: (0,) * len(shape))
    o = pl.pallas_call(
        functools.partial(_attn_kernel, seq=seq),
        grid=(batch, n_ah),
        in_specs=[pl.BlockSpec(blk, lambda b, h: (h, b, 0, 0)),
                  pl.BlockSpec(blk, lambda b, h: (n_ah + h, b, 0, 0)),
                  pl.BlockSpec(blk, lambda b, h: (2 * n_ah + h, b, 0, 0)),
                  const((3, QT, KW)), const((PERM_ROWS, PERM_ROWS)), const((PERM_ROWS, PERM_ROWS))],
        out_specs=pl.BlockSpec(blk, lambda b, h: (h, b, 0, 0)),
        out_shape=jax.ShapeDtypeStruct((n_ah, batch, seq, ATTN_HEAD_DIM), BF16),
        scratch_shapes=[pltpu.VMEM((seq, LANES), F32)] * 3
        + [pltpu.VMEM((2, seq, ATTN_HEAD_DIM), BF16)] * 3,
        compiler_params=_cparams(("arbitrary", "arbitrary")),
        name="attn",
    )(zv, zv, zv, bias, _perm_matrix(4), _perm_matrix(16))
    return o.reshape(n_ah, batch * seq, ATTN_HEAD_DIM)


def _ret_kernel(lg_ref, q_ref, k_ref, v_ref, g_ref, o_ref, y_ref, sf_ref, sb_ref, *, n_tiles, tile):
    h = pl.program_id(1)
    t = pl.program_id(2)
    c = RET_CHUNK
    lgf = lg_ref[0, h]
    lgb = lg_ref[1, h]
    ii = lax.broadcasted_iota(I32, (c, c), 0)
    jj = lax.broadcasted_iota(I32, (c, c), 1)
    diff = (ii - jj).astype(F32)
    dmat = jnp.where(diff >= 0, jnp.exp(lgf * jnp.maximum(diff, 0.0)), 0.0) \
        + jnp.where(diff < 0, jnp.exp(lgb * jnp.maximum(-diff, 0.0)), 0.0)
    col = lax.broadcasted_iota(I32, (c, 1), 0).astype(F32)
    row = lax.broadcasted_iota(I32, (1, c), 1).astype(F32)
    n_chunks = tile // c

    @pl.when(t == 0)
    def _():
        sf_ref[...] = jnp.zeros_like(sf_ref)
        sb_ref[...] = jnp.zeros_like(sb_ref)

    @pl.when(t < n_tiles)
    def _():
        xi = jnp.exp(lgf * (col + 1.0))
        zeta = jnp.exp(lgf * (c - 1.0 - row))
        cdec = jnp.exp(lgf * jnp.full((1, 1), float(c), F32))
        base = t * tile

        def body(n, carry):
            r0 = pl.multiple_of(n * c, c)
            q = q_ref[0, pl.ds(r0, c), :]
            k = k_ref[0, pl.ds(r0, c), :]
            v = v_ref[0, pl.ds(r0, c), :]
            s = lax.dot_general(q, k, (((1,), (1,)), ((), ())), preferred_element_type=F32) * dmat
            y = jnp.dot(s.astype(BF16), v, preferred_element_type=F32)
            y = y + jnp.dot(q, sf_ref[...].astype(BF16), preferred_element_type=F32) * xi
            kt = (k.astype(F32).T * zeta).astype(BF16)
            sf_ref[...] = sf_ref[...] * cdec + jnp.dot(kt, v, preferred_element_type=F32)
            y_ref[pl.ds(pl.multiple_of(base + r0, c), c), :] = y
            return carry

        lax.fori_loop(0, n_chunks, body, 0)

    @pl.when(t >= n_tiles)
    def _():
        xi = jnp.exp(lgb * (c - col))
        zeta = jnp.exp(lgb * row)
        cdec = jnp.exp(lgb * jnp.full((1, 1), float(c), F32))
        base = (2 * n_tiles - 1 - t) * tile

        def body(m, carry):
            n = n_chunks - 1 - m
            r0 = pl.multiple_of(n * c, c)
            q = q_ref[0, pl.ds(r0, c), :]
            k = k_ref[0, pl.ds(r0, c), :]
            v = v_ref[0, pl.ds(r0, c), :]
            y = y_ref[pl.ds(pl.multiple_of(base + r0, c), c), :]
            y = y + jnp.dot(q, sb_ref[...].astype(BF16), preferred_element_type=F32) * xi
            kt = (k.astype(F32).T * zeta).astype(BF16)
            sb_ref[...] = sb_ref[...] * cdec + jnp.dot(kt, v, preferred_element_type=F32)
            mu = jnp.mean(y, axis=-1, keepdims=True)
            yc = y - mu
            var = jnp.mean(yc * yc, axis=-1, keepdims=True)
            yn = yc * lax.rsqrt(var + LN_EPS)
            gate = jax.nn.silu(g_ref[0, pl.ds(r0, c), :].astype(F32))
            o_ref[0, pl.ds(r0, c), :] = (gate * yn).astype(BF16)
            return carry

        lax.fori_loop(0, n_chunks, body, 0)


def _ret(zr, lg, batch, seq, n_rh, tile):
    n_tiles = seq // tile
    per_b = seq // tile

    def rows(t):
        return jnp.where(t < n_tiles, t, 2 * n_tiles - 1 - t)

    def spec(seg):
        return pl.BlockSpec((1, tile, RET_HEAD_DIM), lambda b, h, t: (seg * n_rh + h, b * per_b + rows(t), 0))

    return pl.pallas_call(
        functools.partial(_ret_kernel, n_tiles=n_tiles, tile=tile),
        grid=(batch, n_rh, 2 * n_tiles),
        in_specs=[pl.BlockSpec(memory_space=pltpu.SMEM), spec(0), spec(1), spec(2), spec(3)],
        out_specs=pl.BlockSpec((1, tile, RET_HEAD_DIM),
                               lambda b, h, t: (h, b * per_b + jnp.where(t < n_tiles, n_tiles - 1, 2 * n_tiles - 1 - t), 0)),
        out_shape=jax.ShapeDtypeStruct((n_rh, batch * seq, RET_HEAD_DIM), BF16),
        scratch_shapes=[pltpu.VMEM((seq, RET_HEAD_DIM), F32),
                        pltpu.VMEM((RET_HEAD_DIM, RET_HEAD_DIM), F32),
                        pltpu.VMEM((RET_HEAD_DIM, RET_HEAD_DIM), F32)],
        compiler_params=_cparams(("arbitrary", "arbitrary", "arbitrary")),
        name="ret",
    )(lg, zr, zr, zr, zr)


def _outproj_kernel(a_ref, r_ref, x_ref, mod_ref, g0_ref, b0_ref, g1_ref, b1_ref, w_ref, x1_ref, hp_ref,
                    *, n_ah, n_rh, alpha):
    mix = jnp.concatenate([a_ref[h] for h in range(n_ah)] + [r_ref[h] for h in range(n_rh)], axis=-1)
    acc = jnp.dot(mix, w_ref[...], preferred_element_type=F32)
    xn = _ln(x_ref[...], g0_ref[...], b0_ref[...])
    y = alpha * xn + (1.0 + mod_ref[0, 2:3, :]) * acc
    x1 = _ln(y, g1_ref[...], b1_ref[...])
    x1_ref[...] = x1
    _store_token_tiles(hp_ref, _pack_bf16_pairs(x1 * (1.0 + mod_ref[0, 4:5, :]) + mod_ref[0, 3:4, :]))


def _outproj(attn, r, x2, mod3, g0, b0, g1, b1, w_out_b, seq, tm, alpha):
    t, d = x2.shape
    n_ah, n_rh = attn.shape[0], r.shape[0]
    per_b = seq // tm
    per = d // 2 // LANES
    row = lambda i: (i, 0)
    vec = pl.BlockSpec((1, d), lambda i: (0, 0))
    return pl.pallas_call(
        functools.partial(_outproj_kernel, n_ah=n_ah, n_rh=n_rh, alpha=alpha),
        grid=(t // tm,),
        in_specs=[pl.BlockSpec((n_ah, tm, ATTN_HEAD_DIM), lambda i: (0, i, 0)),
                  pl.BlockSpec((n_rh, tm, RET_HEAD_DIM), lambda i: (0, i, 0)),
                  pl.BlockSpec((tm, d), row),
                  pl.BlockSpec((1, 6, d), lambda i: (i // per_b, 0, 0)),
                  vec, vec, vec, vec,
                  pl.BlockSpec((d, d), lambda i: (0, 0))],
        out_specs=[pl.BlockSpec((tm, d), row), pl.BlockSpec((tm * per, LANES), row)],
        out_shape=[jax.ShapeDtypeStruct((t, d), F32), jax.ShapeDtypeStruct((t * per, LANES), U32)],
        compiler_params=_cparams(("arbitrary",)),
        name="outproj",
    )(attn, r, x2, mod3, g0, b0, g1, b1, w_out_b)


def _first_argmax(rows):
    best, idx = rows[0], jnp.zeros(rows[0].shape, I32)
    for e in range(1, len(rows)):
        better = rows[e] > best
        idx = jnp.where(better, e, idx)
        best = jnp.maximum(best, rows[e])
    return best, idx


def _router_kernel(h_ref, w_ref, b_ref, u_ref, eid_ref, wt_ref, rank_ref, cnt_ref, carry_ref, *, per):
    i = pl.program_id(0)

    @pl.when(i == 0)
    def _():
        carry_ref[...] = jnp.zeros_like(carry_ref)

    tm = eid_ref.shape[2]
    lt = lax.dot_general(w_ref[...], _load_token_tiles(h_ref, tm, per), (((1,), (1,)), ((), ())),
                         preferred_element_type=F32) + b_ref[...]
    grow = [lt[g:g + 1, :] for g in range(N_GROUPS)]
    gmax, gsel = _first_argmax(grow)
    gsum = grow[0] * 0.0
    for g in range(N_GROUPS):
        gsum = gsum + jnp.exp(grow[g] - gmax)
    pg = 1.0 / gsum
    srow = []
    for e in range(EXPERTS_PER_GROUP):
        r = lt[N_GROUPS + e:N_GROUPS + e + 1, :]
        for g in range(1, N_GROUPS):
            o = N_GROUPS + g * EXPERTS_PER_GROUP + e
            r = jnp.where(gsel == g, lt[o:o + 1, :], r)
        srow.append(r)
    v1, i1 = _first_argmax(srow)
    v2, i2 = _first_argmax([jnp.where(i1 == e, -jnp.inf, srow[e]) for e in range(EXPERTS_PER_GROUP)])
    e2 = jnp.exp(v2 - v1)
    den = 1.0 + e2
    wt_ref[0:1, :] = (1.0 / den) * pg
    wt_ref[1:2, :] = (e2 / den) * pg
    eid0 = gsel * EXPERTS_PER_GROUP + i1
    eid1 = gsel * EXPERTS_PER_GROUP + i2
    eid_ref[0, 0:1, :] = eid0
    eid_ref[0, 1:2, :] = eid1
    erow = lax.broadcasted_iota(I32, (N_EXPERTS, tm), 0)
    oh0 = (erow == eid0).astype(F32)
    oh1 = (erow == eid1).astype(F32)
    oh = oh0 + oh1
    incl = jnp.dot(oh.astype(BF16), u_ref[...], preferred_element_type=F32)
    before = carry_ref[:, 0:1] + incl - oh
    rank_ref[0, 0:1, :] = jnp.sum(oh0 * before, axis=0, keepdims=True).astype(I32)
    rank_ref[0, 1:2, :] = jnp.sum(oh1 * before, axis=0, keepdims=True).astype(I32)
    carry = carry_ref[...] + jnp.sum(oh, axis=1, keepdims=True)
    carry_ref[...] = carry
    cnt_ref[...] = carry


def _router(hp, wr, br, tm):
    d = wr.shape[1]
    per = d // 2 // LANES
    t = hp.shape[0] // per
    tri = (lax.broadcasted_iota(I32, (tm, tm), 0) <= lax.broadcasted_iota(I32, (tm, tm), 1)).astype(BF16)
    tile3 = pl.BlockSpec((1, 2, tm), lambda i: (i, 0, 0))
    return pl.pallas_call(
        functools.partial(_router_kernel, per=per),
        grid=(t // tm,),
        in_specs=[pl.BlockSpec((tm * per, LANES), lambda i: (i, 0)),
                  pl.BlockSpec((ROUTER_ROWS, d), lambda i: (0, 0)),
                  pl.BlockSpec((ROUTER_ROWS, 1), lambda i: (0, 0)),
                  pl.BlockSpec((tm, tm), lambda i: (0, 0))],
        out_specs=[tile3, pl.BlockSpec((2, tm), lambda i: (0, i)), tile3,
                   pl.BlockSpec((N_EXPERTS, LANES), lambda i: (0, 0))],
        out_shape=[jax.ShapeDtypeStruct((t // tm, 2, tm), I32), jax.ShapeDtypeStruct((2, t), F32),
                   jax.ShapeDtypeStruct((t // tm, 2, tm), I32), jax.ShapeDtypeStruct((N_EXPERTS, LANES), F32)],
        scratch_shapes=[pltpu.VMEM((N_EXPERTS, LANES), F32)],
        compiler_params=_cparams(("arbitrary",)),
        name="router",
    )(hp, wr, br, tri)


def _row_copy(src, s, dst, d, sem):
    return pltpu.make_async_copy(src.at[pl.ds(s, 1), :], dst.at[pl.ds(d, 1), :], sem)


def _expert_kernel(be_ref, tok_ref, nxt_ref, h_hbm, w1_ref, w3_ref, w2_ref, y_ref, xbuf, w1b, w3b, w2b, sem, *, per):
    i = pl.program_id(0)
    n = pl.num_programs(0)
    slot = i % 2

    def gather(ids_ref, s):
        def body(r, carry):
            tok = ids_ref[0, 0, r]
            pltpu.make_async_copy(h_hbm.at[pl.ds(pl.multiple_of(tok * per, per), per), :],
                                  xbuf.at[s, pl.ds(pl.multiple_of(r * per, per), per), :], sem.at[s]).start()
            return carry
        lax.fori_loop(0, MOE_BLOCK, body, 0, unroll=8)

    @pl.when(i == 0)
    def _():
        gather(tok_ref, 0)

    @pl.when(i + 1 < n)
    def _():
        gather(nxt_ref, 1 - slot)

    changed = (i == 0) | (be_ref[i] != be_ref[jnp.maximum(i - 1, 0)])

    @pl.when(changed)
    def _():
        w1b[...] = w1_ref[0].astype(BF16)
        w3b[...] = w3_ref[0].astype(BF16)
        w2b[...] = w2_ref[0].astype(BF16)

    pltpu.make_async_copy(h_hbm.at[pl.ds(0, MOE_BLOCK * per), :], xbuf.at[slot], sem.at[slot]).wait()
    x = _load_token_tiles(xbuf.at[slot], MOE_BLOCK, per)
    a = jnp.dot(x, w1b[...], preferred_element_type=F32)
    b = jnp.dot(x, w3b[...], preferred_element_type=F32)
    mid = (jax.nn.silu(a) * b).astype(BF16)
    y_ref[...] = jnp.dot(mid, w2b[...], preferred_element_type=F32)


def _experts(blk_e, buf_tok, hp, w1, w3, w2):
    d, ff = w1.shape[1], w1.shape[2]
    per = d // 2 // LANES
    nblk = buf_tok.shape[0]
    smem_blk = lambda f: pl.BlockSpec((1, 1, MOE_BLOCK), f, memory_space=pltpu.SMEM)
    grid_spec = pltpu.PrefetchScalarGridSpec(
        num_scalar_prefetch=1,
        grid=(nblk,),
        in_specs=[smem_blk(lambda i, be: (i, 0, 0)),
                  smem_blk(lambda i, be: (jnp.minimum(i + 1, nblk - 1), 0, 0)),
                  pl.BlockSpec(memory_space=pl.ANY),
                  pl.BlockSpec((1, d, ff), lambda i, be: (be[i], 0, 0)),
                  pl.BlockSpec((1, d, ff), lambda i, be: (be[i], 0, 0)),
                  pl.BlockSpec((1, ff, d), lambda i, be: (be[i], 0, 0))],
        out_specs=pl.BlockSpec((MOE_BLOCK, d), lambda i, be: (i, 0)),
        scratch_shapes=[pltpu.VMEM((2, MOE_BLOCK * per, LANES), U32),
                        pltpu.VMEM((d, ff), BF16), pltpu.VMEM((d, ff), BF16), pltpu.VMEM((ff, d), BF16),
                        pltpu.SemaphoreType.DMA((2,))])
    return pl.pallas_call(
        functools.partial(_expert_kernel, per=per),
        grid_spec=grid_spec,
        out_shape=jax.ShapeDtypeStruct((nblk * MOE_BLOCK, d), F32),
        compiler_params=_cparams(("arbitrary",)),
        name="experts",
    )(blk_e, buf_tok, buf_tok, hp, w1, w3, w2)


def _combine_kernel(ps_ref, eid_ref, rank_ref, eidn_ref, rankn_ref, y_hbm, wt_ref, x1_ref, mod_ref, g_ref, b_ref,
                    o_ref, ybuf, sem, *, alpha):
    i = pl.program_id(0)
    n = pl.num_programs(0)
    slot = i % 2
    tm = x1_ref.shape[0]

    def start(e_ref, r_ref, s):
        def body(r, carry):
            for k in range(2):
                src = ps_ref[e_ref[0, k, r]] + r_ref[0, k, r]
                _row_copy(y_hbm, src, ybuf.at[s, k], r, sem.at[s]).start()
            return carry
        lax.fori_loop(0, tm, body, 0, unroll=8)

    @pl.when(i == 0)
    def _():
        start(eid_ref, rank_ref, 0)

    @pl.when(i + 1 < n)
    def _():
        start(eidn_ref, rankn_ref, 1 - slot)

    for k in range(2):
        pltpu.make_async_copy(y_hbm.at[pl.ds(0, tm), :], ybuf.at[slot, k], sem.at[slot]).wait()
    wt = wt_ref[...]
    ffn = wt[:, 0:1] * ybuf[slot, 0] + wt[:, 1:2] * ybuf[slot, 1]
    y = alpha * x1_ref[...] + (1.0 + mod_ref[0, 5:6, :]) * ffn
    o_ref[...] = _ln(y, g_ref[...], b_ref[...])


def _combine(pstart, eid3, rank3, yb, wt_t, x1, mod3, g2, b2, seq, alpha):
    t, d = x1.shape
    n, _, tm = eid3.shape
    per_b = seq // tm
    cur = pl.BlockSpec((1, 2, tm), lambda i, ps: (i, 0, 0), memory_space=pltpu.SMEM)
    nxt = pl.BlockSpec((1, 2, tm), lambda i, ps: (jnp.minimum(i + 1, n - 1), 0, 0), memory_space=pltpu.SMEM)
    vec = pl.BlockSpec((1, d), lambda i, ps: (0, 0))
    grid_spec = pltpu.PrefetchScalarGridSpec(
        num_scalar_prefetch=1,
        grid=(n,),
        in_specs=[cur, cur, nxt, nxt,
                  pl.BlockSpec(memory_space=pl.ANY),
                  pl.BlockSpec((tm, 2), lambda i, ps: (i, 0)),
                  pl.BlockSpec((tm, d), lambda i, ps: (i, 0)),
                  pl.BlockSpec((1, 6, d), lambda i, ps: (i // per_b, 0, 0)),
                  vec, vec],
        out_specs=pl.BlockSpec((tm, d), lambda i, ps: (i, 0)),
        scratch_shapes=[pltpu.VMEM((2, 2, tm, d), F32), pltpu.SemaphoreType.DMA((2,))])
    return pl.pallas_call(
        functools.partial(_combine_kernel, alpha=alpha),
        grid_spec=grid_spec,
        out_shape=jax.ShapeDtypeStruct((t, d), F32),
        compiler_params=_cparams(("arbitrary",)),
        name="combine",
    )(pstart, eid3, rank3, eid3, rank3, yb, wt_t, x1, mod3, g2, b2)


def _pick_tile(n, want):
    tm = min(n, want)
    assert n % tm == 0
    return tm


def kernel(x, c, positions, ln0_g, ln0_b, w_ada, b_ada, w_in, w_out, ret_log_decay_f, ret_log_decay_b,
           ln1_g, ln1_b, w_group, b_group, w_sub, b_sub, w1, w3, w2, ln2_g, ln2_b):
    batch, seq, d = x.shape
    depth = w_ada.shape[0]
    t = batch * seq
    n_ah = d // 2 // ATTN_HEAD_DIM
    n_rh = d // 2 // RET_HEAD_DIM
    assert depth == 1 and d % (2 * RET_HEAD_DIM) == 0 and seq % (max(DILATIONS) * KW) == 0 and batch <= 8
    assert seq % (QT * CHAINS) == 0
    alpha = (2 * depth) ** 0.25

    inv_rope = ROPE_THETA ** (-jnp.arange(0, ROPE_DIM, 2, dtype=F32) / ROPE_DIM)
    inv_ret = RET_THETA ** (-jnp.linspace(0.0, 1.0, RET_HEAD_DIM // 2, dtype=F32))
    fa = jnp.zeros((1, LANES), F32).at[0, :ROPE_DIM].set(jnp.concatenate([inv_rope, inv_rope]))
    fr = inv_ret.reshape(1, LANES)
    posb = jnp.broadcast_to(positions.astype(F32).reshape(t, 1), (t, LANES))
    c8 = jnp.zeros((8, d), F32).at[:batch].set(c)
    row = lambda v: v.reshape(1, d)

    xs = x.reshape(t, d)
    mod = _ada(c8, w_ada[0], b_ada[0].reshape(1, -1))
    mod3 = mod[:batch].reshape(batch, 6, d)
    za, zr = _inproj(xs, posb, mod3, row(ln0_g), row(ln0_b), w_in[0].astype(BF16), fa, fr, seq, _pick_tile(seq, 512))
    attn = _attn(za, batch, seq, n_ah)
    lg = jnp.stack([ret_log_decay_f[0], ret_log_decay_b[0]]).astype(F32)
    r = _ret(zr, lg, batch, seq, n_rh, _pick_tile(seq, 1024))
    x1, hp = _outproj(attn, r, xs, mod3, row(ln0_g), row(ln0_b), row(ln1_g[0]), row(ln1_b[0]),
                      w_out[0].astype(BF16), seq, _pick_tile(seq, 256), alpha)
    wr = jnp.zeros((ROUTER_ROWS, d), F32)
    wr = wr.at[:N_GROUPS].set(w_group[0].T)
    wr = wr.at[N_GROUPS:N_GROUPS + N_EXPERTS].set(w_sub[0].transpose(0, 2, 1).reshape(N_EXPERTS, d))
    br = jnp.zeros((ROUTER_ROWS, 1), F32)
    br = br.at[:N_GROUPS, 0].set(b_group[0]).at[N_GROUPS:N_GROUPS + N_EXPERTS, 0].set(b_sub[0].reshape(-1))
    eid3, wt, rank3, cnt = _router(hp, wr.astype(BF16), br, _pick_tile(seq, 512))
    counts = cnt[:, 0].astype(I32)
    padded = (counts + MOE_BLOCK - 1) // MOE_BLOCK * MOE_BLOCK
    pend = jnp.cumsum(padded)
    pstart = pend - padded
    n_slots = 2 * t + N_EXPERTS * MOE_BLOCK
    nblk = n_slots // MOE_BLOCK
    starts = jnp.arange(nblk, dtype=I32) * MOE_BLOCK
    blk_e = jnp.minimum(jnp.sum((pend[None, :] <= starts[:, None]).astype(I32), axis=1), N_EXPERTS - 1)
    tm = eid3.shape[2]
    onehot = eid3[..., None] == jnp.arange(N_EXPERTS, dtype=I32)
    dest = rank3 + jnp.sum(jnp.where(onehot, pstart, 0), axis=-1)
    tok = jnp.arange(t, dtype=I32).reshape(-1, 1, tm) + jnp.zeros_like(dest)
    buf_tok = jnp.zeros((n_slots,), I32).at[dest.reshape(-1)].set(tok.reshape(-1))
    yb = _experts(blk_e, buf_tok.reshape(nblk, 1, MOE_BLOCK), hp, w1[0], w3[0], w2[0])
    out = _combine(pstart, eid3, rank3, yb, wt.T, x1, mod3, row(ln2_g[0]), row(ln2_b[0]), seq, alpha)
    return out.reshape(batch, seq, d)
```

```python
import functools

import jax
import jax.numpy as jnp
from jax import lax
from jax.experimental import pallas as pl
from jax.experimental.pallas import tpu as pltpu

F32 = jnp.float32
BF16 = jnp.bfloat16
I32 = jnp.int32
U32 = jnp.uint32

LANES = 128
ATTN_HEAD_DIM = 128
RET_HEAD_DIM = 256
DILATIONS = (1, 4, 16)
BAND = 64
QT = 128
KW = QT + 2 * BAND
CHAINS = 8
PERM_ROWS = 256
PERM_UNROLL = 2
ROPE_THETA = 500000.0
ROPE_DIM = ATTN_HEAD_DIM // 4
RET_THETA = 10000.0
RET_CHUNK = 128
RET_UNROLL = 4
N_GROUPS = 4
EXPERTS_PER_GROUP = 8
N_EXPERTS = N_GROUPS * EXPERTS_PER_GROUP
MOE_BLOCK = 128
LN_EPS = 1e-5
MASK_VALUE = -1e30
N_SEG = 7
ROUTER_ROWS = 48
HI_MASK = 0xFFFF0000
VMEM_LIMIT = 56 * 1024 * 1024


def _cparams(sem, vmem=VMEM_LIMIT):
    return pltpu.CompilerParams(dimension_semantics=sem, vmem_limit_bytes=vmem)


def _ln(x, g, b):
    mu = jnp.mean(x, axis=-1, keepdims=True)
    xc = x - mu
    var = jnp.mean(xc * xc, axis=-1, keepdims=True)
    return xc * lax.rsqrt(var + LN_EPS) * g + b


def _pack_bf16_pairs(h):
    bits = lax.bitcast_convert_type(h.astype(BF16).astype(F32), U32)
    n = h.shape[1] // 2
    return (bits[:, :n] >> 16) | (bits[:, n:] & jnp.uint32(HI_MASK))


def _store_token_tiles(ref, packed):
    m, n = packed.shape
    per = n // LANES
    for s in range(per):
        ref[pl.ds(s, m, stride=per), :] = packed[:, s * LANES:(s + 1) * LANES]


def _load_token_tiles(ref, m, per):
    slabs = [ref[pl.ds(s, m, stride=per), :] for s in range(per)]
    lo = [lax.bitcast_convert_type(p << 16, F32) for p in slabs]
    hi = [lax.bitcast_convert_type(p & jnp.uint32(HI_MASK), F32) for p in slabs]
    return jnp.concatenate(lo + hi, axis=-1).astype(BF16)


def _ada_kernel(c_ref, w_ref, b_ref, o_ref):
    cs = jax.nn.silu(c_ref[...])
    o_ref[...] = jnp.dot(cs.astype(BF16), w_ref[...].astype(BF16), preferred_element_type=F32) + b_ref[...]


def _ada(c8, w_ada, b_ada):
    d, n = w_ada.shape
    tn = min(n, 512)
    return pl.pallas_call(
        _ada_kernel,
        grid=(n // tn,),
        in_specs=[pl.BlockSpec((8, d), lambda j: (0, 0)),
                  pl.BlockSpec((d, tn), lambda j: (0, j)),
                  pl.BlockSpec((1, tn), lambda j: (0, j))],
        out_specs=pl.BlockSpec((8, tn), lambda j: (0, j)),
        out_shape=jax.ShapeDtypeStruct((8, n), F32),
        compiler_params=_cparams(("arbitrary",)),
        name="ada",
    )(c8, w_ada, b_ada)


def _inproj_kernel(x_ref, pos_ref, mod_ref, g_ref, b_ref, w_ref, fa_ref, fr_ref, za_ref, zr_ref,
                   h_ref, ca_ref, sa1_ref, sa2_ref, cr_ref, sr_ref, *, n_ah, n_rh):
    j = pl.program_id(1)

    @pl.when(j == 0)
    def _():
        xn = _ln(x_ref[...], g_ref[...], b_ref[...])
        h = xn * (1.0 + mod_ref[0, 1:2, :]) + mod_ref[0, 0:1, :]
        h_ref[...] = h.astype(BF16)
        pos = pos_ref[...]
        ang = pos * fa_ref[...]
        c, s = jnp.cos(ang), jnp.sin(ang)
        lane = lax.broadcasted_iota(I32, ang.shape, 1)
        half = ROPE_DIM // 2
        ca_ref[...] = c
        sa1_ref[...] = jnp.where(lane < half, -s, 0.0)
        sa2_ref[...] = jnp.where((lane >= half) & (lane < 2 * half), s, 0.0)
        angr = pos * fr_ref[...]
        cr_ref[...] = jnp.cos(angr)
        sr_ref[...] = jnp.sin(angr)

    res = jnp.dot(h_ref[...], w_ref[...], preferred_element_type=F32)

    def attn_rot(scale):
        ca, sa1, sa2 = ca_ref[...], sa1_ref[...], sa2_ref[...]
        half = ROPE_DIM // 2
        for hc in range(n_ah):
            xh = res[:, hc * LANES:(hc + 1) * LANES]
            r = xh * ca + pltpu.roll(xh, LANES - half, 1) * sa1 + pltpu.roll(xh, half, 1) * sa2
            if scale is not None:
                r = r * scale
            za_ref[hc] = r.astype(BF16)

    def ret_rot(scale):
        cr, sr = cr_ref[...], sr_ref[...]
        for hc in range(n_rh):
            x1 = res[:, hc * RET_HEAD_DIM:hc * RET_HEAD_DIM + LANES]
            x2 = res[:, hc * RET_HEAD_DIM + LANES:(hc + 1) * RET_HEAD_DIM]
            o1 = x1 * cr - x2 * sr
            o2 = x2 * cr + x1 * sr
            if scale is not None:
                o1, o2 = o1 * scale, o2 * scale
            zr_ref[hc, :, 0:LANES] = o1.astype(BF16)
            zr_ref[hc, :, LANES:RET_HEAD_DIM] = o2.astype(BF16)

    @pl.when(j == 0)
    def _():
        attn_rot(ATTN_HEAD_DIM ** -0.5)

    @pl.when(j == 1)
    def _():
        attn_rot(None)

    @pl.when(j == 2)
    def _():
        for hc in range(n_ah):
            za_ref[hc] = res[:, hc * LANES:(hc + 1) * LANES].astype(BF16)

    @pl.when(j == 3)
    def _():
        ret_rot(None)

    @pl.when(j == 4)
    def _():
        ret_rot(RET_HEAD_DIM ** -0.5)

    @pl.when(j >= 5)
    def _():
        for hc in range(n_rh):
            zr_ref[hc] = res[:, hc * RET_HEAD_DIM:(hc + 1) * RET_HEAD_DIM].astype(BF16)


def _inproj(x2, posb, mod3, ln0_g, ln0_b, w_in_b, fa, fr, seq, tm):
    t, d = x2.shape
    tn = d // 2
    n_ah, n_rh = tn // ATTN_HEAD_DIM, tn // RET_HEAD_DIM
    per_b = seq // tm
    return pl.pallas_call(
        functools.partial(_inproj_kernel, n_ah=n_ah, n_rh=n_rh),
        grid=(t // tm, N_SEG),
        in_specs=[pl.BlockSpec((tm, d), lambda i, j: (i, 0)),
                  pl.BlockSpec((tm, LANES), lambda i, j: (i, 0)),
                  pl.BlockSpec((1, 6, d), lambda i, j: (i // per_b, 0, 0)),
                  pl.BlockSpec((1, d), lambda i, j: (0, 0)),
                  pl.BlockSpec((1, d), lambda i, j: (0, 0)),
                  pl.BlockSpec((d, tn), lambda i, j: (0, j)),
                  pl.BlockSpec((1, LANES), lambda i, j: (0, 0)),
                  pl.BlockSpec((1, LANES), lambda i, j: (0, 0))],
        out_specs=[pl.BlockSpec((n_ah, tm, ATTN_HEAD_DIM), lambda i, j: (jnp.minimum(j, 2), i, 0)),
                   pl.BlockSpec((n_rh, tm, RET_HEAD_DIM), lambda i, j: (jnp.maximum(j - 3, 0), i, 0))],
        out_shape=[jax.ShapeDtypeStruct((3 * n_ah, t, ATTN_HEAD_DIM), BF16),
                   jax.ShapeDtypeStruct((4 * n_rh, t, RET_HEAD_DIM), BF16)],
        scratch_shapes=[pltpu.VMEM((tm, d), BF16)] + [pltpu.VMEM((tm, LANES), F32)] * 5,
        compiler_params=_cparams(("arbitrary", "arbitrary")),
        name="inproj",
    )(x2, posb, mod3, ln0_g, ln0_b, w_in_b, fa, fr)


def _attn_tile(q, k, v, bias):
    s = lax.dot_general(q, k, (((1,), (1,)), ((), ())), preferred_element_type=F32) + bias
    m = jnp.max(jnp.maximum(s[:, :LANES], s[:, LANES:]), axis=-1, keepdims=True)
    p = jnp.exp(s - m)
    den = jnp.sum(p[:, :LANES] + p[:, LANES:], axis=-1, keepdims=True)
    acc = jnp.dot(p.astype(BF16), v, preferred_element_type=F32)
    return acc, jnp.broadcast_to(m, acc.shape), jnp.broadcast_to(den, acc.shape)


def _attn_kernel(q_ref, k_ref, v_ref, bias_ref, p4_ref, p16_ref, o_ref,
                 acc_ref, m_ref, d_ref, qp_ref, kp_ref, vp_ref, *, seq):
    n_items = seq // QT

    def window(n, length):
        qs = n * QT
        ks = jnp.clip(qs - BAND, 0, length - KW)
        return qs, ks, bias_ref[lax.div(qs - ks, BAND)]

    def perm_body(it, carry):
        for u in range(PERM_UNROLL):
            blk = it * PERM_UNROLL + u
            r0 = pl.multiple_of(blk * PERM_ROWS, PERM_ROWS)
            qk = jnp.concatenate([q_ref[0, 0, pl.ds(r0, PERM_ROWS), :], k_ref[0, 0, pl.ds(r0, PERM_ROWS), :]], axis=-1)
            v = v_ref[0, 0, pl.ds(r0, PERM_ROWS), :]
            for slot, (dil, p_ref) in enumerate(((4, p4_ref), (16, p16_ref))):
                length, w = seq // dil, PERM_ROWS // dil
                qkp = jnp.dot(p_ref[...], qk, preferred_element_type=F32).astype(BF16)
                vp = jnp.dot(p_ref[...], v, preferred_element_type=F32).astype(BF16)
                for r in range(dil):
                    rows = pl.ds(pl.multiple_of(r * length + blk * w, w), w)
                    qp_ref[slot, rows, :] = qkp[r * w:(r + 1) * w, :ATTN_HEAD_DIM]
                    kp_ref[slot, rows, :] = qkp[r * w:(r + 1) * w, ATTN_HEAD_DIM:]
                    vp_ref[slot, rows, :] = vp[r * w:(r + 1) * w, :]
        return carry

    lax.fori_loop(0, seq // PERM_ROWS // PERM_UNROLL, perm_body, 0)

    def dilated_tile(item, dil, slot):
        length = seq // dil
        n, r = lax.div(item, dil), lax.rem(item, dil)
        qs, ks, bias = window(n, length)
        qrow = pl.multiple_of(r * length + qs, QT)
        krow = pl.multiple_of(r * length + ks, BAND)
        out = _attn_tile(qp_ref[slot, pl.ds(qrow, QT), :], kp_ref[slot, pl.ds(krow, KW), :],
                         vp_ref[slot, pl.ds(krow, KW), :], bias)
        return out, pl.ds(qs * dil + r, QT, stride=dil)

    def body16(it, carry):
        for g in range(CHAINS):
            (acc, m, den), rows = dilated_tile(it * CHAINS + g, 16, 1)
            acc_ref[rows, :] = acc
            m_ref[rows, :] = m
            d_ref[rows, :] = den
        return carry

    lax.fori_loop(0, n_items // CHAINS, body16, 0)

    def body4(it, carry):
        for g in range(CHAINS):
            (acc, m, den), rows = dilated_tile(it * CHAINS + g, 4, 0)
            m0 = m_ref[rows, :]
            mn = jnp.maximum(m0, m)
            a, b = jnp.exp(m0 - mn), jnp.exp(m - mn)
            acc_ref[rows, :] = a * acc_ref[rows, :] + b * acc
            d_ref[rows, :] = a * d_ref[rows, :] + b * den
            m_ref[rows, :] = mn
        return carry

    lax.fori_loop(0, n_items // CHAINS, body4, 0)

    def body1(it, carry):
        for g in range(CHAINS):
            qs, ks, bias = window(it * CHAINS + g, seq)
            qs, ks = pl.multiple_of(qs, QT), pl.multiple_of(ks, BAND)
            acc, m, den = _attn_tile(q_ref[0, 0, pl.ds(qs, QT), :], k_ref[0, 0, pl.ds(ks, KW), :],
                                     v_ref[0, 0, pl.ds(ks, KW), :], bias)
            rows = pl.ds(qs, QT)
            m0 = m_ref[rows, :]
            mn = jnp.maximum(m0, m)
            a, b = jnp.exp(m0 - mn), jnp.exp(m - mn)
            num = a * acc_ref[rows, :] + b * acc
            o_ref[0, 0, rows, :] = (num / (a * d_ref[rows, :] + b * den)).astype(BF16)
        return carry

    lax.fori_loop(0, n_items // CHAINS, body1, 0)


def _perm_matrix(dil):
    w = PERM_ROWS // dil
    i = jnp.arange(PERM_ROWS)
    src = (i % w) * dil + i // w
    return (src[:, None] == jnp.arange(PERM_ROWS)[None, :]).astype(BF16)


def _attn(za, batch, seq, n_ah):
    zv = za.reshape(3 * n_ah, batch, seq, ATTN_HEAD_DIM)
    blk = (1, 1, seq, ATTN_HEAD_DIM)
    i = jnp.arange(QT)[:, None]
    jj = jnp.arange(KW)[None, :]
    bias = jnp.stack([jnp.where(jnp.abs(i + o * BAND - jj) <= BAND, 0.0, MASK_VALUE) for o in range(3)]).astype(F32)
    const = lambda shape: pl.BlockSpec(shape, lambda b, h: (0,) * len(shape))
    o = pl.pallas_call(
        functools.partial(_attn_kernel, seq=seq),
        grid=(batch, n_ah),
        in_specs=[pl.BlockSpec(blk, lambda b, h: (h, b, 0, 0)),
                  pl.BlockSpec(blk, lambda b, h: (n_ah + h, b, 0, 0)),
                  pl.BlockSpec(blk, lambda b, h: (2 * n_ah + h, b, 0, 0)),
                  const((3, QT, KW)), const((PERM_ROWS, PERM_ROWS)), const((PERM_ROWS, PERM_ROWS))],
        out_specs=pl.BlockSpec(blk, lambda b, h: (h, b, 0, 0)),
        out_shape=jax.ShapeDtypeStruct((n_ah, batch, seq, ATTN_HEAD_DIM), BF16),
        scratch_shapes=[pltpu.VMEM((seq, LANES), F32)] * 3
        + [pltpu.VMEM((2, seq, ATTN_HEAD_DIM), BF16)] * 3,
        compiler_params=_cparams(("arbitrary", "arbitrary")),
        name="attn",
    )(zv, zv, zv, bias, _perm_matrix(4), _perm_matrix(16))
    return o.reshape(n_ah, batch * seq, ATTN_HEAD_DIM)


def _ret_kernel(lg_ref, q_ref, k_ref, v_ref, g_ref, o_ref, y_ref, sf_ref, sb_ref, dmat_ref, *, n_tiles, tile):
    h = pl.program_id(1)
    t = pl.program_id(2)
    c = RET_CHUNK
    lgf = lg_ref[0, h]
    lgb = lg_ref[1, h]
    col = lax.broadcasted_iota(I32, (c, 1), 0).astype(F32)
    row = lax.broadcasted_iota(I32, (1, c), 1).astype(F32)
    n_iter = tile // c // RET_UNROLL

    @pl.when(t == 0)
    def _():
        sf_ref[...] = jnp.zeros_like(sf_ref)
        sb_ref[...] = jnp.zeros_like(sb_ref)
        diff = (lax.broadcasted_iota(I32, (c, c), 0) - lax.broadcasted_iota(I32, (c, c), 1)).astype(F32)
        dmat_ref[...] = jnp.where(diff >= 0, jnp.exp(lgf * jnp.maximum(diff, 0.0)), 0.0) \
            + jnp.where(diff < 0, jnp.exp(lgb * jnp.maximum(-diff, 0.0)), 0.0)

    def chunk(r0):
        rows = pl.ds(pl.multiple_of(r0, c), c)
        return rows, q_ref[0, rows, :], k_ref[0, rows, :], v_ref[0, rows, :]

    @pl.when(t < n_tiles)
    def _():
        xi = jnp.exp(lgf * (col + 1.0))
        zeta = jnp.exp(lgf * (c - 1.0 - row))
        cdec = jnp.exp(lgf * jnp.full((1, 1), float(c), F32))
        base = t * tile

        def body(it, carry):
            state = sf_ref[...]
            for u in range(RET_UNROLL):
                r0 = (it * RET_UNROLL + u) * c
                _, q, k, v = chunk(r0)
                s = lax.dot_general(q, k, (((1,), (1,)), ((), ())), preferred_element_type=F32) * dmat_ref[...]
                y = jnp.dot(s.astype(BF16), v, preferred_element_type=F32)
                y = y + jnp.dot(q, state.astype(BF16), preferred_element_type=F32) * xi
                kt = (k.astype(F32).T * zeta).astype(BF16)
                state = state * cdec + jnp.dot(kt, v, preferred_element_type=F32)
                y_ref[pl.ds(pl.multiple_of(base + r0, c), c), :] = y
            sf_ref[...] = state
            return carry

        lax.fori_loop(0, n_iter, body, 0)

    @pl.when(t >= n_tiles)
    def _():
        xi = jnp.exp(lgb * (c - col))
        zeta = jnp.exp(lgb * row)
        cdec = jnp.exp(lgb * jnp.full((1, 1), float(c), F32))
        base = (2 * n_tiles - 1 - t) * tile

        def body(it, carry):
            state = sb_ref[...]
            for u in range(RET_UNROLL):
                r0 = (tile // c - 1 - (it * RET_UNROLL + u)) * c
                rows, q, k, v = chunk(r0)
                y = y_ref[pl.ds(pl.multiple_of(base + r0, c), c), :]
                y = y + jnp.dot(q, state.astype(BF16), preferred_element_type=F32) * xi
                kt = (k.astype(F32).T * zeta).astype(BF16)
                state = state * cdec + jnp.dot(kt, v, preferred_element_type=F32)
                mu = jnp.mean(y, axis=-1, keepdims=True)
                yc = y - mu
                var = jnp.mean(yc * yc, axis=-1, keepdims=True)
                yn = yc * lax.rsqrt(var + LN_EPS)
                gate = jax.nn.silu(g_ref[0, rows, :].astype(F32))
                o_ref[0, rows, :] = (gate * yn).astype(BF16)
            sb_ref[...] = state
            return carry

        lax.fori_loop(0, n_iter, body, 0)


def _ret(zr, lg, batch, seq, n_rh, tile):
    n_tiles = seq // tile
    per_b = seq // tile

    def rows(t):
        return jnp.where(t < n_tiles, t, 2 * n_tiles - 1 - t)

    def spec(seg):
        return pl.BlockSpec((1, tile, RET_HEAD_DIM), lambda b, h, t: (seg * n_rh + h, b * per_b + rows(t), 0))

    return pl.pallas_call(
        functools.partial(_ret_kernel, n_tiles=n_tiles, tile=tile),
        grid=(batch, n_rh, 2 * n_tiles),
        in_specs=[pl.BlockSpec(memory_space=pltpu.SMEM), spec(0), spec(1), spec(2), spec(3)],
        out_specs=pl.BlockSpec((1, tile, RET_HEAD_DIM),
                               lambda b, h, t: (h, b * per_b + jnp.where(t < n_tiles, n_tiles - 1, 2 * n_tiles - 1 - t), 0)),
        out_shape=jax.ShapeDtypeStruct((n_rh, batch * seq, RET_HEAD_DIM), BF16),
        scratch_shapes=[pltpu.VMEM((seq, RET_HEAD_DIM), F32),
                        pltpu.VMEM((RET_HEAD_DIM, RET_HEAD_DIM), F32),
                        pltpu.VMEM((RET_HEAD_DIM, RET_HEAD_DIM), F32),
                        pltpu.VMEM((RET_CHUNK, RET_CHUNK), F32)],
        compiler_params=_cparams(("arbitrary", "arbitrary", "arbitrary")),
        name="ret",
    )(lg, zr, zr, zr, zr)


def _outproj_kernel(a_ref, r_ref, x_ref, mod_ref, g0_ref, b0_ref, g1_ref, b1_ref, w_ref, x1_ref, hp_ref,
                    *, n_ah, n_rh, alpha):
    mix = jnp.concatenate([a_ref[h] for h in range(n_ah)] + [r_ref[h] for h in range(n_rh)], axis=-1)
    acc = jnp.dot(mix, w_ref[...], preferred_element_type=F32)
    xn = _ln(x_ref[...], g0_ref[...], b0_ref[...])
    y = alpha * xn + (1.0 + mod_ref[0, 2:3, :]) * acc
    x1 = _ln(y, g1_ref[...], b1_ref[...])
    x1_ref[...] = x1
    _store_token_tiles(hp_ref, _pack_bf16_pairs(x1 * (1.0 + mod_ref[0, 4:5, :]) + mod_ref[0, 3:4, :]))


def _outproj(attn, r, x2, mod3, g0, b0, g1, b1, w_out_b, seq, tm, alpha):
    t, d = x2.shape
    n_ah, n_rh = attn.shape[0], r.shape[0]
    per_b = seq // tm
    per = d // 2 // LANES
    row = lambda i: (i, 0)
    vec = pl.BlockSpec((1, d), lambda i: (0, 0))
    return pl.pallas_call(
        functools.partial(_outproj_kernel, n_ah=n_ah, n_rh=n_rh, alpha=alpha),
        grid=(t // tm,),
        in_specs=[pl.BlockSpec((n_ah, tm, ATTN_HEAD_DIM), lambda i: (0, i, 0)),
                  pl.BlockSpec((n_rh, tm, RET_HEAD_DIM), lambda i: (0, i, 0)),
                  pl.BlockSpec((tm, d), row),
                  pl.BlockSpec((1, 6, d), lambda i: (i // per_b, 0, 0)),
                  vec, vec, vec, vec,
                  pl.BlockSpec((d, d), lambda i: (0, 0))],
        out_specs=[pl.BlockSpec((tm, d), row), pl.BlockSpec((tm * per, LANES), row)],
        out_shape=[jax.ShapeDtypeStruct((t, d), F32), jax.ShapeDtypeStruct((t * per, LANES), U32)],
        compiler_params=_cparams(("arbitrary",)),
        name="outproj",
    )(attn, r, x2, mod3, g0, b0, g1, b1, w_out_b)


def _first_argmax(rows):
    best, idx = rows[0], jnp.zeros(rows[0].shape, I32)
    for e in range(1, len(rows)):
        better = rows[e] > best
        idx = jnp.where(better, e, idx)
        best = jnp.maximum(best, rows[e])
    return best, idx


def _router_kernel(h_ref, w_ref, b_ref, u_ref, eid_ref, wt_ref, rank_ref, cnt_ref, carry_ref, *, per):
    i = pl.program_id(0)

    @pl.when(i == 0)
    def _():
        carry_ref[...] = jnp.zeros_like(carry_ref)

    tm = eid_ref.shape[2]
    lt = lax.dot_general(w_ref[...], _load_token_tiles(h_ref, tm, per), (((1,), (1,)), ((), ())),
                         preferred_element_type=F32) + b_ref[...]
    grow = [lt[g:g + 1, :] for g in range(N_GROUPS)]
    gmax, gsel = _first_argmax(grow)
    gsum = grow[0] * 0.0
    for g in range(N_GROUPS):
        gsum = gsum + jnp.exp(grow[g] - gmax)
    pg = 1.0 / gsum
    srow = []
    for e in range(EXPERTS_PER_GROUP):
        r = lt[N_GROUPS + e:N_GROUPS + e + 1, :]
        for g in range(1, N_GROUPS):
            o = N_GROUPS + g * EXPERTS_PER_GROUP + e
            r = jnp.where(gsel == g, lt[o:o + 1, :], r)
        srow.append(r)
    v1, i1 = _first_argmax(srow)
    v2, i2 = _first_argmax([jnp.where(i1 == e, -jnp.inf, srow[e]) for e in range(EXPERTS_PER_GROUP)])
    e2 = jnp.exp(v2 - v1)
    den = 1.0 + e2
    wt_ref[0:1, :] = (1.0 / den) * pg
    wt_ref[1:2, :] = (e2 / den) * pg
    eid0 = gsel * EXPERTS_PER_GROUP + i1
    eid1 = gsel * EXPERTS_PER_GROUP + i2
    eid_ref[0, 0:1, :] = eid0
    eid_ref[0, 1:2, :] = eid1
    erow = lax.broadcasted_iota(I32, (N_EXPERTS, tm), 0)
    oh0 = (erow == eid0).astype(F32)
    oh1 = (erow == eid1).astype(F32)
    oh = oh0 + oh1
    incl = jnp.dot(oh.astype(BF16), u_ref[...], preferred_element_type=F32)
    before = carry_ref[:, 0:1] + incl - oh
    rank_ref[0, 0:1, :] = jnp.sum(oh0 * before, axis=0, keepdims=True).astype(I32)
    rank_ref[0, 1:2, :] = jnp.sum(oh1 * before, axis=0, keepdims=True).astype(I32)
    carry = carry_ref[...] + jnp.sum(oh, axis=1, keepdims=True)
    carry_ref[...] = carry
    cnt_ref[...] = carry


def _router(hp, wr, br, tm):
    d = wr.shape[1]
    per = d // 2 // LANES
    t = hp.shape[0] // per
    tri = (lax.broadcasted_iota(I32, (tm, tm), 0) <= lax.broadcasted_iota(I32, (tm, tm), 1)).astype(BF16)
    tile3 = pl.BlockSpec((1, 2, tm), lambda i: (i, 0, 0))
    return pl.pallas_call(
        functools.partial(_router_kernel, per=per),
        grid=(t // tm,),
        in_specs=[pl.BlockSpec((tm * per, LANES), lambda i: (i, 0)),
                  pl.BlockSpec((ROUTER_ROWS, d), lambda i: (0, 0)),
                  pl.BlockSpec((ROUTER_ROWS, 1), lambda i: (0, 0)),
                  pl.BlockSpec((tm, tm), lambda i: (0, 0))],
        out_specs=[tile3, pl.BlockSpec((2, tm), lambda i: (0, i)), tile3,
                   pl.BlockSpec((N_EXPERTS, LANES), lambda i: (0, 0))],
        out_shape=[jax.ShapeDtypeStruct((t // tm, 2, tm), I32), jax.ShapeDtypeStruct((2, t), F32),
                   jax.ShapeDtypeStruct((t // tm, 2, tm), I32), jax.ShapeDtypeStruct((N_EXPERTS, LANES), F32)],
        scratch_shapes=[pltpu.VMEM((N_EXPERTS, LANES), F32)],
        compiler_params=_cparams(("arbitrary",)),
        name="router",
    )(hp, wr, br, tri)


def _tile_copy(src, s, dst, d, per, sem):
    return pltpu.make_async_copy(src.at[pl.ds(pl.multiple_of(s * per, per), per), :],
                                 dst.at[pl.ds(pl.multiple_of(d * per, per), per), :], sem)


def _dispatch_kernel(ps_ref, lo_ref, hi_ref, eid_ref, rank_ref, h_ref, xb_hbm, stage, zero, sem, *, per):
    i = pl.program_id(0)
    n = pl.num_programs(0)
    slot = i % 2
    tm = eid_ref.shape[2]

    def wait_tile(s):
        pltpu.make_async_copy(xb_hbm.at[pl.ds(0, 2 * tm * per), :], xb_hbm.at[pl.ds(0, 2 * tm * per), :], sem.at[s]).wait()

    @pl.when(i >= 2)
    def _():
        wait_tile(slot)

    stage[slot] = h_ref[...]

    def body(r, carry):
        for k in range(2):
            dst = ps_ref[eid_ref[0, k, r]] + rank_ref[0, k, r]
            _tile_copy(stage.at[slot], r, xb_hbm, dst, per, sem.at[slot]).start()
        return carry

    lax.fori_loop(0, tm, body, 0, unroll=8)

    @pl.when(i == n - 1)
    def _():
        @pl.when(n >= 2)
        def _():
            wait_tile(1 - slot)
        wait_tile(slot)
        zero[...] = jnp.zeros_like(zero)

        def fill(e, carry):
            def one(s, c):
                _tile_copy(zero, 0, xb_hbm, s, per, sem.at[2]).start()
                return c
            lax.fori_loop(lo_ref[e], hi_ref[e], one, 0)

            def one_wait(s, c):
                _tile_copy(zero, 0, xb_hbm, s, per, sem.at[2]).wait()
                return c
            lax.fori_loop(lo_ref[e], hi_ref[e], one_wait, 0)
            return carry

        lax.fori_loop(0, N_EXPERTS + 1, fill, 0)


def _dispatch(pstart, fill_lo, fill_hi, eid3, rank3, hp, n_slots, per):
    n, _, tm = eid3.shape
    smem3 = pl.BlockSpec((1, 2, tm), lambda i, *_: (i, 0, 0), memory_space=pltpu.SMEM)
    grid_spec = pltpu.PrefetchScalarGridSpec(
        num_scalar_prefetch=3,
        grid=(n,),
        in_specs=[smem3, smem3, pl.BlockSpec((tm * per, LANES), lambda i, *_: (i, 0))],
        out_specs=pl.BlockSpec(memory_space=pl.ANY),
        scratch_shapes=[pltpu.VMEM((2, tm * per, LANES), U32), pltpu.VMEM((per, LANES), U32),
                        pltpu.SemaphoreType.DMA((3,))])
    return pl.pallas_call(
        functools.partial(_dispatch_kernel, per=per),
        grid_spec=grid_spec,
        out_shape=jax.ShapeDtypeStruct((n_slots * per, LANES), U32),
        compiler_params=_cparams(("arbitrary",)),
        name="dispatch",
    )(pstart, fill_lo, fill_hi, eid3, rank3, hp)


def _expert_kernel(be_ref, x_ref, w1_ref, w3_ref, w2_ref, y_ref, w1b, w3b, w2b, *, per):
    i = pl.program_id(0)
    changed = (i == 0) | (be_ref[i] != be_ref[jnp.maximum(i - 1, 0)])

    @pl.when(changed)
    def _():
        w1b[...] = w1_ref[0].astype(BF16)
        w3b[...] = w3_ref[0].astype(BF16)
        w2b[...] = w2_ref[0].astype(BF16)

    x = _load_token_tiles(x_ref, MOE_BLOCK, per)
    a = jnp.dot(x, w1b[...], preferred_element_type=F32)
    b = jnp.dot(x, w3b[...], preferred_element_type=F32)
    mid = (jax.nn.silu(a) * b).astype(BF16)
    _store_token_tiles(y_ref, _pack_bf16_pairs(jnp.dot(mid, w2b[...], preferred_element_type=F32)))


def _experts(blk_e, xb, w1, w3, w2):
    d, ff = w1.shape[1], w1.shape[2]
    per = d // 2 // LANES
    nblk = xb.shape[0] // per // MOE_BLOCK
    rows = pl.BlockSpec((MOE_BLOCK * per, LANES), lambda i, be: (i, 0))
    grid_spec = pltpu.PrefetchScalarGridSpec(
        num_scalar_prefetch=1,
        grid=(nblk,),
        in_specs=[rows,
                  pl.BlockSpec((1, d, ff), lambda i, be: (be[i], 0, 0)),
                  pl.BlockSpec((1, d, ff), lambda i, be: (be[i], 0, 0)),
                  pl.BlockSpec((1, ff, d), lambda i, be: (be[i], 0, 0))],
        out_specs=rows,
        scratch_shapes=[pltpu.VMEM((d, ff), BF16), pltpu.VMEM((d, ff), BF16), pltpu.VMEM((ff, d), BF16)])
    return pl.pallas_call(
        functools.partial(_expert_kernel, per=per),
        grid_spec=grid_spec,
        out_shape=jax.ShapeDtypeStruct(xb.shape, U32),
        compiler_params=_cparams(("arbitrary",)),
        name="experts",
    )(blk_e, xb, w1, w3, w2)


def _combine_kernel(ps_ref, eid_ref, rank_ref, eidn_ref, rankn_ref, y_hbm, wt_ref, x1_ref, mod_ref, g_ref, b_ref,
                    o_ref, ybuf, sem, *, alpha, per):
    i = pl.program_id(0)
    n = pl.num_programs(0)
    slot = i % 2
    tm = x1_ref.shape[0]

    def start(e_ref, r_ref, s):
        def body(r, carry):
            for k in range(2):
                src = ps_ref[e_ref[0, k, r]] + r_ref[0, k, r]
                _tile_copy(y_hbm, src, ybuf.at[s, k], r, per, sem.at[s]).start()
            return carry
        lax.fori_loop(0, tm, body, 0, unroll=8)

    @pl.when(i == 0)
    def _():
        start(eid_ref, rank_ref, 0)

    @pl.when(i + 1 < n)
    def _():
        start(eidn_ref, rankn_ref, 1 - slot)

    for k in range(2):
        pltpu.make_async_copy(y_hbm.at[pl.ds(0, tm * per), :], ybuf.at[slot, k], sem.at[slot]).wait()
    wt = wt_ref[...]
    y0 = _load_token_tiles(ybuf.at[slot, 0], tm, per).astype(F32)
    y1 = _load_token_tiles(ybuf.at[slot, 1], tm, per).astype(F32)
    ffn = wt[:, 0:1] * y0 + wt[:, 1:2] * y1
    y = alpha * x1_ref[...] + (1.0 + mod_ref[0, 5:6, :]) * ffn
    o_ref[...] = _ln(y, g_ref[...], b_ref[...])


def _combine(pstart, eid3, rank3, yb, wt_t, x1, mod3, g2, b2, seq, alpha):
    t, d = x1.shape
    per = d // 2 // LANES
    n, _, tm = eid3.shape
    per_b = seq // tm
    cur = pl.BlockSpec((1, 2, tm), lambda i, ps: (i, 0, 0), memory_space=pltpu.SMEM)
    nxt = pl.BlockSpec((1, 2, tm), lambda i, ps: (jnp.minimum(i + 1, n - 1), 0, 0), memory_space=pltpu.SMEM)
    vec = pl.BlockSpec((1, d), lambda i, ps: (0, 0))
    grid_spec = pltpu.PrefetchScalarGridSpec(
        num_scalar_prefetch=1,
        grid=(n,),
        in_specs=[cur, cur, nxt, nxt,
                  pl.BlockSpec(memory_space=pl.ANY),
                  pl.BlockSpec((tm, 2), lambda i, ps: (i, 0)),
                  pl.BlockSpec((tm, d), lambda i, ps: (i, 0)),
                  pl.BlockSpec((1, 6, d), lambda i, ps: (i // per_b, 0, 0)),
                  vec, vec],
        out_specs=pl.BlockSpec((tm, d), lambda i, ps: (i, 0)),
        scratch_shapes=[pltpu.VMEM((2, 2, tm * per, LANES), U32), pltpu.SemaphoreType.DMA((2,))])
    return pl.pallas_call(
        functools.partial(_combine_kernel, alpha=alpha, per=per),
        grid_spec=grid_spec,
        out_shape=jax.ShapeDtypeStruct((t, d), F32),
        compiler_params=_cparams(("arbitrary",)),
        name="combine",
    )(pstart, eid3, rank3, eid3, rank3, yb, wt_t, x1, mod3, g2, b2)


def _pick_tile(n, want):
    tm = min(n, want)
    assert n % tm == 0
    return tm


def kernel(x, c, positions, ln0_g, ln0_b, w_ada, b_ada, w_in, w_out, ret_log_decay_f, ret_log_decay_b,
           ln1_g, ln1_b, w_group, b_group, w_sub, b_sub, w1, w3, w2, ln2_g, ln2_b):
    batch, seq, d = x.shape
    depth = w_ada.shape[0]
    t = batch * seq
    n_ah = d // 2 // ATTN_HEAD_DIM
    n_rh = d // 2 // RET_HEAD_DIM
    assert depth == 1 and d % (2 * RET_HEAD_DIM) == 0 and seq % (max(DILATIONS) * KW) == 0 and batch <= 8
    assert seq % (QT * CHAINS) == 0
    alpha = (2 * depth) ** 0.25

    inv_rope = ROPE_THETA ** (-jnp.arange(0, ROPE_DIM, 2, dtype=F32) / ROPE_DIM)
    inv_ret = RET_THETA ** (-jnp.linspace(0.0, 1.0, RET_HEAD_DIM // 2, dtype=F32))
    fa = jnp.zeros((1, LANES), F32).at[0, :ROPE_DIM].set(jnp.concatenate([inv_rope, inv_rope]))
    fr = inv_ret.reshape(1, LANES)
    posb = jnp.broadcast_to(positions.astype(F32).reshape(t, 1), (t, LANES))
    c8 = jnp.zeros((8, d), F32).at[:batch].set(c)
    row = lambda v: v.reshape(1, d)

    xs = x.reshape(t, d)
    mod = _ada(c8, w_ada[0], b_ada[0].reshape(1, -1))
    mod3 = mod[:batch].reshape(batch, 6, d)
    za, zr = _inproj(xs, posb, mod3, row(ln0_g), row(ln0_b), w_in[0].astype(BF16), fa, fr, seq, _pick_tile(seq, 512))
    attn = _attn(za, batch, seq, n_ah)
    lg = jnp.stack([ret_log_decay_f[0], ret_log_decay_b[0]]).astype(F32)
    r = _ret(zr, lg, batch, seq, n_rh, _pick_tile(seq, 1024))
    x1, hp = _outproj(attn, r, xs, mod3, row(ln0_g), row(ln0_b), row(ln1_g[0]), row(ln1_b[0]),
                      w_out[0].astype(BF16), seq, _pick_tile(seq, 256), alpha)
    wr = jnp.zeros((ROUTER_ROWS, d), F32)
    wr = wr.at[:N_GROUPS].set(w_group[0].T)
    wr = wr.at[N_GROUPS:N_GROUPS + N_EXPERTS].set(w_sub[0].transpose(0, 2, 1).reshape(N_EXPERTS, d))
    br = jnp.zeros((ROUTER_ROWS, 1), F32)
    br = br.at[:N_GROUPS, 0].set(b_group[0]).at[N_GROUPS:N_GROUPS + N_EXPERTS, 0].set(b_sub[0].reshape(-1))
    eid3, wt, rank3, cnt = _router(hp, wr.astype(BF16), br, _pick_tile(seq, 512))
    counts = cnt[:, 0].astype(I32)
    padded = (counts + MOE_BLOCK - 1) // MOE_BLOCK * MOE_BLOCK
    pend = jnp.cumsum(padded)
    pstart = pend - padded
    n_slots = 2 * t + N_EXPERTS * MOE_BLOCK
    nblk = n_slots // MOE_BLOCK
    starts = jnp.arange(nblk, dtype=I32) * MOE_BLOCK
    blk_e = jnp.minimum(jnp.sum((pend[None, :] <= starts[:, None]).astype(I32), axis=1), N_EXPERTS - 1)
    fill_lo = jnp.concatenate([pstart + counts, pend[-1:]])
    fill_hi = jnp.concatenate([pend, jnp.full((1,), n_slots, I32)])
    xb = _dispatch(pstart, fill_lo, fill_hi, eid3, rank3, hp, n_slots, d // 2 // LANES)
    yb = _experts(blk_e, xb, w1[0], w3[0], w2[0])
    out = _combine(pstart, eid3, rank3, yb, wt.T, x1, mod3, row(ln2_g[0]), row(ln2_b[0]), seq, alpha)
    return out.reshape(batch, seq, d)
```

```python
import functools

import jax
import jax.numpy as jnp
from jax import lax
from jax.experimental import pallas as pl
from jax.experimental.pallas import tpu as pltpu

F32 = jnp.float32
BF16 = jnp.bfloat16
I32 = jnp.int32
U32 = jnp.uint32

LANES = 128
ATTN_HEAD_DIM = 128
RET_HEAD_DIM = 256
DILATIONS = (1, 4, 16)
BAND = 64
QT = 128
KW = QT + 2 * BAND
CHAINS = 8
PERM_ROWS = 256
PERM_UNROLL = 2
ROPE_THETA = 500000.0
ROPE_DIM = ATTN_HEAD_DIM // 4
RET_THETA = 10000.0
RET_CHUNK = 128
RET_UNROLL = 4
N_GROUPS = 4
EXPERTS_PER_GROUP = 8
N_EXPERTS = N_GROUPS * EXPERTS_PER_GROUP
MOE_BLOCK = 128
LN_EPS = 1e-5
MASK_VALUE = -1e30
EPI_ROWS = 64
N_SEG = 7
ROUTER_ROWS = 48
HI_MASK = 0xFFFF0000
VMEM_LIMIT = 56 * 1024 * 1024


def _cparams(sem, vmem=VMEM_LIMIT):
    return pltpu.CompilerParams(dimension_semantics=sem, vmem_limit_bytes=vmem)


def _ln(x, g, b):
    mu = jnp.mean(x, axis=-1, keepdims=True)
    xc = x - mu
    var = jnp.mean(xc * xc, axis=-1, keepdims=True)
    return xc * lax.rsqrt(var + LN_EPS) * g + b


def _pack_bf16_pairs(h):
    bits = lax.bitcast_convert_type(h.astype(BF16).astype(F32), U32)
    n = h.shape[1] // 2
    return (bits[:, :n] >> 16) | (bits[:, n:] & jnp.uint32(HI_MASK))


def _store_token_tiles(ref, packed):
    m, n = packed.shape
    per = n // LANES
    for s in range(per):
        ref[pl.ds(s, m, stride=per), :] = packed[:, s * LANES:(s + 1) * LANES]


def _load_token_tiles(ref, m, per):
    slabs = [ref[pl.ds(s, m, stride=per), :] for s in range(per)]
    lo = [lax.bitcast_convert_type(p << 16, F32) for p in slabs]
    hi = [lax.bitcast_convert_type(p & jnp.uint32(HI_MASK), F32) for p in slabs]
    return jnp.concatenate(lo + hi, axis=-1).astype(BF16)


def _ada_kernel(c_ref, w_ref, b_ref, o_ref):
    cs = jax.nn.silu(c_ref[...])
    o_ref[...] = jnp.dot(cs.astype(BF16), w_ref[...].astype(BF16), preferred_element_type=F32) + b_ref[...]


def _ada(c8, w_ada, b_ada):
    d, n = w_ada.shape
    tn = min(n, 512)
    return pl.pallas_call(
        _ada_kernel,
        grid=(n // tn,),
        in_specs=[pl.BlockSpec((8, d), lambda j: (0, 0)),
                  pl.BlockSpec((d, tn), lambda j: (0, j)),
                  pl.BlockSpec((1, tn), lambda j: (0, j))],
        out_specs=pl.BlockSpec((8, tn), lambda j: (0, j)),
        out_shape=jax.ShapeDtypeStruct((8, n), F32),
        compiler_params=_cparams(("arbitrary",)),
        name="ada",
    )(c8, w_ada, b_ada)


def _inproj_kernel(coef_ref, x_ref, pos_ref, mod_ref, g_ref, b_ref, w_ref, fa_ref, fr_ref, z_ref,
                   h_ref, ca_ref, sa1_ref, sa2_ref, cr_ref, sr_ref, ka_ref, kb_ref, kc1_ref, kc2_ref,
                   res0_ref, res1_ref, *, n_sub):
    j = pl.program_id(1)

    @pl.when(j == 0)
    def _():
        xn = _ln(x_ref[...], g_ref[...], b_ref[...])
        h = xn * (1.0 + mod_ref[0, 1:2, :]) + mod_ref[0, 0:1, :]
        h_ref[...] = h.astype(BF16)
        pos = pos_ref[...]
        ang = pos * fa_ref[...]
        c, s = jnp.cos(ang), jnp.sin(ang)
        lane = lax.broadcasted_iota(I32, ang.shape, 1)
        half = ROPE_DIM // 2
        ca_ref[...] = c
        sa1_ref[...] = jnp.where(lane < half, -s, 0.0)
        sa2_ref[...] = jnp.where((lane >= half) & (lane < 2 * half), s, 0.0)
        angr = pos * fr_ref[...]
        cr_ref[...] = jnp.cos(angr)
        sr_ref[...] = jnp.sin(angr)

    a0, a1, a2, bb, cc = (coef_ref[j, n] for n in range(5))
    half = ROPE_DIM // 2
    tm = h_ref.shape[0]
    chunks = [pl.ds(r0, EPI_ROWS) for r0 in range(0, tm, EPI_ROWS)]
    for rows in chunks:
        ka_ref[rows, :] = a0 + a1 * ca_ref[rows, :] + a2 * cr_ref[rows, :]
        kb_ref[rows, :] = bb * sr_ref[rows, :]
        kc1_ref[rows, :] = cc * sa1_ref[rows, :]
        kc2_ref[rows, :] = cc * sa2_ref[rows, :]
    for hc in range(n_sub):
        res_ref = res0_ref if hc % 2 == 0 else res1_ref
        res_ref[...] = jnp.dot(h_ref[...], w_ref[:, hc * 2 * LANES:(hc + 1) * 2 * LANES], preferred_element_type=F32)
        for rows in chunks:
            lo, hi = res_ref[rows, :LANES], res_ref[rows, LANES:]
            ka, kb, kc1, kc2 = ka_ref[rows, :], kb_ref[rows, :], kc1_ref[rows, :], kc2_ref[rows, :]
            out_lo = lo * ka - hi * kb + pltpu.roll(lo, LANES - half, 1) * kc1 + pltpu.roll(lo, half, 1) * kc2
            out_hi = hi * ka + lo * kb + pltpu.roll(hi, LANES - half, 1) * kc1 + pltpu.roll(hi, half, 1) * kc2
            z_ref[hc, rows, :LANES] = out_lo.astype(BF16)
            z_ref[hc, rows, LANES:] = out_hi.astype(BF16)


def _inproj(x2, posb, mod3, ln0_g, ln0_b, w_in_b, fa, fr, seq, tm):
    t, d = x2.shape
    tn = d // 2
    n_sub = tn // (2 * LANES)
    per_b = seq // tm
    sa, sr = ATTN_HEAD_DIM ** -0.5, RET_HEAD_DIM ** -0.5
    coef = [[0.0, sa, 0.0, 0.0, sa],
            [0.0, 1.0, 0.0, 0.0, 1.0],
            [1.0, 0.0, 0.0, 0.0, 0.0],
            [0.0, 0.0, 1.0, 1.0, 0.0],
            [0.0, 0.0, sr, sr, 0.0],
            [1.0, 0.0, 0.0, 0.0, 0.0],
            [1.0, 0.0, 0.0, 0.0, 0.0]]
    coef = jnp.asarray([row + [0.0] * 3 for row in coef], F32)
    grid_spec = pltpu.PrefetchScalarGridSpec(
        num_scalar_prefetch=1,
        grid=(t // tm, N_SEG),
        in_specs=[pl.BlockSpec((tm, d), lambda i, j, c: (i, 0)),
                  pl.BlockSpec((tm, LANES), lambda i, j, c: (i, 0)),
                  pl.BlockSpec((1, 6, d), lambda i, j, c: (i // per_b, 0, 0)),
                  pl.BlockSpec((1, d), lambda i, j, c: (0, 0)),
                  pl.BlockSpec((1, d), lambda i, j, c: (0, 0)),
                  pl.BlockSpec((d, tn), lambda i, j, c: (0, j)),
                  pl.BlockSpec((1, LANES), lambda i, j, c: (0, 0)),
                  pl.BlockSpec((1, LANES), lambda i, j, c: (0, 0))],
        out_specs=pl.BlockSpec((n_sub, tm, 2 * LANES), lambda i, j, c: (j, i, 0)),
        scratch_shapes=[pltpu.VMEM((tm, d), BF16)] + [pltpu.VMEM((tm, LANES), F32)] * 9
        + [pltpu.VMEM((tm, 2 * LANES), F32)] * 2)
    return pl.pallas_call(
        functools.partial(_inproj_kernel, n_sub=n_sub),
        grid_spec=grid_spec,
        out_shape=jax.ShapeDtypeStruct((N_SEG * n_sub, t, 2 * LANES), BF16),
        compiler_params=_cparams(("arbitrary", "arbitrary")),
        name="inproj",
    )(coef, x2, posb, mod3, ln0_g, ln0_b, w_in_b, fa, fr)


def _attn_tile(q, k, v, bias):
    s = lax.dot_general(q, k, (((1,), (1,)), ((), ())), preferred_element_type=F32) + bias
    m = jnp.max(jnp.maximum(s[:, :LANES], s[:, LANES:]), axis=-1, keepdims=True)
    p = jnp.exp(s - m)
    den = jnp.sum(p[:, :LANES] + p[:, LANES:], axis=-1, keepdims=True)
    acc = jnp.dot(p.astype(BF16), v, preferred_element_type=F32)
    return acc, jnp.broadcast_to(m, acc.shape), jnp.broadcast_to(den, acc.shape)


def _attn_kernel(q_ref, k_ref, v_ref, bias_ref, p4_ref, p16_ref, o_ref,
                 acc_ref, m_ref, d_ref, qp_ref, kp_ref, vp_ref, *, seq):
    n_items = seq // QT

    def window(n, length):
        qs = n * QT
        ks = jnp.clip(qs - BAND, 0, length - KW)
        return qs, ks, bias_ref[lax.div(qs - ks, BAND)]

    def perm_body(it, carry):
        for u in range(PERM_UNROLL):
            blk = it * PERM_UNROLL + u
            r0 = pl.multiple_of(blk * PERM_ROWS, PERM_ROWS)
            qk = jnp.concatenate([q_ref[0, 0, pl.ds(r0, PERM_ROWS), :], k_ref[0, 0, pl.ds(r0, PERM_ROWS), :]], axis=-1)
            v = v_ref[0, 0, pl.ds(r0, PERM_ROWS), :]
            for slot, (dil, p_ref) in enumerate(((4, p4_ref), (16, p16_ref))):
                length, w = seq // dil, PERM_ROWS // dil
                qkp = jnp.dot(p_ref[...], qk, preferred_element_type=F32).astype(BF16)
                vp = jnp.dot(p_ref[...], v, preferred_element_type=F32).astype(BF16)
                for r in range(dil):
                    rows = pl.ds(pl.multiple_of(r * length + blk * w, w), w)
                    qp_ref[slot, rows, :] = qkp[r * w:(r + 1) * w, :ATTN_HEAD_DIM]
                    kp_ref[slot, rows, :] = qkp[r * w:(r + 1) * w, ATTN_HEAD_DIM:]
                    vp_ref[slot, rows, :] = vp[r * w:(r + 1) * w, :]
        return carry

    lax.fori_loop(0, seq // PERM_ROWS // PERM_UNROLL, perm_body, 0)

    def dilated_tile(item, dil, slot):
        length = seq // dil
        n, r = lax.div(item, dil), lax.rem(item, dil)
        qs, ks, bias = window(n, length)
        qrow = pl.multiple_of(r * length + qs, QT)
        krow = pl.multiple_of(r * length + ks, BAND)
        out = _attn_tile(qp_ref[slot, pl.ds(qrow, QT), :], kp_ref[slot, pl.ds(krow, KW), :],
                         vp_ref[slot, pl.ds(krow, KW), :], bias)
        return out, pl.ds(qs * dil + r, QT, stride=dil)

    def body16(it, carry):
        for g in range(CHAINS):
            (acc, m, den), rows = dilated_tile(it * CHAINS + g, 16, 1)
            acc_ref[rows, :] = acc
            m_ref[rows, :] = m
            d_ref[rows, :] = den
        return carry

    lax.fori_loop(0, n_items // CHAINS, body16, 0)

    def body4(it, carry):
        for g in range(CHAINS):
            (acc, m, den), rows = dilated_tile(it * CHAINS + g, 4, 0)
            m0 = m_ref[rows, :]
            mn = jnp.maximum(m0, m)
            a, b = jnp.exp(m0 - mn), jnp.exp(m - mn)
            acc_ref[rows, :] = a * acc_ref[rows, :] + b * acc
            d_ref[rows, :] = a * d_ref[rows, :] + b * den
            m_ref[rows, :] = mn
        return carry

    lax.fori_loop(0, n_items // CHAINS, body4, 0)

    def body1(it, carry):
        for g in range(CHAINS):
            qs, ks, bias = window(it * CHAINS + g, seq)
            qs, ks = pl.multiple_of(qs, QT), pl.multiple_of(ks, BAND)
            acc, m, den = _attn_tile(q_ref[0, 0, pl.ds(qs, QT), :], k_ref[0, 0, pl.ds(ks, KW), :],
                                     v_ref[0, 0, pl.ds(ks, KW), :], bias)
            rows = pl.ds(qs, QT)
            m0 = m_ref[rows, :]
            mn = jnp.maximum(m0, m)
            a, b = jnp.exp(m0 - mn), jnp.exp(m - mn)
            num = a * acc_ref[rows, :] + b * acc
            o_ref[0, 0, rows, :] = (num / (a * d_ref[rows, :] + b * den)).astype(BF16)
        return carry

    lax.fori_loop(0, n_items // CHAINS, body1, 0)


def _perm_matrix(dil):
    w = PERM_ROWS // dil
    i = jnp.arange(PERM_ROWS)
    src = (i % w) * dil + i // w
    return (src[:, None] == jnp.arange(PERM_ROWS)[None, :]).astype(BF16)


def _attn(z, batch, seq, n_ah):
    n_sub = n_ah // 2
    zv = z.reshape(z.shape[0], batch, seq, 2 * LANES)
    blk = (1, 1, seq, ATTN_HEAD_DIM)
    i = jnp.arange(QT)[:, None]
    jj = jnp.arange(KW)[None, :]
    bias = jnp.stack([jnp.where(jnp.abs(i + o * BAND - jj) <= BAND, 0.0, MASK_VALUE) for o in range(3)]).astype(F32)
    const = lambda shape: pl.BlockSpec(shape, lambda b, h: (0,) * len(shape))
    o = pl.pallas_call(
        functools.partial(_attn_kernel, seq=seq),
        grid=(batch, n_ah),
        in_specs=[pl.BlockSpec(blk, lambda b, h: (h // 2, b, 0, h % 2)),
                  pl.BlockSpec(blk, lambda b, h: (n_sub + h // 2, b, 0, h % 2)),
                  pl.BlockSpec(blk, lambda b, h: (2 * n_sub + h // 2, b, 0, h % 2)),
                  const((3, QT, KW)), const((PERM_ROWS, PERM_ROWS)), const((PERM_ROWS, PERM_ROWS))],
        out_specs=pl.BlockSpec(blk, lambda b, h: (h, b, 0, 0)),
        out_shape=jax.ShapeDtypeStruct((n_ah, batch, seq, ATTN_HEAD_DIM), BF16),
        scratch_shapes=[pltpu.VMEM((seq, LANES), F32)] * 3
        + [pltpu.VMEM((2, seq, ATTN_HEAD_DIM), BF16)] * 3,
        compiler_params=_cparams(("arbitrary", "arbitrary")),
        name="attn",
    )(zv, zv, zv, bias, _perm_matrix(4), _perm_matrix(16))
    return o.reshape(n_ah, batch * seq, ATTN_HEAD_DIM)


def _ret_kernel(lg_ref, q_ref, k_ref, v_ref, g_ref, o_ref, y_ref, sf_ref, sb_ref, dmat_ref, *, n_tiles, tile):
    h = pl.program_id(1)
    t = pl.program_id(2)
    c = RET_CHUNK
    lgf = lg_ref[0, h]
    lgb = lg_ref[1, h]
    col = lax.broadcasted_iota(I32, (c, 1), 0).astype(F32)
    row = lax.broadcasted_iota(I32, (1, c), 1).astype(F32)
    n_iter = tile // c // RET_UNROLL

    @pl.when(t == 0)
    def _():
        sf_ref[...] = jnp.zeros_like(sf_ref)
        sb_ref[...] = jnp.zeros_like(sb_ref)
        diff = (lax.broadcasted_iota(I32, (c, c), 0) - lax.broadcasted_iota(I32, (c, c), 1)).astype(F32)
        dmat_ref[...] = jnp.where(diff >= 0, jnp.exp(lgf * jnp.maximum(diff, 0.0)), 0.0) \
            + jnp.where(diff < 0, jnp.exp(lgb * jnp.maximum(-diff, 0.0)), 0.0)

    def chunk(r0):
        rows = pl.ds(pl.multiple_of(r0, c), c)
        return rows, q_ref[0, rows, :], k_ref[0, rows, :], v_ref[0, rows, :]

    @pl.when(t < n_tiles)
    def _():
        xi = jnp.exp(lgf * (col + 1.0))
        zeta = jnp.exp(lgf * (c - 1.0 - row))
        cdec = jnp.exp(lgf * jnp.full((1, 1), float(c), F32))
        base = t * tile

        def body(it, carry):
            state = sf_ref[...]
            for u in range(RET_UNROLL):
                r0 = (it * RET_UNROLL + u) * c
                _, q, k, v = chunk(r0)
                s = lax.dot_general(q, k, (((1,), (1,)), ((), ())), preferred_element_type=F32) * dmat_ref[...]
                y = jnp.dot(s.astype(BF16), v, preferred_element_type=F32)
                y = y + jnp.dot(q, state.astype(BF16), preferred_element_type=F32) * xi
                kt = (k.astype(F32).T * zeta).astype(BF16)
                state = state * cdec + jnp.dot(kt, v, preferred_element_type=F32)
                y_ref[pl.ds(pl.multiple_of(base + r0, c), c), :] = y
            sf_ref[...] = state
            return carry

        lax.fori_loop(0, n_iter, body, 0)

    @pl.when(t >= n_tiles)
    def _():
        xi = jnp.exp(lgb * (c - col))
        zeta = jnp.exp(lgb * row)
        cdec = jnp.exp(lgb * jnp.full((1, 1), float(c), F32))
        base = (2 * n_tiles - 1 - t) * tile

        def body(it, carry):
            state = sb_ref[...]
            for u in range(RET_UNROLL):
                r0 = (tile // c - 1 - (it * RET_UNROLL + u)) * c
                rows, q, k, v = chunk(r0)
                y = y_ref[pl.ds(pl.multiple_of(base + r0, c), c), :]
                y = y + jnp.dot(q, state.astype(BF16), preferred_element_type=F32) * xi
                kt = (k.astype(F32).T * zeta).astype(BF16)
                state = state * cdec + jnp.dot(kt, v, preferred_element_type=F32)
                mu = jnp.mean(y, axis=-1, keepdims=True)
                yc = y - mu
                var = jnp.mean(yc * yc, axis=-1, keepdims=True)
                yn = yc * lax.rsqrt(var + LN_EPS)
                gate = jax.nn.silu(g_ref[0, rows, :].astype(F32))
                o_ref[0, rows, :] = (gate * yn).astype(BF16)
            sb_ref[...] = state
            return carry

        lax.fori_loop(0, n_iter, body, 0)


def _ret(z, lg, batch, seq, n_rh, seg0, tile):
    n_tiles = seq // tile
    per_b = seq // tile

    def rows(t):
        return jnp.where(t < n_tiles, t, 2 * n_tiles - 1 - t)

    def spec(seg):
        return pl.BlockSpec((1, tile, RET_HEAD_DIM), lambda b, h, t: ((seg0 + seg) * n_rh + h, b * per_b + rows(t), 0))

    return pl.pallas_call(
        functools.partial(_ret_kernel, n_tiles=n_tiles, tile=tile),
        grid=(batch, n_rh, 2 * n_tiles),
        in_specs=[pl.BlockSpec(memory_space=pltpu.SMEM), spec(0), spec(1), spec(2), spec(3)],
        out_specs=pl.BlockSpec((1, tile, RET_HEAD_DIM),
                               lambda b, h, t: (h, b * per_b + jnp.where(t < n_tiles, n_tiles - 1, 2 * n_tiles - 1 - t), 0)),
        out_shape=jax.ShapeDtypeStruct((n_rh, batch * seq, RET_HEAD_DIM), BF16),
        scratch_shapes=[pltpu.VMEM((seq, RET_HEAD_DIM), F32),
                        pltpu.VMEM((RET_HEAD_DIM, RET_HEAD_DIM), F32),
                        pltpu.VMEM((RET_HEAD_DIM, RET_HEAD_DIM), F32),
                        pltpu.VMEM((RET_CHUNK, RET_CHUNK), F32)],
        compiler_params=_cparams(("arbitrary", "arbitrary", "arbitrary")),
        name="ret",
    )(lg, z, z, z, z)


def _outproj_kernel(a_ref, r_ref, x_ref, mod_ref, g0_ref, b0_ref, g1_ref, b1_ref, w_ref, x1_ref, hp_ref,
                    *, n_ah, n_rh, alpha):
    mix = jnp.concatenate([a_ref[h] for h in range(n_ah)] + [r_ref[h] for h in range(n_rh)], axis=-1)
    acc = jnp.dot(mix, w_ref[...], preferred_element_type=F32)
    xn = _ln(x_ref[...], g0_ref[...], b0_ref[...])
    y = alpha * xn + (1.0 + mod_ref[0, 2:3, :]) * acc
    x1 = _ln(y, g1_ref[...], b1_ref[...])
    x1_ref[...] = x1
    _store_token_tiles(hp_ref, _pack_bf16_pairs(x1 * (1.0 + mod_ref[0, 4:5, :]) + mod_ref[0, 3:4, :]))


def _outproj(attn, r, x2, mod3, g0, b0, g1, b1, w_out_b, seq, tm, alpha):
    t, d = x2.shape
    n_ah, n_rh = attn.shape[0], r.shape[0]
    per_b = seq // tm
    per = d // 2 // LANES
    row = lambda i: (i, 0)
    vec = pl.BlockSpec((1, d), lambda i: (0, 0))
    return pl.pallas_call(
        functools.partial(_outproj_kernel, n_ah=n_ah, n_rh=n_rh, alpha=alpha),
        grid=(t // tm,),
        in_specs=[pl.BlockSpec((n_ah, tm, ATTN_HEAD_DIM), lambda i: (0, i, 0)),
                  pl.BlockSpec((n_rh, tm, RET_HEAD_DIM), lambda i: (0, i, 0)),
                  pl.BlockSpec((tm, d), row),
                  pl.BlockSpec((1, 6, d), lambda i: (i // per_b, 0, 0)),
                  vec, vec, vec, vec,
                  pl.BlockSpec((d, d), lambda i: (0, 0))],
        out_specs=[pl.BlockSpec((tm, d), row), pl.BlockSpec((tm * per, LANES), row)],
        out_shape=[jax.ShapeDtypeStruct((t, d), F32), jax.ShapeDtypeStruct((t * per, LANES), U32)],
        compiler_params=_cparams(("arbitrary",)),
        name="outproj",
    )(attn, r, x2, mod3, g0, b0, g1, b1, w_out_b)


def _first_argmax(rows):
    best, idx = rows[0], jnp.zeros(rows[0].shape, I32)
    for e in range(1, len(rows)):
        better = rows[e] > best
        idx = jnp.where(better, e, idx)
        best = jnp.maximum(best, rows[e])
    return best, idx


def _router_kernel(h_ref, w_ref, b_ref, u_ref, eid_ref, wt_ref, rank_ref, cnt_ref, carry_ref, *, per):
    i = pl.program_id(0)

    @pl.when(i == 0)
    def _():
        carry_ref[...] = jnp.zeros_like(carry_ref)

    tm = eid_ref.shape[2]
    lt = lax.dot_general(w_ref[...], _load_token_tiles(h_ref, tm, per), (((1,), (1,)), ((), ())),
                         preferred_element_type=F32) + b_ref[...]
    grow = [lt[g:g + 1, :] for g in range(N_GROUPS)]
    gmax, gsel = _first_argmax(grow)
    gsum = grow[0] * 0.0
    for g in range(N_GROUPS):
        gsum = gsum + jnp.exp(grow[g] - gmax)
    pg = 1.0 / gsum
    srow = []
    for e in range(EXPERTS_PER_GROUP):
        r = lt[N_GROUPS + e:N_GROUPS + e + 1, :]
        for g in range(1, N_GROUPS):
            o = N_GROUPS + g * EXPERTS_PER_GROUP + e
            r = jnp.where(gsel == g, lt[o:o + 1, :], r)
        srow.append(r)
    v1, i1 = _first_argmax(srow)
    v2, i2 = _first_argmax([jnp.where(i1 == e, -jnp.inf, srow[e]) for e in range(EXPERTS_PER_GROUP)])
    e2 = jnp.exp(v2 - v1)
    den = 1.0 + e2
    wt_ref[0:1, :] = (1.0 / den) * pg
    wt_ref[1:2, :] = (e2 / den) * pg
    eid0 = gsel * EXPERTS_PER_GROUP + i1
    eid1 = gsel * EXPERTS_PER_GROUP + i2
    eid_ref[0, 0:1, :] = eid0
    eid_ref[0, 1:2, :] = eid1
    erow = lax.broadcasted_iota(I32, (N_EXPERTS, tm), 0)
    oh0 = (erow == eid0).astype(F32)
    oh1 = (erow == eid1).astype(F32)
    oh = oh0 + oh1
    incl = jnp.dot(oh.astype(BF16), u_ref[...], preferred_element_type=F32)
    before = carry_ref[:, 0:1] + incl - oh
    rank_ref[0, 0:1, :] = jnp.sum(oh0 * before, axis=0, keepdims=True).astype(I32)
    rank_ref[0, 1:2, :] = jnp.sum(oh1 * before, axis=0, keepdims=True).astype(I32)
    carry = carry_ref[...] + jnp.sum(oh, axis=1, keepdims=True)
    carry_ref[...] = carry
    cnt_ref[...] = carry


def _router(hp, wr, br, tm):
    d = wr.shape[1]
    per = d // 2 // LANES
    t = hp.shape[0] // per
    tri = (lax.broadcasted_iota(I32, (tm, tm), 0) <= lax.broadcasted_iota(I32, (tm, tm), 1)).astype(BF16)
    tile3 = pl.BlockSpec((1, 2, tm), lambda i: (i, 0, 0))
    return pl.pallas_call(
        functools.partial(_router_kernel, per=per),
        grid=(t // tm,),
        in_specs=[pl.BlockSpec((tm * per, LANES), lambda i: (i, 0)),
                  pl.BlockSpec((ROUTER_ROWS, d), lambda i: (0, 0)),
                  pl.BlockSpec((ROUTER_ROWS, 1), lambda i: (0, 0)),
                  pl.BlockSpec((tm, tm), lambda i: (0, 0))],
        out_specs=[tile3, pl.BlockSpec((2, tm), lambda i: (0, i)), tile3,
                   pl.BlockSpec((N_EXPERTS, LANES), lambda i: (0, 0))],
        out_shape=[jax.ShapeDtypeStruct((t // tm, 2, tm), I32), jax.ShapeDtypeStruct((2, t), F32),
                   jax.ShapeDtypeStruct((t // tm, 2, tm), I32), jax.ShapeDtypeStruct((N_EXPERTS, LANES), F32)],
        scratch_shapes=[pltpu.VMEM((N_EXPERTS, LANES), F32)],
        compiler_params=_cparams(("arbitrary",)),
        name="router",
    )(hp, wr, br, tri)


def _tile_copy(src, s, dst, d, per, sem):
    return pltpu.make_async_copy(src.at[pl.ds(pl.multiple_of(s * per, per), per), :],
                                 dst.at[pl.ds(pl.multiple_of(d * per, per), per), :], sem)


def _dispatch_kernel(ps_ref, lo_ref, hi_ref, eid_ref, rank_ref, h_ref, xb_hbm, stage, zero, sem, *, per):
    i = pl.program_id(0)
    n = pl.num_programs(0)
    slot = i % 2
    tm = eid_ref.shape[2]

    def wait_tile(s):
        pltpu.make_async_copy(xb_hbm.at[pl.ds(0, 2 * tm * per), :], xb_hbm.at[pl.ds(0, 2 * tm * per), :], sem.at[s]).wait()

    @pl.when(i >= 2)
    def _():
        wait_tile(slot)

    stage[slot] = h_ref[...]

    def body(r, carry):
        for k in range(2):
            dst = ps_ref[eid_ref[0, k, r]] + rank_ref[0, k, r]
            _tile_copy(stage.at[slot], r, xb_hbm, dst, per, sem.at[slot]).start(priority=k)
        return carry

    lax.fori_loop(0, tm, body, 0, unroll=8)

    @pl.when(i == n - 1)
    def _():
        @pl.when(n >= 2)
        def _():
            wait_tile(1 - slot)
        wait_tile(slot)
        zero[...] = jnp.zeros_like(zero)

        def fill(e, carry):
            def one(s, c):
                _tile_copy(zero, 0, xb_hbm, s, per, sem.at[2]).start()
                return c
            lax.fori_loop(lo_ref[e], hi_ref[e], one, 0)

            def one_wait(s, c):
                _tile_copy(zero, 0, xb_hbm, s, per, sem.at[2]).wait()
                return c
            lax.fori_loop(lo_ref[e], hi_ref[e], one_wait, 0)
            return carry

        lax.fori_loop(0, N_EXPERTS + 1, fill, 0)


def _dispatch(pstart, fill_lo, fill_hi, eid3, rank3, hp, n_slots, per):
    n, _, tm = eid3.shape
    smem3 = pl.BlockSpec((1, 2, tm), lambda i, *_: (i, 0, 0), memory_space=pltpu.SMEM)
    grid_spec = pltpu.PrefetchScalarGridSpec(
        num_scalar_prefetch=3,
        grid=(n,),
        in_specs=[smem3, smem3, pl.BlockSpec((tm * per, LANES), lambda i, *_: (i, 0))],
        out_specs=pl.BlockSpec(memory_space=pl.ANY),
        scratch_shapes=[pltpu.VMEM((2, tm * per, LANES), U32), pltpu.VMEM((per, LANES), U32),
                        pltpu.SemaphoreType.DMA((3,))])
    return pl.pallas_call(
        functools.partial(_dispatch_kernel, per=per),
        grid_spec=grid_spec,
        out_shape=jax.ShapeDtypeStruct((n_slots * per, LANES), U32),
        compiler_params=_cparams(("arbitrary",)),
        name="dispatch",
    )(pstart, fill_lo, fill_hi, eid3, rank3, hp)


def _expert_kernel(be_ref, first_ref, par_ref, nxt_ref, has_ref, x_ref, w1_hbm, w3_hbm, w2_hbm, y_ref,
                   wf1, wf3, wf2, w1b, w3b, w2b, sem, *, per):
    i = pl.program_id(0)

    def weight_copies(e, s):
        return [pltpu.make_async_copy(src.at[e], dst.at[s], sem.at[s])
                for src, dst in ((w1_hbm, wf1), (w3_hbm, wf3), (w2_hbm, wf2))]

    @pl.when(i == 0)
    def _():
        for cp in weight_copies(be_ref[0], 0):
            cp.start()

    @pl.when(first_ref[i] == 1)
    def _():
        s = par_ref[i]
        for cp in weight_copies(be_ref[i], s):
            cp.wait()

        @pl.when(has_ref[i] == 1)
        def _():
            for cp in weight_copies(nxt_ref[i], 1 - s):
                cp.start()

        w1b[...] = wf1[s].astype(BF16)
        w3b[...] = wf3[s].astype(BF16)
        w2b[...] = wf2[s].astype(BF16)

    x = _load_token_tiles(x_ref, MOE_BLOCK, per)
    a = jnp.dot(x, w1b[...], preferred_element_type=F32)
    b = jnp.dot(x, w3b[...], preferred_element_type=F32)
    mid = (jax.nn.silu(a) * b).astype(BF16)
    _store_token_tiles(y_ref, _pack_bf16_pairs(jnp.dot(mid, w2b[...], preferred_element_type=F32)))


def _experts(blk_e, xb, w1, w3, w2):
    d, ff = w1.shape[1], w1.shape[2]
    per = d // 2 // LANES
    nblk = xb.shape[0] // per // MOE_BLOCK
    rows = pl.BlockSpec((MOE_BLOCK * per, LANES), lambda i, *_: (i, 0))
    first = jnp.concatenate([jnp.ones((1,), I32), (blk_e[1:] != blk_e[:-1]).astype(I32)])
    parity = (jnp.cumsum(first) - 1) % 2
    nxt = jnp.min(jnp.where(blk_e[None, :] > blk_e[:, None], blk_e[None, :], N_EXPERTS), axis=1)
    has_next = (nxt < N_EXPERTS).astype(I32)
    nxt = jnp.minimum(nxt, N_EXPERTS - 1)
    grid_spec = pltpu.PrefetchScalarGridSpec(
        num_scalar_prefetch=5,
        grid=(nblk,),
        in_specs=[rows] + [pl.BlockSpec(memory_space=pl.ANY)] * 3,
        out_specs=rows,
        scratch_shapes=[pltpu.VMEM((2, d, ff), F32), pltpu.VMEM((2, d, ff), F32), pltpu.VMEM((2, ff, d), F32),
                        pltpu.VMEM((d, ff), BF16), pltpu.VMEM((d, ff), BF16), pltpu.VMEM((ff, d), BF16),
                        pltpu.SemaphoreType.DMA((2,))])
    return pl.pallas_call(
        functools.partial(_expert_kernel, per=per),
        grid_spec=grid_spec,
        out_shape=jax.ShapeDtypeStruct(xb.shape, U32),
        compiler_params=_cparams(("arbitrary",)),
        name="experts",
    )(blk_e, first, parity.astype(I32), nxt.astype(I32), has_next, xb, w1, w3, w2)


def _combine_kernel(ps_ref, eid_ref, rank_ref, eidn_ref, rankn_ref, y_hbm, wt_ref, x1_ref, mod_ref, g_ref, b_ref,
                    o_ref, ybuf, sem, *, alpha, per):
    i = pl.program_id(0)
    n = pl.num_programs(0)
    slot = i % 2
    tm = x1_ref.shape[0]

    def start(e_ref, r_ref, s):
        def body(r, carry):
            for k in range(2):
                src = ps_ref[e_ref[0, k, r]] + r_ref[0, k, r]
                _tile_copy(y_hbm, src, ybuf.at[s, k], r, per, sem.at[s]).start(priority=k)
            return carry
        lax.fori_loop(0, tm, body, 0, unroll=8)

    @pl.when(i == 0)
    def _():
        start(eid_ref, rank_ref, 0)

    @pl.when(i + 1 < n)
    def _():
        start(eidn_ref, rankn_ref, 1 - slot)

    for k in range(2):
        pltpu.make_async_copy(y_hbm.at[pl.ds(0, tm * per), :], ybuf.at[slot, k], sem.at[slot]).wait()
    wt = wt_ref[...]
    y0 = _load_token_tiles(ybuf.at[slot, 0], tm, per).astype(F32)
    y1 = _load_token_tiles(ybuf.at[slot, 1], tm, per).astype(F32)
    ffn = wt[:, 0:1] * y0 + wt[:, 1:2] * y1
    y = alpha * x1_ref[...] + (1.0 + mod_ref[0, 5:6, :]) * ffn
    o_ref[...] = _ln(y, g_ref[...], b_ref[...])


def _combine(pstart, eid3, rank3, yb, wt_t, x1, mod3, g2, b2, seq, alpha):
    t, d = x1.shape
    per = d // 2 // LANES
    n, _, tm = eid3.shape
    per_b = seq // tm
    cur = pl.BlockSpec((1, 2, tm), lambda i, ps: (i, 0, 0), memory_space=pltpu.SMEM)
    nxt = pl.BlockSpec((1, 2, tm), lambda i, ps: (jnp.minimum(i + 1, n - 1), 0, 0), memory_space=pltpu.SMEM)
    vec = pl.BlockSpec((1, d), lambda i, ps: (0, 0))
    grid_spec = pltpu.PrefetchScalarGridSpec(
        num_scalar_prefetch=1,
        grid=(n,),
        in_specs=[cur, cur, nxt, nxt,
                  pl.BlockSpec(memory_space=pl.ANY),
                  pl.BlockSpec((tm, 2), lambda i, ps: (i, 0)),
                  pl.BlockSpec((tm, d), lambda i, ps: (i, 0)),
                  pl.BlockSpec((1, 6, d), lambda i, ps: (i // per_b, 0, 0)),
                  vec, vec],
        out_specs=pl.BlockSpec((tm, d), lambda i, ps: (i, 0)),
        scratch_shapes=[pltpu.VMEM((2, 2, tm * per, LANES), U32), pltpu.SemaphoreType.DMA((2,))])
    return pl.pallas_call(
        functools.partial(_combine_kernel, alpha=alpha, per=per),
        grid_spec=grid_spec,
        out_shape=jax.ShapeDtypeStruct((t, d), F32),
        compiler_params=_cparams(("arbitrary",)),
        name="combine",
    )(pstart, eid3, rank3, eid3, rank3, yb, wt_t, x1, mod3, g2, b2)


def _pick_tile(n, want):
    tm = min(n, want)
    assert n % tm == 0
    return tm


def kernel(x, c, positions, ln0_g, ln0_b, w_ada, b_ada, w_in, w_out, ret_log_decay_f, ret_log_decay_b,
           ln1_g, ln1_b, w_group, b_group, w_sub, b_sub, w1, w3, w2, ln2_g, ln2_b):
    batch, seq, d = x.shape
    depth = w_ada.shape[0]
    t = batch * seq
    n_ah = d // 2 // ATTN_HEAD_DIM
    n_rh = d // 2 // RET_HEAD_DIM
    assert depth == 1 and d % (2 * RET_HEAD_DIM) == 0 and seq % (max(DILATIONS) * KW) == 0 and batch <= 8
    assert seq % (QT * CHAINS) == 0
    alpha = (2 * depth) ** 0.25

    inv_rope = ROPE_THETA ** (-jnp.arange(0, ROPE_DIM, 2, dtype=F32) / ROPE_DIM)
    inv_ret = RET_THETA ** (-jnp.linspace(0.0, 1.0, RET_HEAD_DIM // 2, dtype=F32))
    fa = jnp.zeros((1, LANES), F32).at[0, :ROPE_DIM].set(jnp.concatenate([inv_rope, inv_rope]))
    fr = inv_ret.reshape(1, LANES)
    posb = jnp.broadcast_to(positions.astype(F32).reshape(t, 1), (t, LANES))
    c8 = jnp.zeros((8, d), F32).at[:batch].set(c)
    row = lambda v: v.reshape(1, d)

    xs = x.reshape(t, d)
    mod = _ada(c8, w_ada[0], b_ada[0].reshape(1, -1))
    mod3 = mod[:batch].reshape(batch, 6, d)
    z = _inproj(xs, posb, mod3, row(ln0_g), row(ln0_b), w_in[0].astype(BF16), fa, fr, seq, _pick_tile(seq, 512))
    attn = _attn(z, batch, seq, n_ah)
    lg = jnp.stack([ret_log_decay_f[0], ret_log_decay_b[0]]).astype(F32)
    r = _ret(z, lg, batch, seq, n_rh, 3, _pick_tile(seq, 1024))
    x1, hp = _outproj(attn, r, xs, mod3, row(ln0_g), row(ln0_b), row(ln1_g[0]), row(ln1_b[0]),
                      w_out[0].astype(BF16), seq, _pick_tile(seq, 256), alpha)
    wr = jnp.zeros((ROUTER_ROWS, d), F32)
    wr = wr.at[:N_GROUPS].set(w_group[0].T)
    wr = wr.at[N_GROUPS:N_GROUPS + N_EXPERTS].set(w_sub[0].transpose(0, 2, 1).reshape(N_EXPERTS, d))
    br = jnp.zeros((ROUTER_ROWS, 1), F32)
    br = br.at[:N_GROUPS, 0].set(b_group[0]).at[N_GROUPS:N_GROUPS + N_EXPERTS, 0].set(b_sub[0].reshape(-1))
    eid3, wt, rank3, cnt = _router(hp, wr.astype(BF16), br, _pick_tile(seq, 512))
    counts = cnt[:, 0].astype(I32)
    padded = (counts + MOE_BLOCK - 1) // MOE_BLOCK * MOE_BLOCK
    pend = jnp.cumsum(padded)
    pstart = pend - padded
    n_slots = 2 * t + N_EXPERTS * MOE_BLOCK
    nblk = n_slots // MOE_BLOCK
    starts = jnp.arange(nblk, dtype=I32) * MOE_BLOCK
    blk_e = jnp.minimum(jnp.sum((pend[None, :] <= starts[:, None]).astype(I32), axis=1), N_EXPERTS - 1)
    fill_lo = jnp.concatenate([pstart + counts, pend[-1:]])
    fill_hi = jnp.concatenate([pend, jnp.full((1,), n_slots, I32)])
    xb = _dispatch(pstart, fill_lo, fill_hi, eid3, rank3, hp, n_slots, d // 2 // LANES)
    yb = _experts(blk_e, xb, w1[0], w3[0], w2[0])
    out = _combine(pstart, eid3, rank3, yb, wt.T, x1, mod3, row(ln2_g[0]), row(ln2_b[0]), seq, alpha)
    return out.reshape(batch, seq, d)
```

```python
import functools

import jax
import jax.numpy as jnp
from jax import lax
from jax.experimental import pallas as pl
from jax.experimental.pallas import tpu as pltpu

F32 = jnp.float32
BF16 = jnp.bfloat16
I32 = jnp.int32
U32 = jnp.uint32

LANES = 128
ATTN_HEAD_DIM = 128
RET_HEAD_DIM = 256
DILATIONS = (1, 4, 16)
BAND = 64
QT = 128
KW = QT + 2 * BAND
CHAINS = 8
PERM_ROWS = 256
PERM_UNROLL = 2
ROPE_THETA = 500000.0
ROPE_DIM = ATTN_HEAD_DIM // 4
RET_THETA = 10000.0
RET_CHUNK = 128
RET_UNROLL = 4
N_GROUPS = 4
EXPERTS_PER_GROUP = 8
N_EXPERTS = N_GROUPS * EXPERTS_PER_GROUP
MOE_BLOCK = 128
LN_EPS = 1e-5
MASK_VALUE = -1e30
N_SEG = 7
ROUTER_ROWS = 48
HI_MASK = 0xFFFF0000
VMEM_LIMIT = 56 * 1024 * 1024


def _cparams(sem, vmem=VMEM_LIMIT):
    return pltpu.CompilerParams(dimension_semantics=sem, vmem_limit_bytes=vmem)


def _ln(x, g, b):
    mu = jnp.mean(x, axis=-1, keepdims=True)
    xc = x - mu
    var = jnp.mean(xc * xc, axis=-1, keepdims=True)
    return xc * lax.rsqrt(var + LN_EPS) * g + b


def _pack_bf16_pairs(h):
    bits = lax.bitcast_convert_type(h.astype(BF16).astype(F32), U32)
    n = h.shape[1] // 2
    return (bits[:, :n] >> 16) | (bits[:, n:] & jnp.uint32(HI_MASK))


def _store_token_tiles(ref, packed):
    m, n = packed.shape
    per = n // LANES
    for s in range(per):
        ref[pl.ds(s, m, stride=per), :] = packed[:, s * LANES:(s + 1) * LANES]


def _load_token_tiles(ref, m, per):
    slabs = [ref[pl.ds(s, m, stride=per), :] for s in range(per)]
    lo = [lax.bitcast_convert_type(p << 16, F32) for p in slabs]
    hi = [lax.bitcast_convert_type(p & jnp.uint32(HI_MASK), F32) for p in slabs]
    return jnp.concatenate(lo + hi, axis=-1).astype(BF16)


def _ada_kernel(c_ref, w_ref, b_ref, o_ref):
    cs = jax.nn.silu(c_ref[...])
    o_ref[...] = jnp.dot(cs.astype(BF16), w_ref[...].astype(BF16), preferred_element_type=F32) + b_ref[...]


def _ada(c8, w_ada, b_ada):
    d, n = w_ada.shape
    tn = min(n, 512)
    return pl.pallas_call(
        _ada_kernel,
        grid=(n // tn,),
        in_specs=[pl.BlockSpec((8, d), lambda j: (0, 0)),
                  pl.BlockSpec((d, tn), lambda j: (0, j)),
                  pl.BlockSpec((1, tn), lambda j: (0, j))],
        out_specs=pl.BlockSpec((8, tn), lambda j: (0, j)),
        out_shape=jax.ShapeDtypeStruct((8, n), F32),
        compiler_params=_cparams(("arbitrary",)),
        name="ada",
    )(c8, w_ada, b_ada)


def _prep_kernel(x_ref, pos_ref, mod_ref, g_ref, b_ref, fa_ref, fr_ref, h_ref, tab_ref):
    xn = _ln(x_ref[...], g_ref[...], b_ref[...])
    h_ref[...] = (xn * (1.0 + mod_ref[0, 1:2, :]) + mod_ref[0, 0:1, :]).astype(BF16)
    pos = pos_ref[...]
    ang = pos * fa_ref[...]
    c, s = jnp.cos(ang), jnp.sin(ang)
    lane = lax.broadcasted_iota(I32, ang.shape, 1)
    half = ROPE_DIM // 2
    tab_ref[0] = c
    tab_ref[1] = jnp.where(lane < half, -s, 0.0)
    tab_ref[2] = jnp.where((lane >= half) & (lane < 2 * half), s, 0.0)
    angr = pos * fr_ref[...]
    tab_ref[3] = jnp.cos(angr)
    tab_ref[4] = jnp.sin(angr)


def _prep(x2, posb, mod3, ln0_g, ln0_b, fa, fr, seq, tm):
    t, d = x2.shape
    per_b = seq // tm
    vec = lambda n: pl.BlockSpec((1, n), lambda i: (0, 0))
    return pl.pallas_call(
        _prep_kernel,
        grid=(t // tm,),
        in_specs=[pl.BlockSpec((tm, d), lambda i: (i, 0)),
                  pl.BlockSpec((tm, LANES), lambda i: (i, 0)),
                  pl.BlockSpec((1, 6, d), lambda i: (i // per_b, 0, 0)),
                  vec(d), vec(d), vec(LANES), vec(LANES)],
        out_specs=[pl.BlockSpec((tm, d), lambda i: (i, 0)), pl.BlockSpec((5, tm, LANES), lambda i: (0, i, 0))],
        out_shape=[jax.ShapeDtypeStruct((t, d), BF16), jax.ShapeDtypeStruct((5, t, LANES), F32)],
        compiler_params=_cparams(("arbitrary",)),
        name="prep",
    )(x2, posb, mod3, ln0_g, ln0_b, fa, fr)


def _proj_chunks(h_ref, w_ref, n_sub):
    for hc in range(n_sub):
        yield hc, jnp.dot(h_ref[...], w_ref[:, hc * 2 * LANES:(hc + 1) * 2 * LANES], preferred_element_type=F32)


def _proj_attn_kernel(h_ref, w_ref, tab_ref, z_ref, *, n_sub):
    scale = jnp.where(pl.program_id(0) == 0, ATTN_HEAD_DIM ** -0.5, 1.0).astype(F32)
    ca, s1, s2 = tab_ref[0] * scale, tab_ref[1] * scale, tab_ref[2] * scale
    half = ROPE_DIM // 2
    for hc, res in _proj_chunks(h_ref, w_ref, n_sub):
        for lanes in (slice(0, LANES), slice(LANES, 2 * LANES)):
            x = res[:, lanes]
            r = x * ca + pltpu.roll(x, LANES - half, 1) * s1 + pltpu.roll(x, half, 1) * s2
            z_ref[hc, :, lanes] = r.astype(BF16)


def _proj_ret_kernel(h_ref, w_ref, tab_ref, z_ref, *, n_sub):
    scale = jnp.where(pl.program_id(0) == 1, RET_HEAD_DIM ** -0.5, 1.0).astype(F32)
    cr, sr = tab_ref[3] * scale, tab_ref[4] * scale
    for hc, res in _proj_chunks(h_ref, w_ref, n_sub):
        x1, x2 = res[:, :LANES], res[:, LANES:]
        z_ref[hc, :, :LANES] = (x1 * cr - x2 * sr).astype(BF16)
        z_ref[hc, :, LANES:] = (x2 * cr + x1 * sr).astype(BF16)


def _proj_plain_kernel(h_ref, w_ref, z_ref, *, n_sub):
    for hc, res in _proj_chunks(h_ref, w_ref, n_sub):
        z_ref[hc] = res.astype(BF16)


def _proj(body, col_of, n_seg, h, w_in_b, tabs, tm, name):
    t, d = h.shape
    tn = d // 2
    n_sub = tn // (2 * LANES)
    in_specs = [pl.BlockSpec((tm, d), lambda j, i: (i, 0)),
                pl.BlockSpec((d, tn), lambda j, i: (0, col_of(j)))]
    args = [h, w_in_b]
    if tabs is not None:
        in_specs.append(pl.BlockSpec((5, tm, LANES), lambda j, i: (0, i, 0)))
        args.append(tabs)
    return pl.pallas_call(
        functools.partial(body, n_sub=n_sub),
        grid=(n_seg, t // tm),
        in_specs=in_specs,
        out_specs=pl.BlockSpec((n_sub, tm, 2 * LANES), lambda j, i: (j, i, 0)),
        out_shape=jax.ShapeDtypeStruct((n_seg * n_sub, t, 2 * LANES), BF16),
        compiler_params=_cparams(("arbitrary", "arbitrary")),
        name=name,
    )(*args)


def _attn_tile(q, k, v, bias):
    s = lax.dot_general(q, k, (((1,), (1,)), ((), ())), preferred_element_type=F32) + bias
    m = jnp.max(jnp.maximum(s[:, :LANES], s[:, LANES:]), axis=-1, keepdims=True)
    p = jnp.exp(s - m)
    den = jnp.sum(p[:, :LANES] + p[:, LANES:], axis=-1, keepdims=True)
    acc = jnp.dot(p.astype(BF16), v, preferred_element_type=F32)
    return acc, jnp.broadcast_to(m, acc.shape), jnp.broadcast_to(den, acc.shape)


def _attn_kernel(q_ref, k_ref, v_ref, bias_ref, p4_ref, p16_ref, o_ref,
                 acc_ref, m_ref, d_ref, qp_ref, kp_ref, vp_ref, *, seq):
    n_items = seq // QT

    def window(n, length):
        qs = n * QT
        ks = jnp.clip(qs - BAND, 0, length - KW)
        return qs, ks, bias_ref[lax.div(qs - ks, BAND)]

    def perm_body(it, carry):
        for u in range(PERM_UNROLL):
            blk = it * PERM_UNROLL + u
            r0 = pl.multiple_of(blk * PERM_ROWS, PERM_ROWS)
            qk = jnp.concatenate([q_ref[0, 0, pl.ds(r0, PERM_ROWS), :], k_ref[0, 0, pl.ds(r0, PERM_ROWS), :]], axis=-1)
            v = v_ref[0, 0, pl.ds(r0, PERM_ROWS), :]
            for slot, (dil, p_ref) in enumerate(((4, p4_ref), (16, p16_ref))):
                length, w = seq // dil, PERM_ROWS // dil
                qkp = jnp.dot(p_ref[...], qk, preferred_element_type=F32).astype(BF16)
                vp = jnp.dot(p_ref[...], v, preferred_element_type=F32).astype(BF16)
                for r in range(dil):
                    rows = pl.ds(pl.multiple_of(r * length + blk * w, w), w)
                    qp_ref[slot, rows, :] = qkp[r * w:(r + 1) * w, :ATTN_HEAD_DIM]
                    kp_ref[slot, rows, :] = qkp[r * w:(r + 1) * w, ATTN_HEAD_DIM:]
                    vp_ref[slot, rows, :] = vp[r * w:(r + 1) * w, :]
        return carry

    lax.fori_loop(0, seq // PERM_ROWS // PERM_UNROLL, perm_body, 0)

    def dilated_tile(item, dil, slot):
        length = seq // dil
        n, r = lax.div(item, dil), lax.rem(item, dil)
        qs, ks, bias = window(n, length)
        qrow = pl.multiple_of(r * length + qs, QT)
        krow = pl.multiple_of(r * length + ks, BAND)
        out = _attn_tile(qp_ref[slot, pl.ds(qrow, QT), :], kp_ref[slot, pl.ds(krow, KW), :],
                         vp_ref[slot, pl.ds(krow, KW), :], bias)
        return out, pl.ds(qs * dil + r, QT, stride=dil)

    def body16(it, carry):
        for g in range(CHAINS):
            (acc, m, den), rows = dilated_tile(it * CHAINS + g, 16, 1)
            acc_ref[rows, :] = acc
            m_ref[rows, :] = m
            d_ref[rows, :] = den
        return carry

    lax.fori_loop(0, n_items // CHAINS, body16, 0)

    def body4(it, carry):
        for g in range(CHAINS):
            (acc, m, den), rows = dilated_tile(it * CHAINS + g, 4, 0)
            m0 = m_ref[rows, :]
            mn = jnp.maximum(m0, m)
            a, b = jnp.exp(m0 - mn), jnp.exp(m - mn)
            acc_ref[rows, :] = a * acc_ref[rows, :] + b * acc
            d_ref[rows, :] = a * d_ref[rows, :] + b * den
            m_ref[rows, :] = mn
        return carry

    lax.fori_loop(0, n_items // CHAINS, body4, 0)

    def body1(it, carry):
        for g in range(CHAINS):
            qs, ks, bias = window(it * CHAINS + g, seq)
            qs, ks = pl.multiple_of(qs, QT), pl.multiple_of(ks, BAND)
            acc, m, den = _attn_tile(q_ref[0, 0, pl.ds(qs, QT), :], k_ref[0, 0, pl.ds(ks, KW), :],
                                     v_ref[0, 0, pl.ds(ks, KW), :], bias)
            rows = pl.ds(qs, QT)
            m0 = m_ref[rows, :]
            mn = jnp.maximum(m0, m)
            a, b = jnp.exp(m0 - mn), jnp.exp(m - mn)
            num = a * acc_ref[rows, :] + b * acc
            o_ref[0, 0, rows, :] = (num / (a * d_ref[rows, :] + b * den)).astype(BF16)
        return carry

    lax.fori_loop(0, n_items // CHAINS, body1, 0)


def _perm_matrix(dil):
    w = PERM_ROWS // dil
    i = jnp.arange(PERM_ROWS)
    src = (i % w) * dil + i // w
    return (src[:, None] == jnp.arange(PERM_ROWS)[None, :]).astype(BF16)


def _attn(zqk, zv_, batch, seq, n_ah):
    n_sub = n_ah // 2
    zqk = zqk.reshape(zqk.shape[0], batch, seq, 2 * LANES)
    zv_ = zv_.reshape(zv_.shape[0], batch, seq, 2 * LANES)
    blk = (1, 1, seq, ATTN_HEAD_DIM)
    i = jnp.arange(QT)[:, None]
    jj = jnp.arange(KW)[None, :]
    bias = jnp.stack([jnp.where(jnp.abs(i + o * BAND - jj) <= BAND, 0.0, MASK_VALUE) for o in range(3)]).astype(F32)
    const = lambda shape: pl.BlockSpec(shape, lambda b, h: (0,) * len(shape))
    o = pl.pallas_call(
        functools.partial(_attn_kernel, seq=seq),
        grid=(batch, n_ah),
        in_specs=[pl.BlockSpec(blk, lambda b, h: (h // 2, b, 0, h % 2)),
                  pl.BlockSpec(blk, lambda b, h: (n_sub + h // 2, b, 0, h % 2)),
                  pl.BlockSpec(blk, lambda b, h: (h // 2, b, 0, h % 2)),
                  const((3, QT, KW)), const((PERM_ROWS, PERM_ROWS)), const((PERM_ROWS, PERM_ROWS))],
        out_specs=pl.BlockSpec(blk, lambda b, h: (h, b, 0, 0)),
        out_shape=jax.ShapeDtypeStruct((n_ah, batch, seq, ATTN_HEAD_DIM), BF16),
        scratch_shapes=[pltpu.VMEM((seq, LANES), F32)] * 3
        + [pltpu.VMEM((2, seq, ATTN_HEAD_DIM), BF16)] * 3,
        compiler_params=_cparams(("arbitrary", "arbitrary")),
        name="attn",
    )(zqk, zqk, zv_, bias, _perm_matrix(4), _perm_matrix(16))
    return o.reshape(n_ah, batch * seq, ATTN_HEAD_DIM)


def _ret_kernel(lg_ref, q_ref, k_ref, v_ref, g_ref, o_ref, y_ref, sf_ref, sb_ref, dmat_ref, *, n_tiles, tile):
    h = pl.program_id(1)
    t = pl.program_id(2)
    c = RET_CHUNK
    lgf = lg_ref[0, h]
    lgb = lg_ref[1, h]
    col = lax.broadcasted_iota(I32, (c, 1), 0).astype(F32)
    row = lax.broadcasted_iota(I32, (1, c), 1).astype(F32)
    n_iter = tile // c // RET_UNROLL

    @pl.when(t == 0)
    def _():
        sf_ref[...] = jnp.zeros_like(sf_ref)
        sb_ref[...] = jnp.zeros_like(sb_ref)
        diff = (lax.broadcasted_iota(I32, (c, c), 0) - lax.broadcasted_iota(I32, (c, c), 1)).astype(F32)
        dmat_ref[...] = jnp.where(diff >= 0, jnp.exp(lgf * jnp.maximum(diff, 0.0)), 0.0) \
            + jnp.where(diff < 0, jnp.exp(lgb * jnp.maximum(-diff, 0.0)), 0.0)

    def chunk(r0):
        rows = pl.ds(pl.multiple_of(r0, c), c)
        return rows, q_ref[0, rows, :], k_ref[0, rows, :], v_ref[0, rows, :]

    @pl.when(t < n_tiles)
    def _():
        xi = jnp.exp(lgf * (col + 1.0))
        zeta = jnp.exp(lgf * (c - 1.0 - row))
        cdec = jnp.exp(lgf * jnp.full((1, 1), float(c), F32))
        base = t * tile

        def body(it, carry):
            state = sf_ref[...]
            for u in range(RET_UNROLL):
                r0 = (it * RET_UNROLL + u) * c
                _, q, k, v = chunk(r0)
                s = lax.dot_general(q, k, (((1,), (1,)), ((), ())), preferred_element_type=F32) * dmat_ref[...]
                y = jnp.dot(s.astype(BF16), v, preferred_element_type=F32)
                y = y + jnp.dot(q, state.astype(BF16), preferred_element_type=F32) * xi
                kt = (k.astype(F32).T * zeta).astype(BF16)
                state = state * cdec + jnp.dot(kt, v, preferred_element_type=F32)
                y_ref[pl.ds(pl.multiple_of(base + r0, c), c), :] = y
            sf_ref[...] = state
            return carry

        lax.fori_loop(0, n_iter, body, 0)

    @pl.when(t >= n_tiles)
    def _():
        xi = jnp.exp(lgb * (c - col))
        zeta = jnp.exp(lgb * row)
        cdec = jnp.exp(lgb * jnp.full((1, 1), float(c), F32))
        base = (2 * n_tiles - 1 - t) * tile

        def body(it, carry):
            state = sb_ref[...]
            for u in range(RET_UNROLL):
                r0 = (tile // c - 1 - (it * RET_UNROLL + u)) * c
                rows, q, k, v = chunk(r0)
                y = y_ref[pl.ds(pl.multiple_of(base + r0, c), c), :]
                y = y + jnp.dot(q, state.astype(BF16), preferred_element_type=F32) * xi
                kt = (k.astype(F32).T * zeta).astype(BF16)
                state = state * cdec + jnp.dot(kt, v, preferred_element_type=F32)
                mu = jnp.mean(y, axis=-1, keepdims=True)
                yc = y - mu
                var = jnp.mean(yc * yc, axis=-1, keepdims=True)
                yn = yc * lax.rsqrt(var + LN_EPS)
                gate = jax.nn.silu(g_ref[0, rows, :].astype(F32))
                o_ref[0, rows, :] = (gate * yn).astype(BF16)
            sb_ref[...] = state
            return carry

        lax.fori_loop(0, n_iter, body, 0)


def _ret(zqk, zvg, lg, batch, seq, n_rh, tile):
    n_tiles = seq // tile
    per_b = seq // tile

    def rows(t):
        return jnp.where(t < n_tiles, t, 2 * n_tiles - 1 - t)

    def spec(seg):
        return pl.BlockSpec((1, tile, RET_HEAD_DIM), lambda b, h, t: (seg * n_rh + h, b * per_b + rows(t), 0))

    return pl.pallas_call(
        functools.partial(_ret_kernel, n_tiles=n_tiles, tile=tile),
        grid=(batch, n_rh, 2 * n_tiles),
        in_specs=[pl.BlockSpec(memory_space=pltpu.SMEM), spec(0), spec(1), spec(1), spec(2)],
        out_specs=pl.BlockSpec((1, tile, RET_HEAD_DIM),
                               lambda b, h, t: (h, b * per_b + jnp.where(t < n_tiles, n_tiles - 1, 2 * n_tiles - 1 - t), 0)),
        out_shape=jax.ShapeDtypeStruct((n_rh, batch * seq, RET_HEAD_DIM), BF16),
        scratch_shapes=[pltpu.VMEM((seq, RET_HEAD_DIM), F32),
                        pltpu.VMEM((RET_HEAD_DIM, RET_HEAD_DIM), F32),
                        pltpu.VMEM((RET_HEAD_DIM, RET_HEAD_DIM), F32),
                        pltpu.VMEM((RET_CHUNK, RET_CHUNK), F32)],
        compiler_params=_cparams(("arbitrary", "arbitrary", "arbitrary")),
        name="ret",
    )(lg, zqk, zqk, zvg, zvg)


def _outproj_kernel(a_ref, r_ref, x_ref, mod_ref, g0_ref, b0_ref, g1_ref, b1_ref, w_ref, x1_ref, hp_ref,
                    *, n_ah, n_rh, alpha):
    mix = jnp.concatenate([a_ref[h] for h in range(n_ah)] + [r_ref[h] for h in range(n_rh)], axis=-1)
    acc = jnp.dot(mix, w_ref[...], preferred_element_type=F32)
    xn = _ln(x_ref[...], g0_ref[...], b0_ref[...])
    y = alpha * xn + (1.0 + mod_ref[0, 2:3, :]) * acc
    x1 = _ln(y, g1_ref[...], b1_ref[...])
    x1_ref[...] = x1
    _store_token_tiles(hp_ref, _pack_bf16_pairs(x1 * (1.0 + mod_ref[0, 4:5, :]) + mod_ref[0, 3:4, :]))


def _outproj(attn, r, x2, mod3, g0, b0, g1, b1, w_out_b, seq, tm, alpha):
    t, d = x2.shape
    n_ah, n_rh = attn.shape[0], r.shape[0]
    per_b = seq // tm
    per = d // 2 // LANES
    row = lambda i: (i, 0)
    vec = pl.BlockSpec((1, d), lambda i: (0, 0))
    return pl.pallas_call(
        functools.partial(_outproj_kernel, n_ah=n_ah, n_rh=n_rh, alpha=alpha),
        grid=(t // tm,),
        in_specs=[pl.BlockSpec((n_ah, tm, ATTN_HEAD_DIM), lambda i: (0, i, 0)),
                  pl.BlockSpec((n_rh, tm, RET_HEAD_DIM), lambda i: (0, i, 0)),
                  pl.BlockSpec((tm, d), row),
                  pl.BlockSpec((1, 6, d), lambda i: (i // per_b, 0, 0)),
                  vec, vec, vec, vec,
                  pl.BlockSpec((d, d), lambda i: (0, 0))],
        out_specs=[pl.BlockSpec((tm, d), row), pl.BlockSpec((tm * per, LANES), row)],
        out_shape=[jax.ShapeDtypeStruct((t, d), F32), jax.ShapeDtypeStruct((t * per, LANES), U32)],
        compiler_params=_cparams(("arbitrary",)),
        name="outproj",
    )(attn, r, x2, mod3, g0, b0, g1, b1, w_out_b)


def _first_argmax(rows):
    best, idx = rows[0], jnp.zeros(rows[0].shape, I32)
    for e in range(1, len(rows)):
        better = rows[e] > best
        idx = jnp.where(better, e, idx)
        best = jnp.maximum(best, rows[e])
    return best, idx


def _router_kernel(h_ref, w_ref, b_ref, u_ref, eid_ref, wt_ref, rank_ref, cnt_ref, carry_ref, *, per):
    i = pl.program_id(0)

    @pl.when(i == 0)
    def _():
        carry_ref[...] = jnp.zeros_like(carry_ref)

    tm = eid_ref.shape[2]
    lt = lax.dot_general(w_ref[...], _load_token_tiles(h_ref, tm, per), (((1,), (1,)), ((), ())),
                         preferred_element_type=F32) + b_ref[...]
    grow = [lt[g:g + 1, :] for g in range(N_GROUPS)]
    gmax, gsel = _first_argmax(grow)
    gsum = grow[0] * 0.0
    for g in range(N_GROUPS):
        gsum = gsum + jnp.exp(grow[g] - gmax)
    pg = 1.0 / gsum
    srow = []
    for e in range(EXPERTS_PER_GROUP):
        r = lt[N_GROUPS + e:N_GROUPS + e + 1, :]
        for g in range(1, N_GROUPS):
            o = N_GROUPS + g * EXPERTS_PER_GROUP + e
            r = jnp.where(gsel == g, lt[o:o + 1, :], r)
        srow.append(r)
    v1, i1 = _first_argmax(srow)
    v2, i2 = _first_argmax([jnp.where(i1 == e, -jnp.inf, srow[e]) for e in range(EXPERTS_PER_GROUP)])
    e2 = jnp.exp(v2 - v1)
    den = 1.0 + e2
    wt_ref[0:1, :] = (1.0 / den) * pg
    wt_ref[1:2, :] = (e2 / den) * pg
    eid0 = gsel * EXPERTS_PER_GROUP + i1
    eid1 = gsel * EXPERTS_PER_GROUP + i2
    eid_ref[0, 0:1, :] = eid0
    eid_ref[0, 1:2, :] = eid1
    erow = lax.broadcasted_iota(I32, (N_EXPERTS, tm), 0)
    oh0 = (erow == eid0).astype(F32)
    oh1 = (erow == eid1).astype(F32)
    oh = oh0 + oh1
    incl = jnp.dot(oh.astype(BF16), u_ref[...], preferred_element_type=F32)
    before = carry_ref[:, 0:1] + incl - oh
    rank_ref[0, 0:1, :] = jnp.sum(oh0 * before, axis=0, keepdims=True).astype(I32)
    rank_ref[0, 1:2, :] = jnp.sum(oh1 * before, axis=0, keepdims=True).astype(I32)
    carry = carry_ref[...] + jnp.sum(oh, axis=1, keepdims=True)
    carry_ref[...] = carry
    cnt_ref[...] = carry


def _router(hp, wr, br, tm):
    d = wr.shape[1]
    per = d // 2 // LANES
    t = hp.shape[0] // per
    tri = (lax.broadcasted_iota(I32, (tm, tm), 0) <= lax.broadcasted_iota(I32, (tm, tm), 1)).astype(BF16)
    tile3 = pl.BlockSpec((1, 2, tm), lambda i: (i, 0, 0))
    return pl.pallas_call(
        functools.partial(_router_kernel, per=per),
        grid=(t // tm,),
        in_specs=[pl.BlockSpec((tm * per, LANES), lambda i: (i, 0)),
                  pl.BlockSpec((ROUTER_ROWS, d), lambda i: (0, 0)),
                  pl.BlockSpec((ROUTER_ROWS, 1), lambda i: (0, 0)),
                  pl.BlockSpec((tm, tm), lambda i: (0, 0))],
        out_specs=[tile3, pl.BlockSpec((2, tm), lambda i: (0, i)), tile3,
                   pl.BlockSpec((N_EXPERTS, LANES), lambda i: (0, 0))],
        out_shape=[jax.ShapeDtypeStruct((t // tm, 2, tm), I32), jax.ShapeDtypeStruct((2, t), F32),
                   jax.ShapeDtypeStruct((t // tm, 2, tm), I32), jax.ShapeDtypeStruct((N_EXPERTS, LANES), F32)],
        scratch_shapes=[pltpu.VMEM((N_EXPERTS, LANES), F32)],
        compiler_params=_cparams(("arbitrary",)),
        name="router",
    )(hp, wr, br, tri)


def _tile_copy(src, s, dst, d, per, sem):
    return pltpu.make_async_copy(src.at[pl.ds(pl.multiple_of(s * per, per), per), :],
                                 dst.at[pl.ds(pl.multiple_of(d * per, per), per), :], sem)


def _dispatch_kernel(ps_ref, lo_ref, hi_ref, eid_ref, rank_ref, h_ref, xb_hbm, stage, zero, sem, *, per):
    i = pl.program_id(0)
    n = pl.num_programs(0)
    slot = i % 2
    tm = eid_ref.shape[2]

    def wait_tile(s):
        pltpu.make_async_copy(xb_hbm.at[pl.ds(0, 2 * tm * per), :], xb_hbm.at[pl.ds(0, 2 * tm * per), :], sem.at[s]).wait()

    @pl.when(i >= 2)
    def _():
        wait_tile(slot)

    stage[slot] = h_ref[...]

    def body(r, carry):
        for k in range(2):
            dst = ps_ref[eid_ref[0, k, r]] + rank_ref[0, k, r]
            _tile_copy(stage.at[slot], r, xb_hbm, dst, per, sem.at[slot]).start(priority=k)
        return carry

    lax.fori_loop(0, tm, body, 0, unroll=8)

    @pl.when(i == n - 1)
    def _():
        @pl.when(n >= 2)
        def _():
            wait_tile(1 - slot)
        wait_tile(slot)
        zero[...] = jnp.zeros_like(zero)

        def fill(e, carry):
            def one(s, c):
                _tile_copy(zero, 0, xb_hbm, s, per, sem.at[2]).start()
                return c
            lax.fori_loop(lo_ref[e], hi_ref[e], one, 0)

            def one_wait(s, c):
                _tile_copy(zero, 0, xb_hbm, s, per, sem.at[2]).wait()
                return c
            lax.fori_loop(lo_ref[e], hi_ref[e], one_wait, 0)
            return carry

        lax.fori_loop(0, N_EXPERTS + 1, fill, 0)


def _dispatch(pstart, fill_lo, fill_hi, eid3, rank3, hp, n_slots, per):
    n, _, tm = eid3.shape
    smem3 = pl.BlockSpec((1, 2, tm), lambda i, *_: (i, 0, 0), memory_space=pltpu.SMEM)
    grid_spec = pltpu.PrefetchScalarGridSpec(
        num_scalar_prefetch=3,
        grid=(n,),
        in_specs=[smem3, smem3, pl.BlockSpec((tm * per, LANES), lambda i, *_: (i, 0))],
        out_specs=pl.BlockSpec(memory_space=pl.ANY),
        scratch_shapes=[pltpu.VMEM((2, tm * per, LANES), U32), pltpu.VMEM((per, LANES), U32),
                        pltpu.SemaphoreType.DMA((3,))])
    return pl.pallas_call(
        functools.partial(_dispatch_kernel, per=per),
        grid_spec=grid_spec,
        out_shape=jax.ShapeDtypeStruct((n_slots * per, LANES), U32),
        compiler_params=_cparams(("arbitrary",)),
        name="dispatch",
    )(pstart, fill_lo, fill_hi, eid3, rank3, hp)


def _expert_kernel(be_ref, first_ref, par_ref, nxt_ref, has_ref, x_ref, w1_hbm, w3_hbm, w2_hbm, y_ref,
                   wf1, wf3, wf2, w1b, w3b, w2b, sem, *, per):
    i = pl.program_id(0)

    def weight_copies(e, s):
        return [pltpu.make_async_copy(src.at[e], dst.at[s], sem.at[s])
                for src, dst in ((w1_hbm, wf1), (w3_hbm, wf3), (w2_hbm, wf2))]

    @pl.when(i == 0)
    def _():
        for cp in weight_copies(be_ref[0], 0):
            cp.start()

    @pl.when(first_ref[i] == 1)
    def _():
        s = par_ref[i]
        for cp in weight_copies(be_ref[i], s):
            cp.wait()

        @pl.when(has_ref[i] == 1)
        def _():
            for cp in weight_copies(nxt_ref[i], 1 - s):
                cp.start()

        w1b[...] = wf1[s].astype(BF16)
        w3b[...] = wf3[s].astype(BF16)
        w2b[...] = wf2[s].astype(BF16)

    x = _load_token_tiles(x_ref, MOE_BLOCK, per)
    a = jnp.dot(x, w1b[...], preferred_element_type=F32)
    b = jnp.dot(x, w3b[...], preferred_element_type=F32)
    mid = (jax.nn.silu(a) * b).astype(BF16)
    _store_token_tiles(y_ref, _pack_bf16_pairs(jnp.dot(mid, w2b[...], preferred_element_type=F32)))


def _experts(blk_e, xb, w1, w3, w2):
    d, ff = w1.shape[1], w1.shape[2]
    per = d // 2 // LANES
    nblk = xb.shape[0] // per // MOE_BLOCK
    rows = pl.BlockSpec((MOE_BLOCK * per, LANES), lambda i, *_: (i, 0))
    first = jnp.concatenate([jnp.ones((1,), I32), (blk_e[1:] != blk_e[:-1]).astype(I32)])
    parity = (jnp.cumsum(first) - 1) % 2
    nxt = jnp.min(jnp.where(blk_e[None, :] > blk_e[:, None], blk_e[None, :], N_EXPERTS), axis=1)
    has_next = (nxt < N_EXPERTS).astype(I32)
    nxt = jnp.minimum(nxt, N_EXPERTS - 1)
    grid_spec = pltpu.PrefetchScalarGridSpec(
        num_scalar_prefetch=5,
        grid=(nblk,),
        in_specs=[rows] + [pl.BlockSpec(memory_space=pl.ANY)] * 3,
        out_specs=rows,
        scratch_shapes=[pltpu.VMEM((2, d, ff), F32), pltpu.VMEM((2, d, ff), F32), pltpu.VMEM((2, ff, d), F32),
                        pltpu.VMEM((d, ff), BF16), pltpu.VMEM((d, ff), BF16), pltpu.VMEM((ff, d), BF16),
                        pltpu.SemaphoreType.DMA((2,))])
    return pl.pallas_call(
        functools.partial(_expert_kernel, per=per),
        grid_spec=grid_spec,
        out_shape=jax.ShapeDtypeStruct(xb.shape, U32),
        compiler_params=_cparams(("arbitrary",)),
        name="experts",
    )(blk_e, first, parity.astype(I32), nxt.astype(I32), has_next, xb, w1, w3, w2)


def _combine_kernel(ps_ref, eid_ref, rank_ref, eidn_ref, rankn_ref, y_hbm, wt_ref, x1_ref, mod_ref, g_ref, b_ref,
                    o_ref, ybuf, sem, *, alpha, per):
    i = pl.program_id(0)
    n = pl.num_programs(0)
    slot = i % 2
    tm = x1_ref.shape[0]

    def start(e_ref, r_ref, s):
        def body(r, carry):
            for k in range(2):
                src = ps_ref[e_ref[0, k, r]] + r_ref[0, k, r]
                _tile_copy(y_hbm, src, ybuf.at[s, k], r, per, sem.at[s]).start(priority=k)
            return carry
        lax.fori_loop(0, tm, body, 0, unroll=8)

    @pl.when(i == 0)
    def _():
        start(eid_ref, rank_ref, 0)

    @pl.when(i + 1 < n)
    def _():
        start(eidn_ref, rankn_ref, 1 - slot)

    for k in range(2):
        pltpu.make_async_copy(y_hbm.at[pl.ds(0, tm * per), :], ybuf.at[slot, k], sem.at[slot]).wait()
    wt = wt_ref[...]
    y0 = _load_token_tiles(ybuf.at[slot, 0], tm, per).astype(F32)
    y1 = _load_token_tiles(ybuf.at[slot, 1], tm, per).astype(F32)
    ffn = wt[:, 0:1] * y0 + wt[:, 1:2] * y1
    y = alpha * x1_ref[...] + (1.0 + mod_ref[0, 5:6, :]) * ffn
    o_ref[...] = _ln(y, g_ref[...], b_ref[...])


def _combine(pstart, eid3, rank3, yb, wt_t, x1, mod3, g2, b2, seq, alpha):
    t, d = x1.shape
    per = d // 2 // LANES
    n, _, tm = eid3.shape
    per_b = seq // tm
    cur = pl.BlockSpec((1, 2, tm), lambda i, ps: (i, 0, 0), memory_space=pltpu.SMEM)
    nxt = pl.BlockSpec((1, 2, tm), lambda i, ps: (jnp.minimum(i + 1, n - 1), 0, 0), memory_space=pltpu.SMEM)
    vec = pl.BlockSpec((1, d), lambda i, ps: (0, 0))
    grid_spec = pltpu.PrefetchScalarGridSpec(
        num_scalar_prefetch=1,
        grid=(n,),
        in_specs=[cur, cur, nxt, nxt,
                  pl.BlockSpec(memory_space=pl.ANY),
                  pl.BlockSpec((tm, 2), lambda i, ps: (i, 0)),
                  pl.BlockSpec((tm, d), lambda i, ps: (i, 0)),
                  pl.BlockSpec((1, 6, d), lambda i, ps: (i // per_b, 0, 0)),
                  vec, vec],
        out_specs=pl.BlockSpec((tm, d), lambda i, ps: (i, 0)),
        scratch_shapes=[pltpu.VMEM((2, 2, tm * per, LANES), U32), pltpu.SemaphoreType.DMA((2,))])
    return pl.pallas_call(
        functools.partial(_combine_kernel, alpha=alpha, per=per),
        grid_spec=grid_spec,
        out_shape=jax.ShapeDtypeStruct((t, d), F32),
        compiler_params=_cparams(("arbitrary",)),
        name="combine",
    )(pstart, eid3, rank3, eid3, rank3, yb, wt_t, x1, mod3, g2, b2)


def _pick_tile(n, want):
    tm = min(n, want)
    assert n % tm == 0
    return tm


def kernel(x, c, positions, ln0_g, ln0_b, w_ada, b_ada, w_in, w_out, ret_log_decay_f, ret_log_decay_b,
           ln1_g, ln1_b, w_group, b_group, w_sub, b_sub, w1, w3, w2, ln2_g, ln2_b):
    batch, seq, d = x.shape
    depth = w_ada.shape[0]
    t = batch * seq
    n_ah = d // 2 // ATTN_HEAD_DIM
    n_rh = d // 2 // RET_HEAD_DIM
    assert depth == 1 and d % (2 * RET_HEAD_DIM) == 0 and seq % (max(DILATIONS) * KW) == 0 and batch <= 8
    assert seq % (QT * CHAINS) == 0
    alpha = (2 * depth) ** 0.25

    inv_rope = ROPE_THETA ** (-jnp.arange(0, ROPE_DIM, 2, dtype=F32) / ROPE_DIM)
    inv_ret = RET_THETA ** (-jnp.linspace(0.0, 1.0, RET_HEAD_DIM // 2, dtype=F32))
    fa = jnp.zeros((1, LANES), F32).at[0, :ROPE_DIM].set(jnp.concatenate([inv_rope, inv_rope]))
    fr = inv_ret.reshape(1, LANES)
    posb = jnp.broadcast_to(positions.astype(F32).reshape(t, 1), (t, LANES))
    c8 = jnp.zeros((8, d), F32).at[:batch].set(c)
    row = lambda v: v.reshape(1, d)

    xs = x.reshape(t, d)
    mod = _ada(c8, w_ada[0], b_ada[0].reshape(1, -1))
    mod3 = mod[:batch].reshape(batch, 6, d)
    h, tabs = _prep(xs, posb, mod3, row(ln0_g), row(ln0_b), fa, fr, seq, _pick_tile(seq, 512))
    w_in_b = w_in[0].astype(BF16)
    tmp = _pick_tile(seq, 1024)
    z_aqk = _proj(_proj_attn_kernel, lambda j: j, 2, h, w_in_b, tabs, tmp, "proj_attn")
    z_rqk = _proj(_proj_ret_kernel, lambda j: 3 + j, 2, h, w_in_b, tabs, tmp, "proj_ret")
    z_pl = _proj(_proj_plain_kernel, lambda j: 2 + 3 * jnp.minimum(j, 1) + jnp.maximum(j - 1, 0), 3,
                 h, w_in_b, None, tmp, "proj_plain")
    attn = _attn(z_aqk, z_pl, batch, seq, n_ah)
    lg = jnp.stack([ret_log_decay_f[0], ret_log_decay_b[0]]).astype(F32)
    r = _ret(z_rqk, z_pl, lg, batch, seq, n_rh, _pick_tile(seq, 1024))
    x1, hp = _outproj(attn, r, xs, mod3, row(ln0_g), row(ln0_b), row(ln1_g[0]), row(ln1_b[0]),
                      w_out[0].astype(BF16), seq, _pick_tile(seq, 256), alpha)
    wr = jnp.zeros((ROUTER_ROWS, d), F32)
    wr = wr.at[:N_GROUPS].set(w_group[0].T)
    wr = wr.at[N_GROUPS:N_GROUPS + N_EXPERTS].set(w_sub[0].transpose(0, 2, 1).reshape(N_EXPERTS, d))
    br = jnp.zeros((ROUTER_ROWS, 1), F32)
    br = br.at[:N_GROUPS, 0].set(b_group[0]).at[N_GROUPS:N_GROUPS + N_EXPERTS, 0].set(b_sub[0].reshape(-1))
    eid3, wt, rank3, cnt = _router(hp, wr.astype(BF16), br, _pick_tile(seq, 512))
    counts = cnt[:, 0].astype(I32)
    padded = (counts + MOE_BLOCK - 1) // MOE_BLOCK * MOE_BLOCK
    pend = jnp.cumsum(padded)
    pstart = pend - padded
    n_slots = 2 * t + N_EXPERTS * MOE_BLOCK
    nblk = n_slots // MOE_BLOCK
    starts = jnp.arange(nblk, dtype=I32) * MOE_BLOCK
    blk_e = jnp.minimum(jnp.sum((pend[None, :] <= starts[:, None]).astype(I32), axis=1), N_EXPERTS - 1)
    fill_lo = jnp.concatenate([pstart + counts, pend[-1:]])
    fill_hi = jnp.concatenate([pend, jnp.full((1,), n_slots, I32)])
    xb = _dispatch(pstart, fill_lo, fill_hi, eid3, rank3, hp, n_slots, d // 2 // LANES)
    yb = _experts(blk_e, xb, w1[0], w3[0], w2[0])
    out = _combine(pstart, eid3, rank3, yb, wt.T, x1, mod3, row(ln2_g[0]), row(ln2_b[0]), seq, alpha)
    return out.reshape(batch, seq, d)
```

```python
import functools

import jax
import jax.numpy as jnp
from jax import lax
from jax.experimental import pallas as pl
from jax.experimental.pallas import tpu as pltpu

F32 = jnp.float32
BF16 = jnp.bfloat16
I32 = jnp.int32
U32 = jnp.uint32

LANES = 128
ATTN_HEAD_DIM = 128
RET_HEAD_DIM = 256
DILATIONS = (1, 4, 16)
BAND = 64
QT = 128
KW = QT + 2 * BAND
CHAINS = 16
PERM_ROWS = 256
PERM_UNROLL = 4
ROPE_THETA = 500000.0
ROPE_DIM = ATTN_HEAD_DIM // 4
RET_THETA = 10000.0
RET_CHUNK = 128
RET_UNROLL = 4
N_GROUPS = 4
EXPERTS_PER_GROUP = 8
N_EXPERTS = N_GROUPS * EXPERTS_PER_GROUP
MOE_BLOCK = 256
LN_EPS = 1e-5
MASK_VALUE = -1e30
N_SEG = 7
ROUTER_ROWS = 48
HI_MASK = 0xFFFF0000
VMEM_LIMIT = 56 * 1024 * 1024


def _cparams(sem, vmem=VMEM_LIMIT):
    return pltpu.CompilerParams(dimension_semantics=sem, vmem_limit_bytes=vmem)


def _ln(x, g, b):
    mu = jnp.mean(x, axis=-1, keepdims=True)
    xc = x - mu
    var = jnp.mean(xc * xc, axis=-1, keepdims=True)
    return xc * lax.rsqrt(var + LN_EPS) * g + b


def _pack_bf16_pairs(h):
    bits = lax.bitcast_convert_type(h.astype(BF16).astype(F32), U32)
    n = h.shape[1] // 2
    return (bits[:, :n] >> 16) | (bits[:, n:] & jnp.uint32(HI_MASK))


def _store_token_tiles(ref, packed):
    m, n = packed.shape
    per = n // LANES
    for s in range(per):
        ref[pl.ds(s, m, stride=per), :] = packed[:, s * LANES:(s + 1) * LANES]


def _load_token_tiles(ref, m, per):
    slabs = [ref[pl.ds(s, m, stride=per), :] for s in range(per)]
    lo = [lax.bitcast_convert_type(p << 16, F32) for p in slabs]
    hi = [lax.bitcast_convert_type(p & jnp.uint32(HI_MASK), F32) for p in slabs]
    return jnp.concatenate(lo + hi, axis=-1).astype(BF16)


def _ada_kernel(c_ref, w_ref, b_ref, o_ref):
    cs = jax.nn.silu(c_ref[...])
    o_ref[...] = jnp.dot(cs.astype(BF16), w_ref[...].astype(BF16), preferred_element_type=F32) + b_ref[...]


def _ada(c8, w_ada, b_ada):
    d, n = w_ada.shape
    tn = min(n, 512)
    return pl.pallas_call(
        _ada_kernel,
        grid=(n // tn,),
        in_specs=[pl.BlockSpec((8, d), lambda j: (0, 0)),
                  pl.BlockSpec((d, tn), lambda j: (0, j)),
                  pl.BlockSpec((1, tn), lambda j: (0, j))],
        out_specs=pl.BlockSpec((8, tn), lambda j: (0, j)),
        out_shape=jax.ShapeDtypeStruct((8, n), F32),
        compiler_params=_cparams(("arbitrary",)),
        name="ada",
    )(c8, w_ada, b_ada)


def _prep_kernel(x_ref, pos_ref, mod_ref, g_ref, b_ref, fa_ref, fr_ref, h_ref, tab_ref):
    xn = _ln(x_ref[...], g_ref[...], b_ref[...])
    h_ref[...] = (xn * (1.0 + mod_ref[0, 1:2, :]) + mod_ref[0, 0:1, :]).astype(BF16)
    pos = pos_ref[...]
    ang = pos * fa_ref[...]
    c, s = jnp.cos(ang), jnp.sin(ang)
    lane = lax.broadcasted_iota(I32, ang.shape, 1)
    half = ROPE_DIM // 2
    tab_ref[0] = c
    tab_ref[1] = jnp.where(lane < half, -s, 0.0)
    tab_ref[2] = jnp.where((lane >= half) & (lane < 2 * half), s, 0.0)
    angr = pos * fr_ref[...]
    tab_ref[3] = jnp.cos(angr)
    tab_ref[4] = jnp.sin(angr)


def _prep(x2, posb, mod3, ln0_g, ln0_b, fa, fr, seq, tm):
    t, d = x2.shape
    per_b = seq // tm
    vec = lambda n: pl.BlockSpec((1, n), lambda i: (0, 0))
    return pl.pallas_call(
        _prep_kernel,
        grid=(t // tm,),
        in_specs=[pl.BlockSpec((tm, d), lambda i: (i, 0)),
                  pl.BlockSpec((tm, LANES), lambda i: (i, 0)),
                  pl.BlockSpec((1, 6, d), lambda i: (i // per_b, 0, 0)),
                  vec(d), vec(d), vec(LANES), vec(LANES)],
        out_specs=[pl.BlockSpec((tm, d), lambda i: (i, 0)), pl.BlockSpec((5, tm, LANES), lambda i: (0, i, 0))],
        out_shape=[jax.ShapeDtypeStruct((t, d), BF16), jax.ShapeDtypeStruct((5, t, LANES), F32)],
        compiler_params=_cparams(("arbitrary",)),
        name="prep",
    )(x2, posb, mod3, ln0_g, ln0_b, fa, fr)


def _proj_chunks(h_ref, w_ref, n_sub):
    for hc in range(n_sub):
        yield hc, jnp.dot(h_ref[...], w_ref[:, hc * 2 * LANES:(hc + 1) * 2 * LANES], preferred_element_type=F32)


def _proj_attn_kernel(h_ref, w_ref, tab_ref, z_ref, *, n_sub):
    scale = jnp.where(pl.program_id(0) == 0, ATTN_HEAD_DIM ** -0.5, 1.0).astype(F32)
    ca, s1, s2 = tab_ref[0] * scale, tab_ref[1] * scale, tab_ref[2] * scale
    half = ROPE_DIM // 2
    for hc, res in _proj_chunks(h_ref, w_ref, n_sub):
        for lanes in (slice(0, LANES), slice(LANES, 2 * LANES)):
            x = res[:, lanes]
            r = x * ca + pltpu.roll(x, LANES - half, 1) * s1 + pltpu.roll(x, half, 1) * s2
            z_ref[hc, :, lanes] = r.astype(BF16)


def _proj_ret_kernel(h_ref, w_ref, tab_ref, z_ref, *, n_sub):
    scale = jnp.where(pl.program_id(0) == 1, RET_HEAD_DIM ** -0.5, 1.0).astype(F32)
    cr, sr = tab_ref[3] * scale, tab_ref[4] * scale
    for hc, res in _proj_chunks(h_ref, w_ref, n_sub):
        x1, x2 = res[:, :LANES], res[:, LANES:]
        z_ref[hc, :, :LANES] = (x1 * cr - x2 * sr).astype(BF16)
        z_ref[hc, :, LANES:] = (x2 * cr + x1 * sr).astype(BF16)


def _proj_plain_kernel(h_ref, w_ref, z_ref, *, n_sub):
    for hc, res in _proj_chunks(h_ref, w_ref, n_sub):
        z_ref[hc] = res.astype(BF16)


def _proj(body, col_of, n_seg, h, w_in_b, tabs, tm, name):
    t, d = h.shape
    tn = d // 2
    n_sub = tn // (2 * LANES)
    in_specs = [pl.BlockSpec((tm, d), lambda j, i: (i, 0)),
                pl.BlockSpec((d, tn), lambda j, i: (0, col_of(j)))]
    args = [h, w_in_b]
    if tabs is not None:
        in_specs.append(pl.BlockSpec((5, tm, LANES), lambda j, i: (0, i, 0)))
        args.append(tabs)
    return pl.pallas_call(
        functools.partial(body, n_sub=n_sub),
        grid=(n_seg, t // tm),
        in_specs=in_specs,
        out_specs=pl.BlockSpec((n_sub, tm, 2 * LANES), lambda j, i: (j, i, 0)),
        out_shape=jax.ShapeDtypeStruct((n_seg * n_sub, t, 2 * LANES), BF16),
        compiler_params=_cparams(("arbitrary", "arbitrary")),
        name=name,
    )(*args)


def _attn_tile(q, k, v, bias):
    s = lax.dot_general(q, k, (((1,), (1,)), ((), ())), preferred_element_type=F32) + bias
    m = jnp.max(jnp.maximum(s[:, :LANES], s[:, LANES:]), axis=-1, keepdims=True)
    p = jnp.exp(s - m)
    den = jnp.sum(p[:, :LANES] + p[:, LANES:], axis=-1, keepdims=True)
    acc = jnp.dot(p.astype(BF16), v, preferred_element_type=F32)
    return acc, jnp.broadcast_to(m, acc.shape), jnp.broadcast_to(den, acc.shape)


def _attn_kernel(q_ref, k_ref, v_ref, bias_ref, p4_ref, p16_ref, o_ref,
                 acc_ref, m_ref, d_ref, qp_ref, kp_ref, vp_ref, *, seq):
    n_items = seq // QT

    def window(n, length):
        qs = n * QT
        ks = jnp.clip(qs - BAND, 0, length - KW)
        return qs, ks, bias_ref[lax.div(qs - ks, BAND)]

    def perm_body(it, carry):
        for u in range(PERM_UNROLL):
            blk = it * PERM_UNROLL + u
            r0 = pl.multiple_of(blk * PERM_ROWS, PERM_ROWS)
            qk = jnp.concatenate([q_ref[0, 0, pl.ds(r0, PERM_ROWS), :], k_ref[0, 0, pl.ds(r0, PERM_ROWS), :]], axis=-1)
            v = v_ref[0, 0, pl.ds(r0, PERM_ROWS), :]
            for slot, (dil, p_ref) in enumerate(((4, p4_ref), (16, p16_ref))):
                length, w = seq // dil, PERM_ROWS // dil
                qkp = jnp.dot(p_ref[...], qk, preferred_element_type=F32).astype(BF16)
                vp = jnp.dot(p_ref[...], v, preferred_element_type=F32).astype(BF16)
                for r in range(dil):
                    rows = pl.ds(pl.multiple_of(r * length + blk * w, w), w)
                    qp_ref[slot, rows, :] = qkp[r * w:(r + 1) * w, :ATTN_HEAD_DIM]
                    kp_ref[slot, rows, :] = qkp[r * w:(r + 1) * w, ATTN_HEAD_DIM:]
                    vp_ref[slot, rows, :] = vp[r * w:(r + 1) * w, :]
        return carry

    lax.fori_loop(0, seq // PERM_ROWS // PERM_UNROLL, perm_body, 0)

    def dilated_tile(item, dil, slot):
        length = seq // dil
        n, r = lax.div(item, dil), lax.rem(item, dil)
        qs, ks, bias = window(n, length)
        qrow = pl.multiple_of(r * length + qs, QT)
        krow = pl.multiple_of(r * length + ks, BAND)
        out = _attn_tile(qp_ref[slot, pl.ds(qrow, QT), :], kp_ref[slot, pl.ds(krow, KW), :],
                         vp_ref[slot, pl.ds(krow, KW), :], bias)
        return out, pl.ds(qs * dil + r, QT, stride=dil)

    def body16(it, carry):
        for g in range(CHAINS):
            (acc, m, den), rows = dilated_tile(it * CHAINS + g, 16, 1)
            acc_ref[rows, :] = acc
            m_ref[rows, :] = m
            d_ref[rows, :] = den
        return carry

    lax.fori_loop(0, n_items // CHAINS, body16, 0)

    def body4(it, carry):
        for g in range(CHAINS):
            (acc, m, den), rows = dilated_tile(it * CHAINS + g, 4, 0)
            m0 = m_ref[rows, :]
            mn = jnp.maximum(m0, m)
            a, b = jnp.exp(m0 - mn), jnp.exp(m - mn)
            acc_ref[rows, :] = a * acc_ref[rows, :] + b * acc
            d_ref[rows, :] = a * d_ref[rows, :] + b * den
            m_ref[rows, :] = mn
        return carry

    lax.fori_loop(0, n_items // CHAINS, body4, 0)

    def body1(it, carry):
        for g in range(CHAINS):
            qs, ks, bias = window(it * CHAINS + g, seq)
            qs, ks = pl.multiple_of(qs, QT), pl.multiple_of(ks, BAND)
            acc, m, den = _attn_tile(q_ref[0, 0, pl.ds(qs, QT), :], k_ref[0, 0, pl.ds(ks, KW), :],
                                     v_ref[0, 0, pl.ds(ks, KW), :], bias)
            rows = pl.ds(qs, QT)
            m0 = m_ref[rows, :]
            mn = jnp.maximum(m0, m)
            a, b = jnp.exp(m0 - mn), jnp.exp(m - mn)
            num = a * acc_ref[rows, :] + b * acc
            o_ref[0, 0, rows, :] = (num / (a * d_ref[rows, :] + b * den)).astype(BF16)
        return carry

    lax.fori_loop(0, n_items // CHAINS, body1, 0)


def _perm_matrix(dil):
    w = PERM_ROWS // dil
    i = jnp.arange(PERM_ROWS)
    src = (i % w) * dil + i // w
    return (src[:, None] == jnp.arange(PERM_ROWS)[None, :]).astype(BF16)


def _attn(zqk, zv_, batch, seq, n_ah):
    n_sub = n_ah // 2
    zqk = zqk.reshape(zqk.shape[0], batch, seq, 2 * LANES)
    zv_ = zv_.reshape(zv_.shape[0], batch, seq, 2 * LANES)
    blk = (1, 1, seq, ATTN_HEAD_DIM)
    i = jnp.arange(QT)[:, None]
    jj = jnp.arange(KW)[None, :]
    bias = jnp.stack([jnp.where(jnp.abs(i + o * BAND - jj) <= BAND, 0.0, MASK_VALUE) for o in range(3)]).astype(F32)
    const = lambda shape: pl.BlockSpec(shape, lambda b, h: (0,) * len(shape))
    o = pl.pallas_call(
        functools.partial(_attn_kernel, seq=seq),
        grid=(batch, n_ah),
        in_specs=[pl.BlockSpec(blk, lambda b, h: (h // 2, b, 0, h % 2)),
                  pl.BlockSpec(blk, lambda b, h: (n_sub + h // 2, b, 0, h % 2)),
                  pl.BlockSpec(blk, lambda b, h: (h // 2, b, 0, h % 2)),
                  const((3, QT, KW)), const((PERM_ROWS, PERM_ROWS)), const((PERM_ROWS, PERM_ROWS))],
        out_specs=pl.BlockSpec(blk, lambda b, h: (h, b, 0, 0)),
        out_shape=jax.ShapeDtypeStruct((n_ah, batch, seq, ATTN_HEAD_DIM), BF16),
        scratch_shapes=[pltpu.VMEM((seq, LANES), F32)] * 3
        + [pltpu.VMEM((2, seq, ATTN_HEAD_DIM), BF16)] * 3,
        compiler_params=_cparams(("arbitrary", "arbitrary")),
        name="attn",
    )(zqk, zqk, zv_, bias, _perm_matrix(4), _perm_matrix(16))
    return o.reshape(n_ah, batch * seq, ATTN_HEAD_DIM)


def _ret_kernel(lg_ref, q_ref, k_ref, v_ref, g_ref, o_ref, y_ref, sf_ref, sb_ref, dmat_ref, *, n_tiles, tile):
    h = pl.program_id(1)
    t = pl.program_id(2)
    c = RET_CHUNK
    lgf = lg_ref[0, h]
    lgb = lg_ref[1, h]
    col = lax.broadcasted_iota(I32, (c, 1), 0).astype(F32)
    row = lax.broadcasted_iota(I32, (1, c), 1).astype(F32)
    n_iter = tile // c // RET_UNROLL

    @pl.when(t == 0)
    def _():
        sf_ref[...] = jnp.zeros_like(sf_ref)
        sb_ref[...] = jnp.zeros_like(sb_ref)
        diff = (lax.broadcasted_iota(I32, (c, c), 0) - lax.broadcasted_iota(I32, (c, c), 1)).astype(F32)
        dmat_ref[...] = jnp.where(diff >= 0, jnp.exp(lgf * jnp.maximum(diff, 0.0)), 0.0) \
            + jnp.where(diff < 0, jnp.exp(lgb * jnp.maximum(-diff, 0.0)), 0.0)

    def chunk(r0):
        rows = pl.ds(pl.multiple_of(r0, c), c)
        return rows, q_ref[0, rows, :], k_ref[0, rows, :], v_ref[0, rows, :]

    @pl.when(t < n_tiles)
    def _():
        xi = jnp.exp(lgf * (col + 1.0))
        zeta = jnp.exp(lgf * (c - 1.0 - row))
        cdec = jnp.exp(lgf * jnp.full((1, 1), float(c), F32))
        base = t * tile

        def body(it, carry):
            state = sf_ref[...]
            for u in range(RET_UNROLL):
                r0 = (it * RET_UNROLL + u) * c
                _, q, k, v = chunk(r0)
                s = lax.dot_general(q, k, (((1,), (1,)), ((), ())), preferred_element_type=F32) * dmat_ref[...]
                y = jnp.dot(s.astype(BF16), v, preferred_element_type=F32)
                y = y + jnp.dot(q, state.astype(BF16), preferred_element_type=F32) * xi
                kt = (k.astype(F32).T * zeta).astype(BF16)
                state = state * cdec + jnp.dot(kt, v, preferred_element_type=F32)
                y_ref[pl.ds(pl.multiple_of(base + r0, c), c), :] = y
            sf_ref[...] = state
            return carry

        lax.fori_loop(0, n_iter, body, 0)

    @pl.when(t >= n_tiles)
    def _():
        xi = jnp.exp(lgb * (c - col))
        zeta = jnp.exp(lgb * row)
        cdec = jnp.exp(lgb * jnp.full((1, 1), float(c), F32))
        base = (2 * n_tiles - 1 - t) * tile

        def body(it, carry):
            state = sb_ref[...]
            for u in range(RET_UNROLL):
                r0 = (tile // c - 1 - (it * RET_UNROLL + u)) * c
                rows, q, k, v = chunk(r0)
                y = y_ref[pl.ds(pl.multiple_of(base + r0, c), c), :]
                y = y + jnp.dot(q, state.astype(BF16), preferred_element_type=F32) * xi
                kt = (k.astype(F32).T * zeta).astype(BF16)
                state = state * cdec + jnp.dot(kt, v, preferred_element_type=F32)
                mu = jnp.mean(y, axis=-1, keepdims=True)
                yc = y - mu
                var = jnp.mean(yc * yc, axis=-1, keepdims=True)
                yn = yc * lax.rsqrt(var + LN_EPS)
                gate = jax.nn.silu(g_ref[0, rows, :].astype(F32))
                o_ref[0, rows, :] = (gate * yn).astype(BF16)
            sb_ref[...] = state
            return carry

        lax.fori_loop(0, n_iter, body, 0)


def _ret(zqk, zvg, lg, batch, seq, n_rh, tile):
    n_tiles = seq // tile
    per_b = seq // tile

    def rows(t):
        return jnp.where(t < n_tiles, t, 2 * n_tiles - 1 - t)

    def spec(seg):
        return pl.BlockSpec((1, tile, RET_HEAD_DIM), lambda b, h, t: (seg * n_rh + h, b * per_b + rows(t), 0))

    return pl.pallas_call(
        functools.partial(_ret_kernel, n_tiles=n_tiles, tile=tile),
        grid=(batch, n_rh, 2 * n_tiles),
        in_specs=[pl.BlockSpec(memory_space=pltpu.SMEM), spec(0), spec(1), spec(1), spec(2)],
        out_specs=pl.BlockSpec((1, tile, RET_HEAD_DIM),
                               lambda b, h, t: (h, b * per_b + jnp.where(t < n_tiles, n_tiles - 1, 2 * n_tiles - 1 - t), 0)),
        out_shape=jax.ShapeDtypeStruct((n_rh, batch * seq, RET_HEAD_DIM), BF16),
        scratch_shapes=[pltpu.VMEM((seq, RET_HEAD_DIM), F32),
                        pltpu.VMEM((RET_HEAD_DIM, RET_HEAD_DIM), F32),
                        pltpu.VMEM((RET_HEAD_DIM, RET_HEAD_DIM), F32),
                        pltpu.VMEM((RET_CHUNK, RET_CHUNK), F32)],
        compiler_params=_cparams(("arbitrary", "arbitrary", "arbitrary")),
        name="ret",
    )(lg, zqk, zqk, zvg, zvg)


def _outproj_kernel(a_ref, r_ref, x_ref, mod_ref, g0_ref, b0_ref, g1_ref, b1_ref, w_ref, x1_ref, hp_ref,
                    *, n_ah, n_rh, alpha):
    mix = jnp.concatenate([a_ref[h] for h in range(n_ah)] + [r_ref[h] for h in range(n_rh)], axis=-1)
    acc = jnp.dot(mix, w_ref[...], preferred_element_type=F32)
    xn = _ln(x_ref[...], g0_ref[...], b0_ref[...])
    y = alpha * xn + (1.0 + mod_ref[0, 2:3, :]) * acc
    x1 = _ln(y, g1_ref[...], b1_ref[...])
    x1_ref[...] = x1
    _store_token_tiles(hp_ref, _pack_bf16_pairs(x1 * (1.0 + mod_ref[0, 4:5, :]) + mod_ref[0, 3:4, :]))


def _outproj(attn, r, x2, mod3, g0, b0, g1, b1, w_out_b, seq, tm, alpha):
    t, d = x2.shape
    n_ah, n_rh = attn.shape[0], r.shape[0]
    per_b = seq // tm
    per = d // 2 // LANES
    row = lambda i: (i, 0)
    vec = pl.BlockSpec((1, d), lambda i: (0, 0))
    return pl.pallas_call(
        functools.partial(_outproj_kernel, n_ah=n_ah, n_rh=n_rh, alpha=alpha),
        grid=(t // tm,),
        in_specs=[pl.BlockSpec((n_ah, tm, ATTN_HEAD_DIM), lambda i: (0, i, 0)),
                  pl.BlockSpec((n_rh, tm, RET_HEAD_DIM), lambda i: (0, i, 0)),
                  pl.BlockSpec((tm, d), row),
                  pl.BlockSpec((1, 6, d), lambda i: (i // per_b, 0, 0)),
                  vec, vec, vec, vec,
                  pl.BlockSpec((d, d), lambda i: (0, 0))],
        out_specs=[pl.BlockSpec((tm, d), row), pl.BlockSpec((tm * per, LANES), row)],
        out_shape=[jax.ShapeDtypeStruct((t, d), F32), jax.ShapeDtypeStruct((t * per, LANES), U32)],
        compiler_params=_cparams(("arbitrary",)),
        name="outproj",
    )(attn, r, x2, mod3, g0, b0, g1, b1, w_out_b)


def _first_argmax(rows):
    best, idx = rows[0], jnp.zeros(rows[0].shape, I32)
    for e in range(1, len(rows)):
        better = rows[e] > best
        idx = jnp.where(better, e, idx)
        best = jnp.maximum(best, rows[e])
    return best, idx


def _router_kernel(h_ref, w_ref, b_ref, u_ref, eid_ref, wt_ref, rank_ref, cnt_ref, carry_ref, *, per):
    i = pl.program_id(0)

    @pl.when(i == 0)
    def _():
        carry_ref[...] = jnp.zeros_like(carry_ref)

    tm = eid_ref.shape[2]
    lt = lax.dot_general(w_ref[...], _load_token_tiles(h_ref, tm, per), (((1,), (1,)), ((), ())),
                         preferred_element_type=F32) + b_ref[...]
    grow = [lt[g:g + 1, :] for g in range(N_GROUPS)]
    gmax, gsel = _first_argmax(grow)
    gsum = grow[0] * 0.0
    for g in range(N_GROUPS):
        gsum = gsum + jnp.exp(grow[g] - gmax)
    pg = 1.0 / gsum
    srow = []
    for e in range(EXPERTS_PER_GROUP):
        r = lt[N_GROUPS + e:N_GROUPS + e + 1, :]
        for g in range(1, N_GROUPS):
            o = N_GROUPS + g * EXPERTS_PER_GROUP + e
            r = jnp.where(gsel == g, lt[o:o + 1, :], r)
        srow.append(r)
    v1, i1 = _first_argmax(srow)
    v2, i2 = _first_argmax([jnp.where(i1 == e, -jnp.inf, srow[e]) for e in range(EXPERTS_PER_GROUP)])
    e2 = jnp.exp(v2 - v1)
    den = 1.0 + e2
    wt_ref[0:1, :] = (1.0 / den) * pg
    wt_ref[1:2, :] = (e2 / den) * pg
    eid0 = gsel * EXPERTS_PER_GROUP + i1
    eid1 = gsel * EXPERTS_PER_GROUP + i2
    eid_ref[0, 0:1, :] = eid0
    eid_ref[0, 1:2, :] = eid1
    erow = lax.broadcasted_iota(I32, (N_EXPERTS, tm), 0)
    oh0 = (erow == eid0).astype(F32)
    oh1 = (erow == eid1).astype(F32)
    oh = oh0 + oh1
    incl = jnp.dot(oh.astype(BF16), u_ref[...], preferred_element_type=F32)
    before = carry_ref[:, 0:1] + incl - oh
    rank_ref[0, 0:1, :] = jnp.sum(oh0 * before, axis=0, keepdims=True).astype(I32)
    rank_ref[0, 1:2, :] = jnp.sum(oh1 * before, axis=0, keepdims=True).astype(I32)
    carry = carry_ref[...] + jnp.sum(oh, axis=1, keepdims=True)
    carry_ref[...] = carry
    cnt_ref[...] = carry


def _router(hp, wr, br, tm):
    d = wr.shape[1]
    per = d // 2 // LANES
    t = hp.shape[0] // per
    tri = (lax.broadcasted_iota(I32, (tm, tm), 0) <= lax.broadcasted_iota(I32, (tm, tm), 1)).astype(BF16)
    tile3 = pl.BlockSpec((1, 2, tm), lambda i: (i, 0, 0))
    return pl.pallas_call(
        functools.partial(_router_kernel, per=per),
        grid=(t // tm,),
        in_specs=[pl.BlockSpec((tm * per, LANES), lambda i: (i, 0)),
                  pl.BlockSpec((ROUTER_ROWS, d), lambda i: (0, 0)),
                  pl.BlockSpec((ROUTER_ROWS, 1), lambda i: (0, 0)),
                  pl.BlockSpec((tm, tm), lambda i: (0, 0))],
        out_specs=[tile3, pl.BlockSpec((2, tm), lambda i: (0, i)), tile3,
                   pl.BlockSpec((N_EXPERTS, LANES), lambda i: (0, 0))],
        out_shape=[jax.ShapeDtypeStruct((t // tm, 2, tm), I32), jax.ShapeDtypeStruct((2, t), F32),
                   jax.ShapeDtypeStruct((t // tm, 2, tm), I32), jax.ShapeDtypeStruct((N_EXPERTS, LANES), F32)],
        scratch_shapes=[pltpu.VMEM((N_EXPERTS, LANES), F32)],
        compiler_params=_cparams(("arbitrary",)),
        name="router",
    )(hp, wr, br, tri)


def _tile_copy(src, s, dst, d, per, sem):
    return pltpu.make_async_copy(src.at[pl.ds(pl.multiple_of(s * per, per), per), :],
                                 dst.at[pl.ds(pl.multiple_of(d * per, per), per), :], sem)


def _dispatch_kernel(ps_ref, lo_ref, hi_ref, eid_ref, rank_ref, h_ref, xb_hbm, stage, zero, sem, *, per):
    i = pl.program_id(0)
    n = pl.num_programs(0)
    slot = i % 2
    tm = eid_ref.shape[2]

    def wait_tile(s):
        pltpu.make_async_copy(xb_hbm.at[pl.ds(0, 2 * tm * per), :], xb_hbm.at[pl.ds(0, 2 * tm * per), :], sem.at[s]).wait()

    @pl.when(i >= 2)
    def _():
        wait_tile(slot)

    stage[slot] = h_ref[...]

    def body(r, carry):
        for k in range(2):
            dst = ps_ref[eid_ref[0, k, r]] + rank_ref[0, k, r]
            _tile_copy(stage.at[slot], r, xb_hbm, dst, per, sem.at[slot]).start(priority=k)
        return carry

    lax.fori_loop(0, tm, body, 0, unroll=8)

    @pl.when(i == n - 1)
    def _():
        @pl.when(n >= 2)
        def _():
            wait_tile(1 - slot)
        wait_tile(slot)
        zero[...] = jnp.zeros_like(zero)

        def fill(e, carry):
            def one(s, c):
                _tile_copy(zero, 0, xb_hbm, s, per, sem.at[2]).start()
                return c
            lax.fori_loop(lo_ref[e], hi_ref[e], one, 0)

            def one_wait(s, c):
                _tile_copy(zero, 0, xb_hbm, s, per, sem.at[2]).wait()
                return c
            lax.fori_loop(lo_ref[e], hi_ref[e], one_wait, 0)
            return carry

        lax.fori_loop(0, N_EXPERTS + 1, fill, 0)


def _dispatch(pstart, fill_lo, fill_hi, eid3, rank3, hp, n_slots, per):
    n, _, tm = eid3.shape
    smem3 = pl.BlockSpec((1, 2, tm), lambda i, *_: (i, 0, 0), memory_space=pltpu.SMEM)
    grid_spec = pltpu.PrefetchScalarGridSpec(
        num_scalar_prefetch=3,
        grid=(n,),
        in_specs=[smem3, smem3, pl.BlockSpec((tm * per, LANES), lambda i, *_: (i, 0))],
        out_specs=pl.BlockSpec(memory_space=pl.ANY),
        scratch_shapes=[pltpu.VMEM((2, tm * per, LANES), U32), pltpu.VMEM((per, LANES), U32),
                        pltpu.SemaphoreType.DMA((3,))])
    return pl.pallas_call(
        functools.partial(_dispatch_kernel, per=per),
        grid_spec=grid_spec,
        out_shape=jax.ShapeDtypeStruct((n_slots * per, LANES), U32),
        compiler_params=_cparams(("arbitrary",)),
        name="dispatch",
    )(pstart, fill_lo, fill_hi, eid3, rank3, hp)


def _expert_kernel(be_ref, first_ref, par_ref, nxt_ref, has_ref, x_ref, w1_hbm, w3_hbm, w2_hbm, y_ref,
                   wf1, wf3, wf2, w1b, w3b, w2b, sem, *, per):
    i = pl.program_id(0)

    def weight_copies(e, s):
        return [pltpu.make_async_copy(src.at[e], dst.at[s], sem.at[s])
                for src, dst in ((w1_hbm, wf1), (w3_hbm, wf3), (w2_hbm, wf2))]

    @pl.when(i == 0)
    def _():
        for cp in weight_copies(be_ref[0], 0):
            cp.start()

    @pl.when(first_ref[i] == 1)
    def _():
        s = par_ref[i]
        for cp in weight_copies(be_ref[i], s):
            cp.wait()

        @pl.when(has_ref[i] == 1)
        def _():
            for cp in weight_copies(nxt_ref[i], 1 - s):
                cp.start()

        w1b[...] = wf1[s].astype(BF16)
        w3b[...] = wf3[s].astype(BF16)
        w2b[...] = wf2[s].astype(BF16)

    x = _load_token_tiles(x_ref, MOE_BLOCK, per)
    a = jnp.dot(x, w1b[...], preferred_element_type=F32)
    b = jnp.dot(x, w3b[...], preferred_element_type=F32)
    mid = (jax.nn.silu(a) * b).astype(BF16)
    _store_token_tiles(y_ref, _pack_bf16_pairs(jnp.dot(mid, w2b[...], preferred_element_type=F32)))


def _experts(blk_e, xb, w1, w3, w2):
    d, ff = w1.shape[1], w1.shape[2]
    per = d // 2 // LANES
    nblk = xb.shape[0] // per // MOE_BLOCK
    rows = pl.BlockSpec((MOE_BLOCK * per, LANES), lambda i, *_: (i, 0))
    first = jnp.concatenate([jnp.ones((1,), I32), (blk_e[1:] != blk_e[:-1]).astype(I32)])
    parity = (jnp.cumsum(first) - 1) % 2
    nxt = jnp.min(jnp.where(blk_e[None, :] > blk_e[:, None], blk_e[None, :], N_EXPERTS), axis=1)
    has_next = (nxt < N_EXPERTS).astype(I32)
    nxt = jnp.minimum(nxt, N_EXPERTS - 1)
    grid_spec = pltpu.PrefetchScalarGridSpec(
        num_scalar_prefetch=5,
        grid=(nblk,),
        in_specs=[rows] + [pl.BlockSpec(memory_space=pl.ANY)] * 3,
        out_specs=rows,
        scratch_shapes=[pltpu.VMEM((2, d, ff), F32), pltpu.VMEM((2, d, ff), F32), pltpu.VMEM((2, ff, d), F32),
                        pltpu.VMEM((d, ff), BF16), pltpu.VMEM((d, ff), BF16), pltpu.VMEM((ff, d), BF16),
                        pltpu.SemaphoreType.DMA((2,))])
    return pl.pallas_call(
        functools.partial(_expert_kernel, per=per),
        grid_spec=grid_spec,
        out_shape=jax.ShapeDtypeStruct(xb.shape, U32),
        compiler_params=_cparams(("arbitrary",)),
        name="experts",
    )(blk_e, first, parity.astype(I32), nxt.astype(I32), has_next, xb, w1, w3, w2)


def _combine_kernel(ps_ref, eid_ref, rank_ref, eidn_ref, rankn_ref, y_hbm, wt_ref, x1_ref, mod_ref, g_ref, b_ref,
                    o_ref, ybuf, sem, *, alpha, per):
    i = pl.program_id(0)
    n = pl.num_programs(0)
    slot = i % 2
    tm = x1_ref.shape[0]

    def start(e_ref, r_ref, s):
        def body(r, carry):
            for k in range(2):
                src = ps_ref[e_ref[0, k, r]] + r_ref[0, k, r]
                _tile_copy(y_hbm, src, ybuf.at[s, k], r, per, sem.at[s]).start(priority=k)
            return carry
        lax.fori_loop(0, tm, body, 0, unroll=8)

    @pl.when(i == 0)
    def _():
        start(eid_ref, rank_ref, 0)

    @pl.when(i + 1 < n)
    def _():
        start(eidn_ref, rankn_ref, 1 - slot)

    for k in range(2):
        pltpu.make_async_copy(y_hbm.at[pl.ds(0, tm * per), :], ybuf.at[slot, k], sem.at[slot]).wait()
    wt = wt_ref[...]
    y0 = _load_token_tiles(ybuf.at[slot, 0], tm, per).astype(F32)
    y1 = _load_token_tiles(ybuf.at[slot, 1], tm, per).astype(F32)
    ffn = wt[:, 0:1] * y0 + wt[:, 1:2] * y1
    y = alpha * x1_ref[...] + (1.0 + mod_ref[0, 5:6, :]) * ffn
    o_ref[...] = _ln(y, g_ref[...], b_ref[...])


def _combine(pstart, eid3, rank3, yb, wt_t, x1, mod3, g2, b2, seq, alpha):
    t, d = x1.shape
    per = d // 2 // LANES
    n, _, tm = eid3.shape
    per_b = seq // tm
    cur = pl.BlockSpec((1, 2, tm), lambda i, ps: (i, 0, 0), memory_space=pltpu.SMEM)
    nxt = pl.BlockSpec((1, 2, tm), lambda i, ps: (jnp.minimum(i + 1, n - 1), 0, 0), memory_space=pltpu.SMEM)
    vec = pl.BlockSpec((1, d), lambda i, ps: (0, 0))
    grid_spec = pltpu.PrefetchScalarGridSpec(
        num_scalar_prefetch=1,
        grid=(n,),
        in_specs=[cur, cur, nxt, nxt,
                  pl.BlockSpec(memory_space=pl.ANY),
                  pl.BlockSpec((tm, 2), lambda i, ps: (i, 0)),
                  pl.BlockSpec((tm, d), lambda i, ps: (i, 0)),
                  pl.BlockSpec((1, 6, d), lambda i, ps: (i // per_b, 0, 0)),
                  vec, vec],
        out_specs=pl.BlockSpec((tm, d), lambda i, ps: (i, 0)),
        scratch_shapes=[pltpu.VMEM((2, 2, tm * per, LANES), U32), pltpu.SemaphoreType.DMA((2,))])
    return pl.pallas_call(
        functools.partial(_combine_kernel, alpha=alpha, per=per),
        grid_spec=grid_spec,
        out_shape=jax.ShapeDtypeStruct((t, d), F32),
        compiler_params=_cparams(("arbitrary",)),
        name="combine",
    )(pstart, eid3, rank3, eid3, rank3, yb, wt_t, x1, mod3, g2, b2)


def _pick_tile(n, want):
    tm = min(n, want)
    assert n % tm == 0
    return tm


def kernel(x, c, positions, ln0_g, ln0_b, w_ada, b_ada, w_in, w_out, ret_log_decay_f, ret_log_decay_b,
           ln1_g, ln1_b, w_group, b_group, w_sub, b_sub, w1, w3, w2, ln2_g, ln2_b):
    batch, seq, d = x.shape
    depth = w_ada.shape[0]
    t = batch * seq
    n_ah = d // 2 // ATTN_HEAD_DIM
    n_rh = d // 2 // RET_HEAD_DIM
    assert depth == 1 and d % (2 * RET_HEAD_DIM) == 0 and seq % (max(DILATIONS) * KW) == 0 and batch <= 8
    assert seq % (QT * CHAINS) == 0
    alpha = (2 * depth) ** 0.25

    inv_rope = ROPE_THETA ** (-jnp.arange(0, ROPE_DIM, 2, dtype=F32) / ROPE_DIM)
    inv_ret = RET_THETA ** (-jnp.linspace(0.0, 1.0, RET_HEAD_DIM // 2, dtype=F32))
    fa = jnp.zeros((1, LANES), F32).at[0, :ROPE_DIM].set(jnp.concatenate([inv_rope, inv_rope]))
    fr = inv_ret.reshape(1, LANES)
    posb = jnp.broadcast_to(positions.astype(F32).reshape(t, 1), (t, LANES))
    c8 = jnp.zeros((8, d), F32).at[:batch].set(c)
    row = lambda v: v.reshape(1, d)

    xs = x.reshape(t, d)
    mod = _ada(c8, w_ada[0], b_ada[0].reshape(1, -1))
    mod3 = mod[:batch].reshape(batch, 6, d)
    h, tabs = _prep(xs, posb, mod3, row(ln0_g), row(ln0_b), fa, fr, seq, _pick_tile(seq, 512))
    w_in_b = w_in[0].astype(BF16)
    tmp = _pick_tile(seq, 1024)
    z_aqk = _proj(_proj_attn_kernel, lambda j: j, 2, h, w_in_b, tabs, tmp, "proj_attn")
    z_rqk = _proj(_proj_ret_kernel, lambda j: 3 + j, 2, h, w_in_b, tabs, tmp, "proj_ret")
    z_pl = _proj(_proj_plain_kernel, lambda j: 2 + 3 * jnp.minimum(j, 1) + jnp.maximum(j - 1, 0), 3,
                 h, w_in_b, None, tmp, "proj_plain")
    attn = _attn(z_aqk, z_pl, batch, seq, n_ah)
    lg = jnp.stack([ret_log_decay_f[0], ret_log_decay_b[0]]).astype(F32)
    r = _ret(z_rqk, z_pl, lg, batch, seq, n_rh, _pick_tile(seq, 2048))
    x1, hp = _outproj(attn, r, xs, mod3, row(ln0_g), row(ln0_b), row(ln1_g[0]), row(ln1_b[0]),
                      w_out[0].astype(BF16), seq, _pick_tile(seq, 512), alpha)
    wr = jnp.zeros((ROUTER_ROWS, d), F32)
    wr = wr.at[:N_GROUPS].set(w_group[0].T)
    wr = wr.at[N_GROUPS:N_GROUPS + N_EXPERTS].set(w_sub[0].transpose(0, 2, 1).reshape(N_EXPERTS, d))
    br = jnp.zeros((ROUTER_ROWS, 1), F32)
    br = br.at[:N_GROUPS, 0].set(b_group[0]).at[N_GROUPS:N_GROUPS + N_EXPERTS, 0].set(b_sub[0].reshape(-1))
    eid3, wt, rank3, cnt = _router(hp, wr.astype(BF16), br, _pick_tile(seq, 512))
    counts = cnt[:, 0].astype(I32)
    padded = (counts + MOE_BLOCK - 1) // MOE_BLOCK * MOE_BLOCK
    pend = jnp.cumsum(padded)
    pstart = pend - padded
    n_slots = 2 * t + N_EXPERTS * MOE_BLOCK
    nblk = n_slots // MOE_BLOCK
    starts = jnp.arange(nblk, dtype=I32) * MOE_BLOCK
    blk_e = jnp.minimum(jnp.sum((pend[None, :] <= starts[:, None]).astype(I32), axis=1), N_EXPERTS - 1)
    fill_lo = jnp.concatenate([pstart + counts, pend[-1:]])
    fill_hi = jnp.concatenate([pend, jnp.full((1,), n_slots, I32)])
    xb = _dispatch(pstart, fill_lo, fill_hi, eid3, rank3, hp, n_slots, d // 2 // LANES)
    yb = _experts(blk_e, xb, w1[0], w3[0], w2[0])
    out = _combine(pstart, eid3, rank3, yb, wt.T, x1, mod3, row(ln2_g[0]), row(ln2_b[0]), seq, alpha)
    return out.reshape(batch, seq, d)
```

```python
import functools

import jax
import jax.numpy as jnp
from jax import lax
from jax.experimental import pallas as pl
from jax.experimental.pallas import tpu as pltpu

F32 = jnp.float32
BF16 = jnp.bfloat16
I32 = jnp.int32
U32 = jnp.uint32

LANES = 128
ATTN_HEAD_DIM = 128
RET_HEAD_DIM = 256
DILATIONS = (1, 4, 16)
BAND = 64
QT = 128
KW = QT + 2 * BAND
CHAINS = 16
PERM_ROWS = 256
PERM_UNROLL = 4
ROPE_THETA = 500000.0
ROPE_DIM = ATTN_HEAD_DIM // 4
TOK_PER_ROW = LANES // (ROPE_DIM // 2)
RET_THETA = 10000.0
RET_CHUNK = 128
RET_UNROLL = 4
N_GROUPS = 4
EXPERTS_PER_GROUP = 8
N_EXPERTS = N_GROUPS * EXPERTS_PER_GROUP
MOE_BLOCK = 256
LN_EPS = 1e-5
MASK_VALUE = -1e30
ROUTER_ROWS = 48
HI_MASK = 0xFFFF0000
VMEM_LIMIT = 56 * 1024 * 1024


def _cparams(sem, vmem=VMEM_LIMIT):
    return pltpu.CompilerParams(dimension_semantics=sem, vmem_limit_bytes=vmem)


def _ln(x, g, b):
    mu = jnp.mean(x, axis=-1, keepdims=True)
    xc = x - mu
    var = jnp.mean(xc * xc, axis=-1, keepdims=True)
    return xc * lax.rsqrt(var + LN_EPS) * g + b


def _pack_bf16_pairs(h):
    bits = lax.bitcast_convert_type(h.astype(BF16).astype(F32), U32)
    n = h.shape[1] // 2
    return (bits[:, :n] >> 16) | (bits[:, n:] & jnp.uint32(HI_MASK))


def _store_token_tiles(ref, packed):
    m, n = packed.shape
    per = n // LANES
    for s in range(per):
        ref[pl.ds(s, m, stride=per), :] = packed[:, s * LANES:(s + 1) * LANES]


def _load_token_tiles(ref, m, per):
    slabs = [ref[pl.ds(s, m, stride=per), :] for s in range(per)]
    lo = [lax.bitcast_convert_type(p << 16, F32) for p in slabs]
    hi = [lax.bitcast_convert_type(p & jnp.uint32(HI_MASK), F32) for p in slabs]
    return jnp.concatenate(lo + hi, axis=-1).astype(BF16)


def _ada_kernel(c_ref, w_ref, b_ref, o_ref):
    cs = jax.nn.silu(c_ref[...])
    o_ref[...] = jnp.dot(cs.astype(BF16), w_ref[...].astype(BF16), preferred_element_type=F32) + b_ref[...]


def _ada(c8, w_ada, b_ada):
    d, n = w_ada.shape
    tn = min(n, 512)
    return pl.pallas_call(
        _ada_kernel,
        grid=(n // tn,),
        in_specs=[pl.BlockSpec((8, d), lambda j: (0, 0)),
                  pl.BlockSpec((d, tn), lambda j: (0, j)),
                  pl.BlockSpec((1, tn), lambda j: (0, j))],
        out_specs=pl.BlockSpec((8, tn), lambda j: (0, j)),
        out_shape=jax.ShapeDtypeStruct((8, n), F32),
        compiler_params=_cparams(("arbitrary",)),
        name="ada",
    )(c8, w_ada, b_ada)


def _prep_kernel(x_ref, pos_ref, posc_ref, mod_ref, g_ref, b_ref, fac_ref, fr_ref, h_ref, tab_ref):
    xn = _ln(x_ref[...], g_ref[...], b_ref[...])
    h_ref[...] = (xn * (1.0 + mod_ref[0, 1:2, :]) + mod_ref[0, 0:1, :]).astype(BF16)
    half = ROPE_DIM // 2
    angc = posc_ref[...] * fac_ref[...]
    cc, sc = jnp.cos(angc), jnp.sin(angc)
    rows_c = angc.shape[0]
    lane = lax.broadcasted_iota(I32, angc.shape, 1)
    first, second = lane < half, (lane >= half) & (lane < 2 * half)
    for ts in range(TOK_PER_ROW):
        shift = (LANES - ts * half) % LANES
        c = pltpu.roll(cc, shift, 1) if shift else cc
        s = pltpu.roll(sc, shift, 1) if shift else sc
        rows = pl.ds(ts, rows_c, stride=TOK_PER_ROW)
        tab_ref.at[0][rows, :] = jnp.where(first, c, jnp.where(second, pltpu.roll(c, half, 1), 1.0))
        tab_ref.at[1][rows, :] = jnp.where(first, -s, 0.0)
        tab_ref.at[2][rows, :] = jnp.where(second, pltpu.roll(s, half, 1), 0.0)
    angr = pos_ref[...] * fr_ref[...]
    tab_ref[3] = jnp.cos(angr)
    tab_ref[4] = jnp.sin(angr)


def _prep(x2, posb, posc, mod3, ln0_g, ln0_b, fac, fr, seq, tm):
    t, d = x2.shape
    per_b = seq // tm
    vec = lambda n: pl.BlockSpec((1, n), lambda i: (0, 0))
    return pl.pallas_call(
        _prep_kernel,
        grid=(t // tm,),
        in_specs=[pl.BlockSpec((tm, d), lambda i: (i, 0)),
                  pl.BlockSpec((tm, LANES), lambda i: (i, 0)),
                  pl.BlockSpec((tm // TOK_PER_ROW, LANES), lambda i: (i, 0)),
                  pl.BlockSpec((1, 6, d), lambda i: (i // per_b, 0, 0)),
                  vec(d), vec(d), vec(LANES), vec(LANES)],
        out_specs=[pl.BlockSpec((tm, d), lambda i: (i, 0)), pl.BlockSpec((5, tm, LANES), lambda i: (0, i, 0))],
        out_shape=[jax.ShapeDtypeStruct((t, d), BF16), jax.ShapeDtypeStruct((5, t, LANES), F32)],
        compiler_params=_cparams(("arbitrary",)),
        name="prep",
    )(x2, posb, posc, mod3, ln0_g, ln0_b, fac, fr)


def _proj_chunks(h_ref, w_ref, n_sub):
    for hc in range(n_sub):
        yield hc, jnp.dot(h_ref[...], w_ref[:, hc * 2 * LANES:(hc + 1) * 2 * LANES], preferred_element_type=F32)


def _proj_attn_kernel(h_ref, w_ref, tab_ref, z_ref, *, n_sub):
    scale = jnp.where(pl.program_id(0) == 0, ATTN_HEAD_DIM ** -0.5, 1.0).astype(F32)
    ca, s1, s2 = tab_ref[0] * scale, tab_ref[1] * scale, tab_ref[2] * scale
    half = ROPE_DIM // 2
    for hc, res in _proj_chunks(h_ref, w_ref, n_sub):
        for lanes in (slice(0, LANES), slice(LANES, 2 * LANES)):
            x = res[:, lanes]
            r = x * ca + pltpu.roll(x, LANES - half, 1) * s1 + pltpu.roll(x, half, 1) * s2
            z_ref[hc, :, lanes] = r.astype(BF16)


def _proj_ret_kernel(h_ref, w_ref, tab_ref, z_ref, *, n_sub):
    scale = jnp.where(pl.program_id(0) == 1, RET_HEAD_DIM ** -0.5, 1.0).astype(F32)
    cr, sr = tab_ref[3] * scale, tab_ref[4] * scale
    for hc, res in _proj_chunks(h_ref, w_ref, n_sub):
        x1, x2 = res[:, :LANES], res[:, LANES:]
        z_ref[hc, :, :LANES] = (x1 * cr - x2 * sr).astype(BF16)
        z_ref[hc, :, LANES:] = (x2 * cr + x1 * sr).astype(BF16)


def _proj_plain_kernel(h_ref, w_ref, z_ref, *, n_sub):
    for hc, res in _proj_chunks(h_ref, w_ref, n_sub):
        z_ref[hc] = res.astype(BF16)


def _proj(body, col_of, n_seg, h, w_in_b, tabs, tm, name):
    t, d = h.shape
    tn = d // 2
    n_sub = tn // (2 * LANES)
    in_specs = [pl.BlockSpec((tm, d), lambda j, i: (i, 0)),
                pl.BlockSpec((d, tn), lambda j, i: (0, col_of(j)))]
    args = [h, w_in_b]
    if tabs is not None:
        in_specs.append(pl.BlockSpec((5, tm, LANES), lambda j, i: (0, i, 0)))
        args.append(tabs)
    return pl.pallas_call(
        functools.partial(body, n_sub=n_sub),
        grid=(n_seg, t // tm),
        in_specs=in_specs,
        out_specs=pl.BlockSpec((n_sub, tm, 2 * LANES), lambda j, i: (j, i, 0)),
        out_shape=jax.ShapeDtypeStruct((n_seg * n_sub, t, 2 * LANES), BF16),
        compiler_params=_cparams(("arbitrary", "arbitrary")),
        name=name,
    )(*args)


def _attn_tile(q, k, v, bias):
    s = lax.dot_general(q, k, (((1,), (1,)), ((), ())), preferred_element_type=F32) + bias
    m = jnp.max(jnp.maximum(s[:, :LANES], s[:, LANES:]), axis=-1, keepdims=True)
    p = jnp.exp(s - m)
    den = jnp.sum(p[:, :LANES] + p[:, LANES:], axis=-1, keepdims=True)
    acc = jnp.dot(p.astype(BF16), v, preferred_element_type=F32)
    return acc, jnp.broadcast_to(m, acc.shape), jnp.broadcast_to(den, acc.shape)


def _attn_kernel(q_ref, k_ref, v_ref, bias_ref, p4_ref, p16_ref, o_ref,
                 acc_ref, m_ref, d_ref, qp_ref, kp_ref, vp_ref, *, seq):
    n_items = seq // QT

    def window(n, length):
        qs = n * QT
        ks = jnp.clip(qs - BAND, 0, length - KW)
        return qs, ks, bias_ref[lax.div(qs - ks, BAND)]

    def perm_body(it, carry):
        for u in range(PERM_UNROLL):
            blk = it * PERM_UNROLL + u
            r0 = pl.multiple_of(blk * PERM_ROWS, PERM_ROWS)
            qk = jnp.concatenate([q_ref[0, 0, pl.ds(r0, PERM_ROWS), :], k_ref[0, 0, pl.ds(r0, PERM_ROWS), :]], axis=-1)
            v = v_ref[0, 0, pl.ds(r0, PERM_ROWS), :]
            for slot, (dil, p_ref) in enumerate(((4, p4_ref), (16, p16_ref))):
                length, w = seq // dil, PERM_ROWS // dil
                qkp = jnp.dot(p_ref[...], qk, preferred_element_type=F32).astype(BF16)
                vp = jnp.dot(p_ref[...], v, preferred_element_type=F32).astype(BF16)
                for r in range(dil):
                    rows = pl.ds(pl.multiple_of(r * length + blk * w, w), w)
                    qp_ref[slot, rows, :] = qkp[r * w:(r + 1) * w, :ATTN_HEAD_DIM]
                    kp_ref[slot, rows, :] = qkp[r * w:(r + 1) * w, ATTN_HEAD_DIM:]
                    vp_ref[slot, rows, :] = vp[r * w:(r + 1) * w, :]
        return carry

    lax.fori_loop(0, seq // PERM_ROWS // PERM_UNROLL, perm_body, 0)

    def dilated_tile(item, dil, slot):
        length = seq // dil
        n, r = lax.div(item, dil), lax.rem(item, dil)
        qs, ks, bias = window(n, length)
        qrow = pl.multiple_of(r * length + qs, QT)
        krow = pl.multiple_of(r * length + ks, BAND)
        out = _attn_tile(qp_ref[slot, pl.ds(qrow, QT), :], kp_ref[slot, pl.ds(krow, KW), :],
                         vp_ref[slot, pl.ds(krow, KW), :], bias)
        return out, pl.ds(qs * dil + r, QT, stride=dil)

    def body16(it, carry):
        for g in range(CHAINS):
            (acc, m, den), rows = dilated_tile(it * CHAINS + g, 16, 1)
            acc_ref[rows, :] = acc
            m_ref[rows, :] = m
            d_ref[rows, :] = den
        return carry

    lax.fori_loop(0, n_items // CHAINS, body16, 0)

    def body4(it, carry):
        for g in range(CHAINS):
            (acc, m, den), rows = dilated_tile(it * CHAINS + g, 4, 0)
            m0 = m_ref[rows, :]
            mn = jnp.maximum(m0, m)
            a, b = jnp.exp(m0 - mn), jnp.exp(m - mn)
            acc_ref[rows, :] = a * acc_ref[rows, :] + b * acc
            d_ref[rows, :] = a * d_ref[rows, :] + b * den
            m_ref[rows, :] = mn
        return carry

    lax.fori_loop(0, n_items // CHAINS, body4, 0)

    def body1(it, carry):
        for g in range(CHAINS):
            qs, ks, bias = window(it * CHAINS + g, seq)
            qs, ks = pl.multiple_of(qs, QT), pl.multiple_of(ks, BAND)
            acc, m, den = _attn_tile(q_ref[0, 0, pl.ds(qs, QT), :], k_ref[0, 0, pl.ds(ks, KW), :],
                                     v_ref[0, 0, pl.ds(ks, KW), :], bias)
            rows = pl.ds(qs, QT)
            m0 = m_ref[rows, :]
            mn = jnp.maximum(m0, m)
            a, b = jnp.exp(m0 - mn), jnp.exp(m - mn)
            num = a * acc_ref[rows, :] + b * acc
            o_ref[0, 0, rows, :] = (num / (a * d_ref[rows, :] + b * den)).astype(BF16)
        return carry

    lax.fori_loop(0, n_items // CHAINS, body1, 0)


def _perm_matrix(dil):
    w = PERM_ROWS // dil
    i = jnp.arange(PERM_ROWS)
    src = (i % w) * dil + i // w
    return (src[:, None] == jnp.arange(PERM_ROWS)[None, :]).astype(BF16)


def _attn(zqk, zv_, batch, seq, n_ah):
    n_sub = n_ah // 2
    zqk = zqk.reshape(zqk.shape[0], batch, seq, 2 * LANES)
    zv_ = zv_.reshape(zv_.shape[0], batch, seq, 2 * LANES)
    blk = (1, 1, seq, ATTN_HEAD_DIM)
    i = jnp.arange(QT)[:, None]
    jj = jnp.arange(KW)[None, :]
    bias = jnp.stack([jnp.where(jnp.abs(i + o * BAND - jj) <= BAND, 0.0, MASK_VALUE) for o in range(3)]).astype(F32)
    const = lambda shape: pl.BlockSpec(shape, lambda b, h: (0,) * len(shape))
    o = pl.pallas_call(
        functools.partial(_attn_kernel, seq=seq),
        grid=(batch, n_ah),
        in_specs=[pl.BlockSpec(blk, lambda b, h: (h // 2, b, 0, h % 2)),
                  pl.BlockSpec(blk, lambda b, h: (n_sub + h // 2, b, 0, h % 2)),
                  pl.BlockSpec(blk, lambda b, h: (h // 2, b, 0, h % 2)),
                  const((3, QT, KW)), const((PERM_ROWS, PERM_ROWS)), const((PERM_ROWS, PERM_ROWS))],
        out_specs=pl.BlockSpec(blk, lambda b, h: (h, b, 0, 0)),
        out_shape=jax.ShapeDtypeStruct((n_ah, batch, seq, ATTN_HEAD_DIM), BF16),
        scratch_shapes=[pltpu.VMEM((seq, LANES), F32)] * 3
        + [pltpu.VMEM((2, seq, ATTN_HEAD_DIM), BF16)] * 3,
        compiler_params=_cparams(("arbitrary", "arbitrary")),
        name="attn",
    )(zqk, zqk, zv_, bias, _perm_matrix(4), _perm_matrix(16))
    return o.reshape(n_ah, batch * seq, ATTN_HEAD_DIM)


def _ret_kernel(lg_ref, q_ref, k_ref, v_ref, g_ref, o_ref, y_ref, sf_ref, sb_ref, dmat_ref, *, n_tiles, tile):
    h = pl.program_id(1)
    t = pl.program_id(2)
    c = RET_CHUNK
    lgf = lg_ref[0, h]
    lgb = lg_ref[1, h]
    col = lax.broadcasted_iota(I32, (c, 1), 0).astype(F32)
    row = lax.broadcasted_iota(I32, (1, c), 1).astype(F32)
    n_iter = tile // c // RET_UNROLL

    @pl.when(t == 0)
    def _():
        sf_ref[...] = jnp.zeros_like(sf_ref)
        sb_ref[...] = jnp.zeros_like(sb_ref)
        diff = (lax.broadcasted_iota(I32, (c, c), 0) - lax.broadcasted_iota(I32, (c, c), 1)).astype(F32)
        dmat_ref[...] = jnp.where(diff >= 0, jnp.exp(lgf * jnp.maximum(diff, 0.0)), 0.0) \
            + jnp.where(diff < 0, jnp.exp(lgb * jnp.maximum(-diff, 0.0)), 0.0)

    def chunk(r0):
        rows = pl.ds(pl.multiple_of(r0, c), c)
        return rows, q_ref[0, rows, :], k_ref[0, rows, :], v_ref[0, rows, :]

    @pl.when(t < n_tiles)
    def _():
        xi = jnp.exp(lgf * (col + 1.0))
        zeta = jnp.exp(lgf * (c - 1.0 - row))
        cdec = jnp.exp(lgf * jnp.full((1, 1), float(c), F32))
        base = t * tile

        def body(it, carry):
            state = sf_ref[...]
            for u in range(RET_UNROLL):
                r0 = (it * RET_UNROLL + u) * c
                _, q, k, v = chunk(r0)
                s = lax.dot_general(q, k, (((1,), (1,)), ((), ())), preferred_element_type=F32) * dmat_ref[...]
                y = jnp.dot(s.astype(BF16), v, preferred_element_type=F32)
                y = y + jnp.dot(q, state.astype(BF16), preferred_element_type=F32) * xi
                kt = (k.astype(F32).T * zeta).astype(BF16)
                state = state * cdec + jnp.dot(kt, v, preferred_element_type=F32)
                y_ref[pl.ds(pl.multiple_of(base + r0, c), c), :] = y
            sf_ref[...] = state
            return carry

        lax.fori_loop(0, n_iter, body, 0)

    @pl.when(t >= n_tiles)
    def _():
        xi = jnp.exp(lgb * (c - col))
        zeta = jnp.exp(lgb * row)
        cdec = jnp.exp(lgb * jnp.full((1, 1), float(c), F32))
        base = (2 * n_tiles - 1 - t) * tile

        def body(it, carry):
            state = sb_ref[...]
            for u in range(RET_UNROLL):
                r0 = (tile // c - 1 - (it * RET_UNROLL + u)) * c
                rows, q, k, v = chunk(r0)
                y = y_ref[pl.ds(pl.multiple_of(base + r0, c), c), :]
                y = y + jnp.dot(q, state.astype(BF16), preferred_element_type=F32) * xi
                kt = (k.astype(F32).T * zeta).astype(BF16)
                state = state * cdec + jnp.dot(kt, v, preferred_element_type=F32)
                mu = jnp.mean(y, axis=-1, keepdims=True)
                yc = y - mu
                var = jnp.mean(yc * yc, axis=-1, keepdims=True)
                yn = yc * lax.rsqrt(var + LN_EPS)
                gate = jax.nn.silu(g_ref[0, rows, :].astype(F32))
                o_ref[0, rows, :] = (gate * yn).astype(BF16)
            sb_ref[...] = state
            return carry

        lax.fori_loop(0, n_iter, body, 0)


def _ret(zqk, zvg, lg, batch, seq, n_rh, tile):
    n_tiles = seq // tile
    per_b = seq // tile

    def rows(t):
        return jnp.where(t < n_tiles, t, 2 * n_tiles - 1 - t)

    def spec(seg):
        return pl.BlockSpec((1, tile, RET_HEAD_DIM), lambda b, h, t: (seg * n_rh + h, b * per_b + rows(t), 0))

    return pl.pallas_call(
        functools.partial(_ret_kernel, n_tiles=n_tiles, tile=tile),
        grid=(batch, n_rh, 2 * n_tiles),
        in_specs=[pl.BlockSpec(memory_space=pltpu.SMEM), spec(0), spec(1), spec(1), spec(2)],
        out_specs=pl.BlockSpec((1, tile, RET_HEAD_DIM),
                               lambda b, h, t: (h, b * per_b + jnp.where(t < n_tiles, n_tiles - 1, 2 * n_tiles - 1 - t), 0)),
        out_shape=jax.ShapeDtypeStruct((n_rh, batch * seq, RET_HEAD_DIM), BF16),
        scratch_shapes=[pltpu.VMEM((seq, RET_HEAD_DIM), F32),
                        pltpu.VMEM((RET_HEAD_DIM, RET_HEAD_DIM), F32),
                        pltpu.VMEM((RET_HEAD_DIM, RET_HEAD_DIM), F32),
                        pltpu.VMEM((RET_CHUNK, RET_CHUNK), F32)],
        compiler_params=_cparams(("arbitrary", "arbitrary", "arbitrary")),
        name="ret",
    )(lg, zqk, zqk, zvg, zvg)


def _outproj_kernel(a_ref, r_ref, x_ref, mod_ref, g0_ref, b0_ref, g1_ref, b1_ref, w_ref, x1_ref, hp_ref,
                    *, n_ah, n_rh, alpha):
    mix = jnp.concatenate([a_ref[h] for h in range(n_ah)] + [r_ref[h] for h in range(n_rh)], axis=-1)
    acc = jnp.dot(mix, w_ref[...], preferred_element_type=F32)
    xn = _ln(x_ref[...], g0_ref[...], b0_ref[...])
    y = alpha * xn + (1.0 + mod_ref[0, 2:3, :]) * acc
    x1 = _ln(y, g1_ref[...], b1_ref[...])
    x1_ref[...] = x1
    _store_token_tiles(hp_ref, _pack_bf16_pairs(x1 * (1.0 + mod_ref[0, 4:5, :]) + mod_ref[0, 3:4, :]))


def _outproj(attn, r, x2, mod3, g0, b0, g1, b1, w_out_b, seq, tm, alpha):
    t, d = x2.shape
    n_ah, n_rh = attn.shape[0], r.shape[0]
    per_b = seq // tm
    per = d // 2 // LANES
    row = lambda i: (i, 0)
    vec = pl.BlockSpec((1, d), lambda i: (0, 0))
    return pl.pallas_call(
        functools.partial(_outproj_kernel, n_ah=n_ah, n_rh=n_rh, alpha=alpha),
        grid=(t // tm,),
        in_specs=[pl.BlockSpec((n_ah, tm, ATTN_HEAD_DIM), lambda i: (0, i, 0)),
                  pl.BlockSpec((n_rh, tm, RET_HEAD_DIM), lambda i: (0, i, 0)),
                  pl.BlockSpec((tm, d), row),
                  pl.BlockSpec((1, 6, d), lambda i: (i // per_b, 0, 0)),
                  vec, vec, vec, vec,
                  pl.BlockSpec((d, d), lambda i: (0, 0))],
        out_specs=[pl.BlockSpec((tm, d), row), pl.BlockSpec((tm * per, LANES), row)],
        out_shape=[jax.ShapeDtypeStruct((t, d), F32), jax.ShapeDtypeStruct((t * per, LANES), U32)],
        compiler_params=_cparams(("arbitrary",)),
        name="outproj",
    )(attn, r, x2, mod3, g0, b0, g1, b1, w_out_b)


def _first_argmax(rows):
    best, idx = rows[0], jnp.zeros(rows[0].shape, I32)
    for e in range(1, len(rows)):
        better = rows[e] > best
        idx = jnp.where(better, e, idx)
        best = jnp.maximum(best, rows[e])
    return best, idx


def _router_kernel(h_ref, w_ref, b_ref, u_ref, eid_ref, wt_ref, rank_ref, cnt_ref, carry_ref, *, per):
    i = pl.program_id(0)

    @pl.when(i == 0)
    def _():
        carry_ref[...] = jnp.zeros_like(carry_ref)

    tm = eid_ref.shape[2]
    lt = lax.dot_general(w_ref[...], _load_token_tiles(h_ref, tm, per), (((1,), (1,)), ((), ())),
                         preferred_element_type=F32) + b_ref[...]
    grow = [lt[g:g + 1, :] for g in range(N_GROUPS)]
    gmax, gsel = _first_argmax(grow)
    gsum = grow[0] * 0.0
    for g in range(N_GROUPS):
        gsum = gsum + jnp.exp(grow[g] - gmax)
    pg = 1.0 / gsum
    srow = []
    for e in range(EXPERTS_PER_GROUP):
        r = lt[N_GROUPS + e:N_GROUPS + e + 1, :]
        for g in range(1, N_GROUPS):
            o = N_GROUPS + g * EXPERTS_PER_GROUP + e
            r = jnp.where(gsel == g, lt[o:o + 1, :], r)
        srow.append(r)
    v1, i1 = _first_argmax(srow)
    v2, i2 = _first_argmax([jnp.where(i1 == e, -jnp.inf, srow[e]) for e in range(EXPERTS_PER_GROUP)])
    e2 = jnp.exp(v2 - v1)
    den = 1.0 + e2
    wt_ref[0:1, :] = (1.0 / den) * pg
    wt_ref[1:2, :] = (e2 / den) * pg
    eid0 = gsel * EXPERTS_PER_GROUP + i1
    eid1 = gsel * EXPERTS_PER_GROUP + i2
    eid_ref[0, 0:1, :] = eid0
    eid_ref[0, 1:2, :] = eid1
    erow = lax.broadcasted_iota(I32, (N_EXPERTS, tm), 0)
    oh0 = (erow == eid0).astype(F32)
    oh1 = (erow == eid1).astype(F32)
    oh = oh0 + oh1
    incl = jnp.dot(oh.astype(BF16), u_ref[...], preferred_element_type=F32)
    before = carry_ref[:, 0:1] + incl - oh
    rank_ref[0, 0:1, :] = jnp.sum(oh0 * before, axis=0, keepdims=True).astype(I32)
    rank_ref[0, 1:2, :] = jnp.sum(oh1 * before, axis=0, keepdims=True).astype(I32)
    carry = carry_ref[...] + jnp.sum(oh, axis=1, keepdims=True)
    carry_ref[...] = carry
    cnt_ref[...] = carry


def _router(hp, wr, br, tm):
    d = wr.shape[1]
    per = d // 2 // LANES
    t = hp.shape[0] // per
    tri = (lax.broadcasted_iota(I32, (tm, tm), 0) <= lax.broadcasted_iota(I32, (tm, tm), 1)).astype(BF16)
    tile3 = pl.BlockSpec((1, 2, tm), lambda i: (i, 0, 0))
    return pl.pallas_call(
        functools.partial(_router_kernel, per=per),
        grid=(t // tm,),
        in_specs=[pl.BlockSpec((tm * per, LANES), lambda i: (i, 0)),
                  pl.BlockSpec((ROUTER_ROWS, d), lambda i: (0, 0)),
                  pl.BlockSpec((ROUTER_ROWS, 1), lambda i: (0, 0)),
                  pl.BlockSpec((tm, tm), lambda i: (0, 0))],
        out_specs=[tile3, pl.BlockSpec((2, tm), lambda i: (0, i)), tile3,
                   pl.BlockSpec((N_EXPERTS, LANES), lambda i: (0, 0))],
        out_shape=[jax.ShapeDtypeStruct((t // tm, 2, tm), I32), jax.ShapeDtypeStruct((2, t), F32),
                   jax.ShapeDtypeStruct((t // tm, 2, tm), I32), jax.ShapeDtypeStruct((N_EXPERTS, LANES), F32)],
        scratch_shapes=[pltpu.VMEM((N_EXPERTS, LANES), F32)],
        compiler_params=_cparams(("arbitrary",)),
        name="router",
    )(hp, wr, br, tri)


def _tile_copy(src, s_row, dst, d_row, per, sem):
    return pltpu.make_async_copy(src.at[pl.ds(pl.multiple_of(s_row, per), per), :],
                                 dst.at[pl.ds(pl.multiple_of(d_row, per), per), :], sem)


def _dispatch_kernel(lo_ref, hi_ref, dst_ref, h_ref, xb_hbm, stage, zero, sem, *, per):
    i = pl.program_id(0)
    n = pl.num_programs(0)
    slot = i % 2
    tm = dst_ref.shape[2]

    def wait_tile(s):
        pltpu.make_async_copy(xb_hbm.at[pl.ds(0, 2 * tm * per), :], xb_hbm.at[pl.ds(0, 2 * tm * per), :], sem.at[s]).wait()

    @pl.when(i >= 2)
    def _():
        wait_tile(slot)

    stage[slot] = h_ref[...]

    def body(r, carry):
        for k in range(2):
            _tile_copy(stage.at[slot], r * per, xb_hbm, dst_ref[0, k, r], per, sem.at[slot]).start(priority=k)
        return carry

    lax.fori_loop(0, tm, body, 0, unroll=8)

    @pl.when(i == n - 1)
    def _():
        @pl.when(n >= 2)
        def _():
            wait_tile(1 - slot)
        wait_tile(slot)
        zero[...] = jnp.zeros_like(zero)

        def fill(e, carry):
            def one(s, c):
                _tile_copy(zero, 0, xb_hbm, s * per, per, sem.at[2]).start()
                return c
            lax.fori_loop(lo_ref[e], hi_ref[e], one, 0)

            def one_wait(s, c):
                _tile_copy(zero, 0, xb_hbm, s * per, per, sem.at[2]).wait()
                return c
            lax.fori_loop(lo_ref[e], hi_ref[e], one_wait, 0)
            return carry

        lax.fori_loop(0, N_EXPERTS + 1, fill, 0)


def _dispatch(fill_lo, fill_hi, drow3, hp, n_slots, per):
    n, _, tm = drow3.shape
    smem3 = pl.BlockSpec((1, 2, tm), lambda i, *_: (i, 0, 0), memory_space=pltpu.SMEM)
    grid_spec = pltpu.PrefetchScalarGridSpec(
        num_scalar_prefetch=2,
        grid=(n,),
        in_specs=[smem3, pl.BlockSpec((tm * per, LANES), lambda i, *_: (i, 0))],
        out_specs=pl.BlockSpec(memory_space=pl.ANY),
        scratch_shapes=[pltpu.VMEM((2, tm * per, LANES), U32), pltpu.VMEM((per, LANES), U32),
                        pltpu.SemaphoreType.DMA((3,))])
    return pl.pallas_call(
        functools.partial(_dispatch_kernel, per=per),
        grid_spec=grid_spec,
        out_shape=jax.ShapeDtypeStruct((n_slots * per, LANES), U32),
        compiler_params=_cparams(("arbitrary",)),
        name="dispatch",
    )(fill_lo, fill_hi, drow3, hp)


def _expert_kernel(be_ref, first_ref, par_ref, nxt_ref, has_ref, x_ref, w1_hbm, w3_hbm, w2_hbm, y_ref,
                   wf1, wf3, wf2, w1b, w3b, w2b, sem, *, per):
    i = pl.program_id(0)

    def weight_copies(e, s):
        return [pltpu.make_async_copy(src.at[e], dst.at[s], sem.at[s])
                for src, dst in ((w1_hbm, wf1), (w3_hbm, wf3), (w2_hbm, wf2))]

    @pl.when(i == 0)
    def _():
        for cp in weight_copies(be_ref[0], 0):
            cp.start()

    @pl.when(first_ref[i] == 1)
    def _():
        s = par_ref[i]
        for cp in weight_copies(be_ref[i], s):
            cp.wait()

        @pl.when(has_ref[i] == 1)
        def _():
            for cp in weight_copies(nxt_ref[i], 1 - s):
                cp.start()

        w1b[...] = wf1[s].astype(BF16)
        w3b[...] = wf3[s].astype(BF16)
        w2b[...] = wf2[s].astype(BF16)

    x = _load_token_tiles(x_ref, MOE_BLOCK, per)
    a = jnp.dot(x, w1b[...], preferred_element_type=F32)
    b = jnp.dot(x, w3b[...], preferred_element_type=F32)
    mid = (jax.nn.silu(a) * b).astype(BF16)
    _store_token_tiles(y_ref, _pack_bf16_pairs(jnp.dot(mid, w2b[...], preferred_element_type=F32)))


def _experts(blk_e, xb, w1, w3, w2):
    d, ff = w1.shape[1], w1.shape[2]
    per = d // 2 // LANES
    nblk = xb.shape[0] // per // MOE_BLOCK
    rows = pl.BlockSpec((MOE_BLOCK * per, LANES), lambda i, *_: (i, 0))
    first = jnp.concatenate([jnp.ones((1,), I32), (blk_e[1:] != blk_e[:-1]).astype(I32)])
    parity = (jnp.cumsum(first) - 1) % 2
    nxt = jnp.min(jnp.where(blk_e[None, :] > blk_e[:, None], blk_e[None, :], N_EXPERTS), axis=1)
    has_next = (nxt < N_EXPERTS).astype(I32)
    nxt = jnp.minimum(nxt, N_EXPERTS - 1)
    grid_spec = pltpu.PrefetchScalarGridSpec(
        num_scalar_prefetch=5,
        grid=(nblk,),
        in_specs=[rows] + [pl.BlockSpec(memory_space=pl.ANY)] * 3,
        out_specs=rows,
        scratch_shapes=[pltpu.VMEM((2, d, ff), F32), pltpu.VMEM((2, d, ff), F32), pltpu.VMEM((2, ff, d), F32),
                        pltpu.VMEM((d, ff), BF16), pltpu.VMEM((d, ff), BF16), pltpu.VMEM((ff, d), BF16),
                        pltpu.SemaphoreType.DMA((2,))])
    return pl.pallas_call(
        functools.partial(_expert_kernel, per=per),
        grid_spec=grid_spec,
        out_shape=jax.ShapeDtypeStruct(xb.shape, U32),
        compiler_params=_cparams(("arbitrary",)),
        name="experts",
    )(blk_e, first, parity.astype(I32), nxt.astype(I32), has_next, xb, w1, w3, w2)


def _combine_kernel(src_ref, srcn_ref, y_hbm, wt_ref, x1_ref, mod_ref, g_ref, b_ref, o_ref, ybuf, sem, *, alpha, per):
    i = pl.program_id(0)
    n = pl.num_programs(0)
    slot = i % 2
    tm = x1_ref.shape[0]

    def wait(s):
        for k in range(2):
            pltpu.make_async_copy(y_hbm.at[pl.ds(0, tm * per), :], ybuf.at[s, k], sem.at[s]).wait()

    @pl.when(i == 0)
    def _():
        def body(r, carry):
            for k in range(2):
                _tile_copy(y_hbm, src_ref[0, k, r], ybuf.at[0, k], r * per, per, sem.at[0]).start(priority=k)
            return carry
        lax.fori_loop(0, tm, body, 0, unroll=8)

    for r in range(tm):
        for k in range(2):
            _tile_copy(y_hbm, srcn_ref[0, k, r], ybuf.at[1 - slot, k], r * per, per, sem.at[1 - slot]).start(priority=k)

    wait(slot)
    wt = wt_ref[...]
    y0 = _load_token_tiles(ybuf.at[slot, 0], tm, per).astype(F32)
    y1 = _load_token_tiles(ybuf.at[slot, 1], tm, per).astype(F32)
    ffn = wt[:, 0:1] * y0 + wt[:, 1:2] * y1
    y = alpha * x1_ref[...] + (1.0 + mod_ref[0, 5:6, :]) * ffn
    o_ref[...] = _ln(y, g_ref[...], b_ref[...])

    @pl.when(i == n - 1)
    def _():
        wait(1 - slot)


def _combine(drow3, yb, wt_t, x1, mod3, g2, b2, seq, alpha):
    t, d = x1.shape
    per = d // 2 // LANES
    n, _, tm = drow3.shape
    per_b = seq // tm
    cur = pl.BlockSpec((1, 2, tm), lambda i: (i, 0, 0), memory_space=pltpu.SMEM)
    nxt = pl.BlockSpec((1, 2, tm), lambda i: (jnp.minimum(i + 1, n - 1), 0, 0), memory_space=pltpu.SMEM)
    vec = pl.BlockSpec((1, d), lambda i: (0, 0))
    return pl.pallas_call(
        functools.partial(_combine_kernel, alpha=alpha, per=per),
        grid=(n,),
        in_specs=[cur, nxt,
                  pl.BlockSpec(memory_space=pl.ANY),
                  pl.BlockSpec((tm, 2), lambda i: (i, 0)),
                  pl.BlockSpec((tm, d), lambda i: (i, 0)),
                  pl.BlockSpec((1, 6, d), lambda i: (i // per_b, 0, 0)),
                  vec, vec],
        out_specs=pl.BlockSpec((tm, d), lambda i: (i, 0)),
        out_shape=jax.ShapeDtypeStruct((t, d), F32),
        scratch_shapes=[pltpu.VMEM((2, 2, tm * per, LANES), U32), pltpu.SemaphoreType.DMA((2,))],
        compiler_params=_cparams(("arbitrary",)),
        name="combine",
    )(drow3, drow3, yb, wt_t, x1, mod3, g2, b2)


def _pick_tile(n, want):
    tm = min(n, want)
    assert n % tm == 0
    return tm


def kernel(x, c, positions, ln0_g, ln0_b, w_ada, b_ada, w_in, w_out, ret_log_decay_f, ret_log_decay_b,
           ln1_g, ln1_b, w_group, b_group, w_sub, b_sub, w1, w3, w2, ln2_g, ln2_b):
    batch, seq, d = x.shape
    depth = w_ada.shape[0]
    t = batch * seq
    n_ah = d // 2 // ATTN_HEAD_DIM
    n_rh = d // 2 // RET_HEAD_DIM
    assert depth == 1 and d % (2 * RET_HEAD_DIM) == 0 and seq % (max(DILATIONS) * KW) == 0 and batch <= 8
    assert seq % (QT * CHAINS) == 0
    alpha = (2 * depth) ** 0.25

    inv_rope = ROPE_THETA ** (-jnp.arange(0, ROPE_DIM, 2, dtype=F32) / ROPE_DIM)
    inv_ret = RET_THETA ** (-jnp.linspace(0.0, 1.0, RET_HEAD_DIM // 2, dtype=F32))
    fac = jnp.tile(inv_rope, TOK_PER_ROW).reshape(1, LANES)
    fr = inv_ret.reshape(1, LANES)
    posb = jnp.broadcast_to(positions.astype(F32).reshape(t, 1), (t, LANES))
    posc = jnp.repeat(positions.astype(F32).reshape(t // TOK_PER_ROW, TOK_PER_ROW), ROPE_DIM // 2, axis=1)
    c8 = jnp.zeros((8, d), F32).at[:batch].set(c)
    row = lambda v: v.reshape(1, d)

    xs = x.reshape(t, d)
    mod = _ada(c8, w_ada[0], b_ada[0].reshape(1, -1))
    mod3 = mod[:batch].reshape(batch, 6, d)
    h, tabs = _prep(xs, posb, posc, mod3, row(ln0_g), row(ln0_b), fac, fr, seq, _pick_tile(seq, 512))
    w_in_b = w_in[0].astype(BF16)
    tmp = _pick_tile(seq, 1024)
    z_aqk = _proj(_proj_attn_kernel, lambda j: j, 2, h, w_in_b, tabs, tmp, "proj_attn")
    z_rqk = _proj(_proj_ret_kernel, lambda j: 3 + j, 2, h, w_in_b, tabs, tmp, "proj_ret")
    z_pl = _proj(_proj_plain_kernel, lambda j: 2 + 3 * jnp.minimum(j, 1) + jnp.maximum(j - 1, 0), 3,
                 h, w_in_b, None, tmp, "proj_plain")
    attn = _attn(z_aqk, z_pl, batch, seq, n_ah)
    lg = jnp.stack([ret_log_decay_f[0], ret_log_decay_b[0]]).astype(F32)
    r = _ret(z_rqk, z_pl, lg, batch, seq, n_rh, _pick_tile(seq, 2048))
    x1, hp = _outproj(attn, r, xs, mod3, row(ln0_g), row(ln0_b), row(ln1_g[0]), row(ln1_b[0]),
                      w_out[0].astype(BF16), seq, _pick_tile(seq, 512), alpha)
    wr = jnp.zeros((ROUTER_ROWS, d), F32)
    wr = wr.at[:N_GROUPS].set(w_group[0].T)
    wr = wr.at[N_GROUPS:N_GROUPS + N_EXPERTS].set(w_sub[0].transpose(0, 2, 1).reshape(N_EXPERTS, d))
    br = jnp.zeros((ROUTER_ROWS, 1), F32)
    br = br.at[:N_GROUPS, 0].set(b_group[0]).at[N_GROUPS:N_GROUPS + N_EXPERTS, 0].set(b_sub[0].reshape(-1))
    eid3, wt, rank3, cnt = _router(hp, wr.astype(BF16), br, _pick_tile(seq, 512))
    counts = cnt[:, 0].astype(I32)
    padded = (counts + MOE_BLOCK - 1) // MOE_BLOCK * MOE_BLOCK
    pend = jnp.cumsum(padded)
    pstart = pend - padded
    n_slots = 2 * t + N_EXPERTS * MOE_BLOCK
    nblk = n_slots // MOE_BLOCK
    starts = jnp.arange(nblk, dtype=I32) * MOE_BLOCK
    blk_e = jnp.minimum(jnp.sum((pend[None, :] <= starts[:, None]).astype(I32), axis=1), N_EXPERTS - 1)
    fill_lo = jnp.concatenate([pstart + counts, pend[-1:]])
    fill_hi = jnp.concatenate([pend, jnp.full((1,), n_slots, I32)])
    per = d // 2 // LANES
    onehot = eid3[..., None] == jnp.arange(N_EXPERTS, dtype=I32)
    drow3 = (rank3 + jnp.sum(jnp.where(onehot, pstart, 0), axis=-1)) * per
    xb = _dispatch(fill_lo, fill_hi, drow3, hp, n_slots, per)
    yb = _experts(blk_e, xb, w1[0], w3[0], w2[0])
    out = _combine(drow3, yb, wt.T, x1, mod3, row(ln2_g[0]), row(ln2_b[0]), seq, alpha)
    return out.reshape(batch, seq, d)
```

```python
import functools

import jax
import jax.numpy as jnp
from jax import lax
from jax.experimental import pallas as pl
from jax.experimental.pallas import tpu as pltpu

F32 = jnp.float32
BF16 = jnp.bfloat16
I32 = jnp.int32
U32 = jnp.uint32

LANES = 128
ATTN_HEAD_DIM = 128
RET_HEAD_DIM = 256
DILATIONS = (1, 4, 16)
BAND = 64
QT = 128
KW = QT + 2 * BAND
CHAINS = 16
PERM_ROWS = 256
PERM_UNROLL = 4
ROPE_THETA = 500000.0
ROPE_DIM = ATTN_HEAD_DIM // 4
TOK_PER_ROW = LANES // (ROPE_DIM // 2)
RET_THETA = 10000.0
RET_CHUNK = 128
RET_UNROLL = 4
N_GROUPS = 4
EXPERTS_PER_GROUP = 8
N_EXPERTS = N_GROUPS * EXPERTS_PER_GROUP
MOE_BLOCK = 256
LN_EPS = 1e-5
MASK_VALUE = -1e30
ROUTER_ROWS = 48
HI_MASK = 0xFFFF0000
VMEM_LIMIT = 56 * 1024 * 1024


def _cparams(sem, vmem=VMEM_LIMIT):
    return pltpu.CompilerParams(dimension_semantics=sem, vmem_limit_bytes=vmem)


def _ln(x, g, b):
    mu = jnp.mean(x, axis=-1, keepdims=True)
    xc = x - mu
    var = jnp.mean(xc * xc, axis=-1, keepdims=True)
    return xc * lax.rsqrt(var + LN_EPS) * g + b


def _pack_bf16_pairs(h):
    bits = lax.bitcast_convert_type(h.astype(BF16).astype(F32), U32)
    n = h.shape[1] // 2
    return (bits[:, :n] >> 16) | (bits[:, n:] & jnp.uint32(HI_MASK))


def _store_token_tiles(ref, packed):
    m, n = packed.shape
    per = n // LANES
    for s in range(per):
        ref[pl.ds(s, m, stride=per), :] = packed[:, s * LANES:(s + 1) * LANES]


def _load_token_tiles(ref, m, per, dtype=BF16):
    slabs = [ref[pl.ds(s, m, stride=per), :] for s in range(per)]
    lo = [lax.bitcast_convert_type(p << 16, F32) for p in slabs]
    hi = [lax.bitcast_convert_type(p & jnp.uint32(HI_MASK), F32) for p in slabs]
    return jnp.concatenate(lo + hi, axis=-1).astype(dtype)


def _ada_kernel(c_ref, w_ref, b_ref, o_ref):
    cs = jax.nn.silu(c_ref[...])
    o_ref[...] = jnp.dot(cs.astype(BF16), w_ref[...].astype(BF16), preferred_element_type=F32) + b_ref[...]


def _ada(c8, w_ada, b_ada):
    d, n = w_ada.shape
    tn = min(n, 512)
    return pl.pallas_call(
        _ada_kernel,
        grid=(n // tn,),
        in_specs=[pl.BlockSpec((8, d), lambda j: (0, 0)),
                  pl.BlockSpec((d, tn), lambda j: (0, j)),
                  pl.BlockSpec((1, tn), lambda j: (0, j))],
        out_specs=pl.BlockSpec((8, tn), lambda j: (0, j)),
        out_shape=jax.ShapeDtypeStruct((8, n), F32),
        compiler_params=_cparams(("arbitrary",)),
        name="ada",
    )(c8, w_ada, b_ada)


def _prep_kernel(x_ref, pos_ref, posc_ref, mod_ref, g_ref, b_ref, fac_ref, fr_ref, h_ref, tab_ref):
    xn = _ln(x_ref[...], g_ref[...], b_ref[...])
    h_ref[...] = (xn * (1.0 + mod_ref[0, 1:2, :]) + mod_ref[0, 0:1, :]).astype(BF16)
    half = ROPE_DIM // 2
    angc = posc_ref[...] * fac_ref[...]
    cc, sc = jnp.cos(angc), jnp.sin(angc)
    rows_c = angc.shape[0]
    lane = lax.broadcasted_iota(I32, angc.shape, 1)
    first, second = lane < half, (lane >= half) & (lane < 2 * half)
    for ts in range(TOK_PER_ROW):
        shift = (LANES - ts * half) % LANES
        c = pltpu.roll(cc, shift, 1) if shift else cc
        s = pltpu.roll(sc, shift, 1) if shift else sc
        rows = pl.ds(ts, rows_c, stride=TOK_PER_ROW)
        tab_ref.at[0][rows, :] = jnp.where(first, c, jnp.where(second, pltpu.roll(c, half, 1), 1.0))
        tab_ref.at[1][rows, :] = jnp.where(first, -s, 0.0)
        tab_ref.at[2][rows, :] = jnp.where(second, pltpu.roll(s, half, 1), 0.0)
    angr = pos_ref[...] * fr_ref[...]
    tab_ref[3] = jnp.cos(angr)
    tab_ref[4] = jnp.sin(angr)


def _prep(x2, posb, posc, mod3, ln0_g, ln0_b, fac, fr, seq, tm):
    t, d = x2.shape
    per_b = seq // tm
    vec = lambda n: pl.BlockSpec((1, n), lambda i: (0, 0))
    return pl.pallas_call(
        _prep_kernel,
        grid=(t // tm,),
        in_specs=[pl.BlockSpec((tm, d), lambda i: (i, 0)),
                  pl.BlockSpec((tm, LANES), lambda i: (i, 0)),
                  pl.BlockSpec((tm // TOK_PER_ROW, LANES), lambda i: (i, 0)),
                  pl.BlockSpec((1, 6, d), lambda i: (i // per_b, 0, 0)),
                  vec(d), vec(d), vec(LANES), vec(LANES)],
        out_specs=[pl.BlockSpec((tm, d), lambda i: (i, 0)), pl.BlockSpec((5, tm, LANES), lambda i: (0, i, 0))],
        out_shape=[jax.ShapeDtypeStruct((t, d), BF16), jax.ShapeDtypeStruct((5, t, LANES), F32)],
        compiler_params=_cparams(("arbitrary",)),
        name="prep",
    )(x2, posb, posc, mod3, ln0_g, ln0_b, fac, fr)


def _proj_chunks(h_ref, w_ref, n_sub):
    for hc in range(n_sub):
        yield hc, jnp.dot(h_ref[...], w_ref[:, hc * 2 * LANES:(hc + 1) * 2 * LANES], preferred_element_type=F32)


def _proj_attn_kernel(h_ref, w_ref, tab_ref, z_ref, *, n_sub):
    scale = jnp.where(pl.program_id(0) == 0, ATTN_HEAD_DIM ** -0.5, 1.0).astype(F32)
    ca, s1, s2 = tab_ref[0] * scale, tab_ref[1] * scale, tab_ref[2] * scale
    half = ROPE_DIM // 2
    for hc, res in _proj_chunks(h_ref, w_ref, n_sub):
        for lanes in (slice(0, LANES), slice(LANES, 2 * LANES)):
            x = res[:, lanes]
            r = x * ca + pltpu.roll(x, LANES - half, 1) * s1 + pltpu.roll(x, half, 1) * s2
            z_ref[hc, :, lanes] = r.astype(BF16)


def _proj_ret_kernel(h_ref, w_ref, tab_ref, z_ref, *, n_sub):
    scale = jnp.where(pl.program_id(0) == 1, RET_HEAD_DIM ** -0.5, 1.0).astype(F32)
    cr, sr = tab_ref[3] * scale, tab_ref[4] * scale
    for hc, res in _proj_chunks(h_ref, w_ref, n_sub):
        x1, x2 = res[:, :LANES], res[:, LANES:]
        z_ref[hc, :, :LANES] = (x1 * cr - x2 * sr).astype(BF16)
        z_ref[hc, :, LANES:] = (x2 * cr + x1 * sr).astype(BF16)


def _proj_plain_kernel(h_ref, w_ref, z_ref, *, n_sub):
    for hc, res in _proj_chunks(h_ref, w_ref, n_sub):
        z_ref[hc] = res.astype(BF16)


def _proj(body, col_of, n_seg, h, w_in_b, tabs, tm, name):
    t, d = h.shape
    tn = d // 2
    n_sub = tn // (2 * LANES)
    in_specs = [pl.BlockSpec((tm, d), lambda j, i: (i, 0)),
                pl.BlockSpec((d, tn), lambda j, i: (0, col_of(j)))]
    args = [h, w_in_b]
    if tabs is not None:
        in_specs.append(pl.BlockSpec((5, tm, LANES), lambda j, i: (0, i, 0)))
        args.append(tabs)
    return pl.pallas_call(
        functools.partial(body, n_sub=n_sub),
        grid=(n_seg, t // tm),
        in_specs=in_specs,
        out_specs=pl.BlockSpec((n_sub, tm, 2 * LANES), lambda j, i: (j, i, 0)),
        out_shape=jax.ShapeDtypeStruct((n_seg * n_sub, t, 2 * LANES), BF16),
        compiler_params=_cparams(("arbitrary", "arbitrary")),
        name=name,
    )(*args)


def _attn_tile(q, k, v, bias):
    s = lax.dot_general(q, k, (((1,), (1,)), ((), ())), preferred_element_type=F32) + bias
    if s.shape[1] == 2 * LANES:
        m = jnp.max(jnp.maximum(s[:, :LANES], s[:, LANES:]), axis=-1, keepdims=True)
        p = jnp.exp(s - m)
        den = jnp.sum(p[:, :LANES] + p[:, LANES:], axis=-1, keepdims=True)
    else:
        m = jnp.max(s, axis=-1, keepdims=True)
        p = jnp.exp(s - m)
        den = jnp.sum(p, axis=-1, keepdims=True)
    acc = jnp.dot(p.astype(BF16), v, preferred_element_type=F32)
    return acc, jnp.broadcast_to(m, acc.shape), jnp.broadcast_to(den, acc.shape)


def _attn_kernel(q_ref, k_ref, v_ref, bias_ref, p4_ref, p16_ref, o_ref,
                 acc_ref, m_ref, d_ref, qp_ref, kp_ref, vp_ref, *, seq):
    n_items = seq // QT

    def window(n, length):
        qs = n * QT
        ks = jnp.clip(qs - BAND, 0, length - KW)
        return qs, ks, bias_ref[lax.div(qs - ks, BAND)]

    def perm_body(it, carry):
        for u in range(PERM_UNROLL):
            blk = it * PERM_UNROLL + u
            r0 = pl.multiple_of(blk * PERM_ROWS, PERM_ROWS)
            qk = jnp.concatenate([q_ref[0, 0, pl.ds(r0, PERM_ROWS), :], k_ref[0, 0, pl.ds(r0, PERM_ROWS), :]], axis=-1)
            v = v_ref[0, 0, pl.ds(r0, PERM_ROWS), :]
            for slot, (dil, p_ref) in enumerate(((4, p4_ref), (16, p16_ref))):
                length, w = seq // dil, PERM_ROWS // dil
                qkp = jnp.dot(p_ref[...], qk, preferred_element_type=F32).astype(BF16)
                vp = jnp.dot(p_ref[...], v, preferred_element_type=F32).astype(BF16)
                for r in range(dil):
                    rows = pl.ds(pl.multiple_of(r * length + blk * w, w), w)
                    qp_ref[slot, rows, :] = qkp[r * w:(r + 1) * w, :ATTN_HEAD_DIM]
                    kp_ref[slot, rows, :] = qkp[r * w:(r + 1) * w, ATTN_HEAD_DIM:]
                    vp_ref[slot, rows, :] = vp[r * w:(r + 1) * w, :]
        return carry

    lax.fori_loop(0, seq // PERM_ROWS // PERM_UNROLL, perm_body, 0)

    def dilated_tile(item, dil, slot):
        length = seq // dil
        n, r = lax.div(item, dil), lax.rem(item, dil)
        qs, ks, bias = window(n, length)
        qrow = pl.multiple_of(r * length + qs, QT)
        krow = pl.multiple_of(r * length + ks, BAND)
        out = _attn_tile(qp_ref[slot, pl.ds(qrow, QT), :], kp_ref[slot, pl.ds(krow, KW), :],
                         vp_ref[slot, pl.ds(krow, KW), :], bias)
        return out, pl.ds(qs * dil + r, QT, stride=dil)

    def body16(it, carry):
        for g in range(CHAINS):
            (acc, m, den), rows = dilated_tile(it * CHAINS + g, 16, 1)
            acc_ref[rows, :] = acc
            m_ref[rows, :] = m
            d_ref[rows, :] = den
        return carry

    lax.fori_loop(0, n_items // CHAINS, body16, 0)

    def body4(it, carry):
        for g in range(CHAINS):
            (acc, m, den), rows = dilated_tile(it * CHAINS + g, 4, 0)
            m0 = m_ref[rows, :]
            mn = jnp.maximum(m0, m)
            a, b = jnp.exp(m0 - mn), jnp.exp(m - mn)
            acc_ref[rows, :] = a * acc_ref[rows, :] + b * acc
            d_ref[rows, :] = a * d_ref[rows, :] + b * den
            m_ref[rows, :] = mn
        return carry

    lax.fori_loop(0, n_items // CHAINS, body4, 0)

    def body1(it, carry):
        for g in range(CHAINS):
            qs, ks, bias = window(it * CHAINS + g, seq)
            qs, ks = pl.multiple_of(qs, QT), pl.multiple_of(ks, BAND)
            acc, m, den = _attn_tile(q_ref[0, 0, pl.ds(qs, QT), :], k_ref[0, 0, pl.ds(ks, KW), :],
                                     v_ref[0, 0, pl.ds(ks, KW), :], bias)
            rows = pl.ds(qs, QT)
            m0 = m_ref[rows, :]
            mn = jnp.maximum(m0, m)
            a, b = jnp.exp(m0 - mn), jnp.exp(m - mn)
            num = a * acc_ref[rows, :] + b * acc
            o_ref[0, 0, rows, :] = (num / (a * d_ref[rows, :] + b * den)).astype(BF16)
        return carry

    lax.fori_loop(0, n_items // CHAINS, body1, 0)


def _perm_matrix(dil):
    w = PERM_ROWS // dil
    i = jnp.arange(PERM_ROWS)
    src = (i % w) * dil + i // w
    return (src[:, None] == jnp.arange(PERM_ROWS)[None, :]).astype(BF16)


def _attn(zqk, zv_, batch, seq, n_ah):
    n_sub = n_ah // 2
    zqk = zqk.reshape(zqk.shape[0], batch, seq, 2 * LANES)
    zv_ = zv_.reshape(zv_.shape[0], batch, seq, 2 * LANES)
    blk = (1, 1, seq, ATTN_HEAD_DIM)
    i = jnp.arange(QT)[:, None]
    jj = jnp.arange(KW)[None, :]
    bias = jnp.stack([jnp.where(jnp.abs(i + o * BAND - jj) <= BAND, 0.0, MASK_VALUE) for o in range(3)]).astype(F32)
    const = lambda shape: pl.BlockSpec(shape, lambda b, h: (0,) * len(shape))
    o = pl.pallas_call(
        functools.partial(_attn_kernel, seq=seq),
        grid=(batch, n_ah),
        in_specs=[pl.BlockSpec(blk, lambda b, h: (h // 2, b, 0, h % 2)),
                  pl.BlockSpec(blk, lambda b, h: (n_sub + h // 2, b, 0, h % 2)),
                  pl.BlockSpec(blk, lambda b, h: (h // 2, b, 0, h % 2)),
                  const((3, QT, KW)), const((PERM_ROWS, PERM_ROWS)), const((PERM_ROWS, PERM_ROWS))],
        out_specs=pl.BlockSpec(blk, lambda b, h: (h, b, 0, 0)),
        out_shape=jax.ShapeDtypeStruct((n_ah, batch, seq, ATTN_HEAD_DIM), BF16),
        scratch_shapes=[pltpu.VMEM((seq, LANES), F32)] * 3
        + [pltpu.VMEM((2, seq, ATTN_HEAD_DIM), BF16)] * 3,
        compiler_params=_cparams(("arbitrary", "arbitrary")),
        name="attn",
    )(zqk, zqk, zv_, bias, _perm_matrix(4), _perm_matrix(16))
    return o.reshape(n_ah, batch * seq, ATTN_HEAD_DIM)


def _ret_kernel(lg_ref, q_ref, k_ref, v_ref, g_ref, o_ref, y_ref, sf_ref, sb_ref, dmat_ref, *, n_tiles, tile):
    h = pl.program_id(1)
    t = pl.program_id(2)
    c = RET_CHUNK
    lgf = lg_ref[0, h]
    lgb = lg_ref[1, h]
    col = lax.broadcasted_iota(I32, (c, 1), 0).astype(F32)
    row = lax.broadcasted_iota(I32, (1, c), 1).astype(F32)
    n_iter = tile // c // RET_UNROLL

    @pl.when(t == 0)
    def _():
        sf_ref[...] = jnp.zeros_like(sf_ref)
        sb_ref[...] = jnp.zeros_like(sb_ref)
        diff = (lax.broadcasted_iota(I32, (c, c), 0) - lax.broadcasted_iota(I32, (c, c), 1)).astype(F32)
        dmat_ref[...] = jnp.where(diff >= 0, jnp.exp(lgf * jnp.maximum(diff, 0.0)), 0.0) \
            + jnp.where(diff < 0, jnp.exp(lgb * jnp.maximum(-diff, 0.0)), 0.0)

    def chunk(r0):
        rows = pl.ds(pl.multiple_of(r0, c), c)
        return rows, q_ref[0, rows, :], k_ref[0, rows, :], v_ref[0, rows, :]

    @pl.when(t < n_tiles)
    def _():
        xi = jnp.exp(lgf * (col + 1.0))
        zeta = jnp.exp(lgf * (c - 1.0 - row))
        cdec = jnp.exp(lgf * jnp.full((1, 1), float(c), F32))
        base = t * tile

        def body(it, carry):
            state = sf_ref[...]
            for u in range(RET_UNROLL):
                r0 = (it * RET_UNROLL + u) * c
                _, q, k, v = chunk(r0)
                s = lax.dot_general(q, k, (((1,), (1,)), ((), ())), preferred_element_type=F32) * dmat_ref[...]
                y = jnp.dot(s.astype(BF16), v, preferred_element_type=F32)
                y = y + jnp.dot(q, state.astype(BF16), preferred_element_type=F32) * xi
                kt = (k.astype(F32).T * zeta).astype(BF16)
                state = state * cdec + jnp.dot(kt, v, preferred_element_type=F32)
                y_ref[pl.ds(pl.multiple_of(base + r0, c), c), :] = y
            sf_ref[...] = state
            return carry

        lax.fori_loop(0, n_iter, body, 0)

    @pl.when(t >= n_tiles)
    def _():
        xi = jnp.exp(lgb * (c - col))
        zeta = jnp.exp(lgb * row)
        cdec = jnp.exp(lgb * jnp.full((1, 1), float(c), F32))
        base = (2 * n_tiles - 1 - t) * tile

        def body(it, carry):
            state = sb_ref[...]
            for u in range(RET_UNROLL):
                r0 = (tile // c - 1 - (it * RET_UNROLL + u)) * c
                rows, q, k, v = chunk(r0)
                y = y_ref[pl.ds(pl.multiple_of(base + r0, c), c), :]
                y = y + jnp.dot(q, state.astype(BF16), preferred_element_type=F32) * xi
                kt = (k.astype(F32).T * zeta).astype(BF16)
                state = state * cdec + jnp.dot(kt, v, preferred_element_type=F32)
                mu = jnp.mean(y, axis=-1, keepdims=True)
                yc = y - mu
                var = jnp.mean(yc * yc, axis=-1, keepdims=True)
                yn = yc * lax.rsqrt(var + LN_EPS)
                gate = jax.nn.silu(g_ref[0, rows, :].astype(F32))
                o_ref[0, rows, :] = (gate * yn).astype(BF16)
            sb_ref[...] = state
            return carry

        lax.fori_loop(0, n_iter, body, 0)


def _ret(zqk, zvg, lg, batch, seq, n_rh, tile):
    n_tiles = seq // tile
    per_b = seq // tile

    def rows(t):
        return jnp.where(t < n_tiles, t, 2 * n_tiles - 1 - t)

    def spec(seg):
        return pl.BlockSpec((1, tile, RET_HEAD_DIM), lambda b, h, t: (seg * n_rh + h, b * per_b + rows(t), 0))

    return pl.pallas_call(
        functools.partial(_ret_kernel, n_tiles=n_tiles, tile=tile),
        grid=(batch, n_rh, 2 * n_tiles),
        in_specs=[pl.BlockSpec(memory_space=pltpu.SMEM), spec(0), spec(1), spec(1), spec(2)],
        out_specs=pl.BlockSpec((1, tile, RET_HEAD_DIM),
                               lambda b, h, t: (h, b * per_b + jnp.where(t < n_tiles, n_tiles - 1, 2 * n_tiles - 1 - t), 0)),
        out_shape=jax.ShapeDtypeStruct((n_rh, batch * seq, RET_HEAD_DIM), BF16),
        scratch_shapes=[pltpu.VMEM((seq, RET_HEAD_DIM), F32),
                        pltpu.VMEM((RET_HEAD_DIM, RET_HEAD_DIM), F32),
                        pltpu.VMEM((RET_HEAD_DIM, RET_HEAD_DIM), F32),
                        pltpu.VMEM((RET_CHUNK, RET_CHUNK), F32)],
        compiler_params=_cparams(("arbitrary", "arbitrary", "arbitrary")),
        name="ret",
    )(lg, zqk, zqk, zvg, zvg)


def _outproj_kernel(a_ref, r_ref, x_ref, mod_ref, g0_ref, b0_ref, g1_ref, b1_ref, w_ref, x1_ref, hp_ref,
                    *, n_ah, n_rh, alpha):
    mix = jnp.concatenate([a_ref[h] for h in range(n_ah)] + [r_ref[h] for h in range(n_rh)], axis=-1)
    acc = jnp.dot(mix, w_ref[...], preferred_element_type=F32)
    xn = _ln(x_ref[...], g0_ref[...], b0_ref[...])
    y = alpha * xn + (1.0 + mod_ref[0, 2:3, :]) * acc
    x1 = _ln(y, g1_ref[...], b1_ref[...])
    x1_ref[...] = x1
    _store_token_tiles(hp_ref, _pack_bf16_pairs(x1 * (1.0 + mod_ref[0, 4:5, :]) + mod_ref[0, 3:4, :]))


def _outproj(attn, r, x2, mod3, g0, b0, g1, b1, w_out_b, seq, tm, alpha):
    t, d = x2.shape
    n_ah, n_rh = attn.shape[0], r.shape[0]
    per_b = seq // tm
    per = d // 2 // LANES
    row = lambda i: (i, 0)
    vec = pl.BlockSpec((1, d), lambda i: (0, 0))
    return pl.pallas_call(
        functools.partial(_outproj_kernel, n_ah=n_ah, n_rh=n_rh, alpha=alpha),
        grid=(t // tm,),
        in_specs=[pl.BlockSpec((n_ah, tm, ATTN_HEAD_DIM), lambda i: (0, i, 0)),
                  pl.BlockSpec((n_rh, tm, RET_HEAD_DIM), lambda i: (0, i, 0)),
                  pl.BlockSpec((tm, d), row),
                  pl.BlockSpec((1, 6, d), lambda i: (i // per_b, 0, 0)),
                  vec, vec, vec, vec,
                  pl.BlockSpec((d, d), lambda i: (0, 0))],
        out_specs=[pl.BlockSpec((tm, d), row), pl.BlockSpec((tm * per, LANES), row)],
        out_shape=[jax.ShapeDtypeStruct((t, d), F32), jax.ShapeDtypeStruct((t * per, LANES), U32)],
        compiler_params=_cparams(("arbitrary",)),
        name="outproj",
    )(attn, r, x2, mod3, g0, b0, g1, b1, w_out_b)


def _first_argmax(rows):
    best, idx = rows[0], jnp.zeros(rows[0].shape, I32)
    for e in range(1, len(rows)):
        better = rows[e] > best
        idx = jnp.where(better, e, idx)
        best = jnp.maximum(best, rows[e])
    return best, idx


def _router_kernel(h_ref, w_ref, b_ref, u_ref, eid_ref, wt_ref, rank_ref, cnt_ref, carry_ref, *, per):
    i = pl.program_id(0)

    @pl.when(i == 0)
    def _():
        carry_ref[...] = jnp.zeros_like(carry_ref)

    tm = eid_ref.shape[2]
    lt = lax.dot_general(w_ref[...], _load_token_tiles(h_ref, tm, per), (((1,), (1,)), ((), ())),
                         preferred_element_type=F32) + b_ref[...]
    grow = [lt[g:g + 1, :] for g in range(N_GROUPS)]
    gmax, gsel = _first_argmax(grow)
    gsum = grow[0] * 0.0
    for g in range(N_GROUPS):
        gsum = gsum + jnp.exp(grow[g] - gmax)
    pg = 1.0 / gsum
    srow = []
    for e in range(EXPERTS_PER_GROUP):
        r = lt[N_GROUPS + e:N_GROUPS + e + 1, :]
        for g in range(1, N_GROUPS):
            o = N_GROUPS + g * EXPERTS_PER_GROUP + e
            r = jnp.where(gsel == g, lt[o:o + 1, :], r)
        srow.append(r)
    v1, i1 = _first_argmax(srow)
    v2, i2 = _first_argmax([jnp.where(i1 == e, -jnp.inf, srow[e]) for e in range(EXPERTS_PER_GROUP)])
    e2 = jnp.exp(v2 - v1)
    den = 1.0 + e2
    wt_ref[0:1, :] = (1.0 / den) * pg
    wt_ref[1:2, :] = (e2 / den) * pg
    eid0 = gsel * EXPERTS_PER_GROUP + i1
    eid1 = gsel * EXPERTS_PER_GROUP + i2
    eid_ref[0, 0:1, :] = eid0
    eid_ref[0, 1:2, :] = eid1
    erow = lax.broadcasted_iota(I32, (N_EXPERTS, tm), 0)
    oh0 = (erow == eid0).astype(F32)
    oh1 = (erow == eid1).astype(F32)
    oh = oh0 + oh1
    incl = jnp.dot(oh.astype(BF16), u_ref[...], preferred_element_type=F32)
    before = carry_ref[:, 0:1] + incl - oh
    rank_ref[0, 0:1, :] = jnp.sum(oh0 * before, axis=0, keepdims=True).astype(I32)
    rank_ref[0, 1:2, :] = jnp.sum(oh1 * before, axis=0, keepdims=True).astype(I32)
    carry = carry_ref[...] + jnp.sum(oh, axis=1, keepdims=True)
    carry_ref[...] = carry
    cnt_ref[...] = carry


def _router(hp, wr, br, tm):
    d = wr.shape[1]
    per = d // 2 // LANES
    t = hp.shape[0] // per
    tri = (lax.broadcasted_iota(I32, (tm, tm), 0) <= lax.broadcasted_iota(I32, (tm, tm), 1)).astype(BF16)
    tile3 = pl.BlockSpec((1, 2, tm), lambda i: (i, 0, 0))
    return pl.pallas_call(
        functools.partial(_router_kernel, per=per),
        grid=(t // tm,),
        in_specs=[pl.BlockSpec((tm * per, LANES), lambda i: (i, 0)),
                  pl.BlockSpec((ROUTER_ROWS, d), lambda i: (0, 0)),
                  pl.BlockSpec((ROUTER_ROWS, 1), lambda i: (0, 0)),
                  pl.BlockSpec((tm, tm), lambda i: (0, 0))],
        out_specs=[tile3, pl.BlockSpec((2, tm), lambda i: (0, i)), tile3,
                   pl.BlockSpec((N_EXPERTS, LANES), lambda i: (0, 0))],
        out_shape=[jax.ShapeDtypeStruct((t // tm, 2, tm), I32), jax.ShapeDtypeStruct((2, t), F32),
                   jax.ShapeDtypeStruct((t // tm, 2, tm), I32), jax.ShapeDtypeStruct((N_EXPERTS, LANES), F32)],
        scratch_shapes=[pltpu.VMEM((N_EXPERTS, LANES), F32)],
        compiler_params=_cparams(("arbitrary",)),
        name="router",
    )(hp, wr, br, tri)


def _tile_copy(src, s_row, dst, d_row, per, sem):
    return pltpu.make_async_copy(src.at[pl.ds(pl.multiple_of(s_row, per), per), :],
                                 dst.at[pl.ds(pl.multiple_of(d_row, per), per), :], sem)


def _dispatch_kernel(lo_ref, hi_ref, dst_ref, h_ref, xb_hbm, stage, zero, sem, *, per):
    i = pl.program_id(0)
    n = pl.num_programs(0)
    slot = i % 2
    tm = dst_ref.shape[2] // 2

    def wait_tile(s):
        pltpu.make_async_copy(xb_hbm.at[pl.ds(0, 2 * tm * per), :], xb_hbm.at[pl.ds(0, 2 * tm * per), :], sem.at[s]).wait()

    @pl.when(i >= 2)
    def _():
        wait_tile(slot)

    stage[slot] = h_ref[...]

    def body(r, carry):
        for k in range(2):
            _tile_copy(stage.at[slot], r * per, xb_hbm, dst_ref[0, 0, k * tm + r], per, sem.at[slot]).start(priority=k)
        return carry

    lax.fori_loop(0, tm, body, 0, unroll=8)

    @pl.when(i == n - 1)
    def _():
        @pl.when(n >= 2)
        def _():
            wait_tile(1 - slot)
        wait_tile(slot)
        zero[...] = jnp.zeros_like(zero)

        def fill(e, carry):
            def one(s, c):
                _tile_copy(zero, 0, xb_hbm, s * per, per, sem.at[2]).start()
                return c
            lax.fori_loop(lo_ref[e], hi_ref[e], one, 0)

            def one_wait(s, c):
                _tile_copy(zero, 0, xb_hbm, s * per, per, sem.at[2]).wait()
                return c
            lax.fori_loop(lo_ref[e], hi_ref[e], one_wait, 0)
            return carry

        lax.fori_loop(0, N_EXPERTS + 1, fill, 0)


def _dispatch(fill_lo, fill_hi, drow3, hp, n_slots, per):
    n, tm = drow3.shape[0], drow3.shape[2] // 2
    smem3 = pl.BlockSpec((1, 1, 2 * tm), lambda i, *_: (i, 0, 0), memory_space=pltpu.SMEM)
    grid_spec = pltpu.PrefetchScalarGridSpec(
        num_scalar_prefetch=2,
        grid=(n,),
        in_specs=[smem3, pl.BlockSpec((tm * per, LANES), lambda i, *_: (i, 0))],
        out_specs=pl.BlockSpec(memory_space=pl.ANY),
        scratch_shapes=[pltpu.VMEM((2, tm * per, LANES), U32), pltpu.VMEM((per, LANES), U32),
                        pltpu.SemaphoreType.DMA((3,))])
    return pl.pallas_call(
        functools.partial(_dispatch_kernel, per=per),
        grid_spec=grid_spec,
        out_shape=jax.ShapeDtypeStruct((n_slots * per, LANES), U32),
        compiler_params=_cparams(("arbitrary",)),
        name="dispatch",
    )(fill_lo, fill_hi, drow3, hp)


def _expert_kernel(be_ref, first_ref, par_ref, nxt_ref, has_ref, x_ref, w1_hbm, w3_hbm, w2_hbm, y_ref,
                   wf1, wf3, wf2, w1b, w3b, w2b, sem, *, per):
    i = pl.program_id(0)

    def weight_copies(e, s):
        return [pltpu.make_async_copy(src.at[e], dst.at[s], sem.at[s])
                for src, dst in ((w1_hbm, wf1), (w3_hbm, wf3), (w2_hbm, wf2))]

    @pl.when(i == 0)
    def _():
        for cp in weight_copies(be_ref[0], 0):
            cp.start()

    @pl.when(first_ref[i] == 1)
    def _():
        s = par_ref[i]
        for cp in weight_copies(be_ref[i], s):
            cp.wait()

        @pl.when(has_ref[i] == 1)
        def _():
            for cp in weight_copies(nxt_ref[i], 1 - s):
                cp.start()

        w1b[...] = wf1[s].astype(BF16)
        w3b[...] = wf3[s].astype(BF16)
        w2b[...] = wf2[s].astype(BF16)

    x = _load_token_tiles(x_ref, MOE_BLOCK, per)
    a = jnp.dot(x, w1b[...], preferred_element_type=F32)
    b = jnp.dot(x, w3b[...], preferred_element_type=F32)
    mid = (jax.nn.silu(a) * b).astype(BF16)
    _store_token_tiles(y_ref, _pack_bf16_pairs(jnp.dot(mid, w2b[...], preferred_element_type=F32)))


def _experts(blk_e, xb, w1, w3, w2):
    d, ff = w1.shape[1], w1.shape[2]
    per = d // 2 // LANES
    nblk = xb.shape[0] // per // MOE_BLOCK
    rows = pl.BlockSpec((MOE_BLOCK * per, LANES), lambda i, *_: (i, 0))
    first = jnp.concatenate([jnp.ones((1,), I32), (blk_e[1:] != blk_e[:-1]).astype(I32)])
    parity = (jnp.cumsum(first) - 1) % 2
    nxt = jnp.min(jnp.where(blk_e[None, :] > blk_e[:, None], blk_e[None, :], N_EXPERTS), axis=1)
    has_next = (nxt < N_EXPERTS).astype(I32)
    nxt = jnp.minimum(nxt, N_EXPERTS - 1)
    grid_spec = pltpu.PrefetchScalarGridSpec(
        num_scalar_prefetch=5,
        grid=(nblk,),
        in_specs=[rows] + [pl.BlockSpec(memory_space=pl.ANY)] * 3,
        out_specs=rows,
        scratch_shapes=[pltpu.VMEM((2, d, ff), F32), pltpu.VMEM((2, d, ff), F32), pltpu.VMEM((2, ff, d), F32),
                        pltpu.VMEM((d, ff), BF16), pltpu.VMEM((d, ff), BF16), pltpu.VMEM((ff, d), BF16),
                        pltpu.SemaphoreType.DMA((2,))])
    return pl.pallas_call(
        functools.partial(_expert_kernel, per=per),
        grid_spec=grid_spec,
        out_shape=jax.ShapeDtypeStruct(xb.shape, U32),
        compiler_params=_cparams(("arbitrary",)),
        name="experts",
    )(blk_e, first, parity.astype(I32), nxt.astype(I32), has_next, xb, w1, w3, w2)


def _combine_kernel(src_ref, srcn_ref, y_hbm, wt_ref, x1_ref, mod_ref, g_ref, b_ref, o_ref, ybuf, sem, *, alpha, per):
    i = pl.program_id(0)
    n = pl.num_programs(0)
    slot = i % 2
    tm = x1_ref.shape[0]

    def wait(s):
        for k in range(2):
            pltpu.make_async_copy(y_hbm.at[pl.ds(0, tm * per), :], ybuf.at[s, k], sem.at[s]).wait()

    @pl.when(i == 0)
    def _():
        def body(r, carry):
            for k in range(2):
                _tile_copy(y_hbm, src_ref[0, 0, k * tm + r], ybuf.at[0, k], r * per, per, sem.at[0]).start(priority=k)
            return carry
        lax.fori_loop(0, tm, body, 0, unroll=8)

    for r in range(tm):
        for k in range(2):
            _tile_copy(y_hbm, srcn_ref[0, 0, k * tm + r], ybuf.at[1 - slot, k], r * per, per, sem.at[1 - slot]).start(priority=k)

    wait(slot)
    wt = wt_ref[...]
    y0 = _load_token_tiles(ybuf.at[slot, 0], tm, per, F32)
    y1 = _load_token_tiles(ybuf.at[slot, 1], tm, per, F32)
    ffn = wt[:, 0:1] * y0 + wt[:, 1:2] * y1
    y = alpha * x1_ref[...] + (1.0 + mod_ref[0, 5:6, :]) * ffn
    o_ref[...] = _ln(y, g_ref[...], b_ref[...])

    @pl.when(i == n - 1)
    def _():
        wait(1 - slot)


def _combine(drow3, yb, wt_t, x1, mod3, g2, b2, seq, alpha):
    t, d = x1.shape
    per = d // 2 // LANES
    n, tm = drow3.shape[0], drow3.shape[2] // 2
    per_b = seq // tm
    cur = pl.BlockSpec((1, 1, 2 * tm), lambda i: (i, 0, 0), memory_space=pltpu.SMEM)
    nxt = pl.BlockSpec((1, 1, 2 * tm), lambda i: (jnp.minimum(i + 1, n - 1), 0, 0), memory_space=pltpu.SMEM)
    vec = pl.BlockSpec((1, d), lambda i: (0, 0))
    return pl.pallas_call(
        functools.partial(_combine_kernel, alpha=alpha, per=per),
        grid=(n,),
        in_specs=[cur, nxt,
                  pl.BlockSpec(memory_space=pl.ANY),
                  pl.BlockSpec((tm, 2), lambda i: (i, 0)),
                  pl.BlockSpec((tm, d), lambda i: (i, 0)),
                  pl.BlockSpec((1, 6, d), lambda i: (i // per_b, 0, 0)),
                  vec, vec],
        out_specs=pl.BlockSpec((tm, d), lambda i: (i, 0)),
        out_shape=jax.ShapeDtypeStruct((t, d), F32),
        scratch_shapes=[pltpu.VMEM((2, 2, tm * per, LANES), U32), pltpu.SemaphoreType.DMA((2,))],
        compiler_params=_cparams(("arbitrary",)),
        name="combine",
    )(drow3, drow3, yb, wt_t, x1, mod3, g2, b2)


def _pick_tile(n, want):
    tm = min(n, want)
    assert n % tm == 0
    return tm


def kernel(x, c, positions, ln0_g, ln0_b, w_ada, b_ada, w_in, w_out, ret_log_decay_f, ret_log_decay_b,
           ln1_g, ln1_b, w_group, b_group, w_sub, b_sub, w1, w3, w2, ln2_g, ln2_b):
    batch, seq, d = x.shape
    depth = w_ada.shape[0]
    t = batch * seq
    n_ah = d // 2 // ATTN_HEAD_DIM
    n_rh = d // 2 // RET_HEAD_DIM
    assert depth == 1 and d % (2 * RET_HEAD_DIM) == 0 and batch <= 8
    assert seq % (QT * CHAINS) == 0 and seq % (max(DILATIONS) * PERM_ROWS) == 0 and seq // max(DILATIONS) >= KW
    alpha = (2 * depth) ** 0.25

    inv_rope = ROPE_THETA ** (-jnp.arange(0, ROPE_DIM, 2, dtype=F32) / ROPE_DIM)
    inv_ret = RET_THETA ** (-jnp.linspace(0.0, 1.0, RET_HEAD_DIM // 2, dtype=F32))
    fac = jnp.tile(inv_rope, TOK_PER_ROW).reshape(1, LANES)
    fr = inv_ret.reshape(1, LANES)
    posb = jnp.broadcast_to(positions.astype(F32).reshape(t, 1), (t, LANES))
    posc = jnp.repeat(positions.astype(F32).reshape(t // TOK_PER_ROW, TOK_PER_ROW), ROPE_DIM // 2, axis=1)
    c8 = jnp.zeros((8, d), F32).at[:batch].set(c)
    row = lambda v: v.reshape(1, d)

    xs = x.reshape(t, d)
    mod = _ada(c8, w_ada[0], b_ada[0].reshape(1, -1))
    mod3 = mod[:batch].reshape(batch, 6, d)
    h, tabs = _prep(xs, posb, posc, mod3, row(ln0_g), row(ln0_b), fac, fr, seq, _pick_tile(seq, 512))
    w_in_b = w_in[0].astype(BF16)
    tmp = _pick_tile(seq, 2048)
    z_aqk = _proj(_proj_attn_kernel, lambda j: j, 2, h, w_in_b, tabs, tmp, "proj_attn")
    z_rqk = _proj(_proj_ret_kernel, lambda j: 3 + j, 2, h, w_in_b, tabs, tmp, "proj_ret")
    z_pl = _proj(_proj_plain_kernel, lambda j: 2 + 3 * jnp.minimum(j, 1) + jnp.maximum(j - 1, 0), 3,
                 h, w_in_b, None, tmp, "proj_plain")
    attn = _attn(z_aqk, z_pl, batch, seq, n_ah)
    lg = jnp.stack([ret_log_decay_f[0], ret_log_decay_b[0]]).astype(F32)
    r = _ret(z_rqk, z_pl, lg, batch, seq, n_rh, _pick_tile(seq, 2048))
    x1, hp = _outproj(attn, r, xs, mod3, row(ln0_g), row(ln0_b), row(ln1_g[0]), row(ln1_b[0]),
                      w_out[0].astype(BF16), seq, _pick_tile(seq, 512), alpha)
    wr = jnp.zeros((ROUTER_ROWS, d), F32)
    wr = wr.at[:N_GROUPS].set(w_group[0].T)
    wr = wr.at[N_GROUPS:N_GROUPS + N_EXPERTS].set(w_sub[0].transpose(0, 2, 1).reshape(N_EXPERTS, d))
    br = jnp.zeros((ROUTER_ROWS, 1), F32)
    br = br.at[:N_GROUPS, 0].set(b_group[0]).at[N_GROUPS:N_GROUPS + N_EXPERTS, 0].set(b_sub[0].reshape(-1))
    eid3, wt, rank3, cnt = _router(hp, wr.astype(BF16), br, _pick_tile(seq, 512))
    counts = cnt[:, 0].astype(I32)
    padded = (counts + MOE_BLOCK - 1) // MOE_BLOCK * MOE_BLOCK
    pend = jnp.cumsum(padded)
    pstart = pend - padded
    n_slots = 2 * t + N_EXPERTS * MOE_BLOCK
    nblk = n_slots // MOE_BLOCK
    starts = jnp.arange(nblk, dtype=I32) * MOE_BLOCK
    blk_e = jnp.minimum(jnp.sum((pend[None, :] <= starts[:, None]).astype(I32), axis=1), N_EXPERTS - 1)
    fill_lo = jnp.concatenate([pstart + counts, pend[-1:]])
    fill_hi = jnp.concatenate([pend, jnp.full((1,), n_slots, I32)])
    per = d // 2 // LANES
    onehot = eid3[..., None] == jnp.arange(N_EXPERTS, dtype=I32)
    drow3 = ((rank3 + jnp.sum(jnp.where(onehot, pstart, 0), axis=-1)) * per).reshape(eid3.shape[0], 1, -1)
    xb = _dispatch(fill_lo, fill_hi, drow3, hp, n_slots, per)
    yb = _experts(blk_e, xb, w1[0], w3[0], w2[0])
    out = _combine(drow3, yb, wt.T, x1, mod3, row(ln2_g[0]), row(ln2_b[0]), seq, alpha)
    return out.reshape(batch, seq, d)
```

```python
import functools

import jax
import jax.numpy as jnp
from jax import lax
from jax.experimental import pallas as pl
from jax.experimental.pallas import tpu as pltpu

F32 = jnp.float32
BF16 = jnp.bfloat16
I32 = jnp.int32
U32 = jnp.uint32

LANES = 128
ATTN_HEAD_DIM = 128
RET_HEAD_DIM = 256
DILATIONS = (1, 4, 16)
BAND = 64
QT = 128
KW = QT + 2 * BAND
CHAINS = 32
PERM_ROWS = 256
PERM_UNROLL = 8
ROPE_THETA = 500000.0
ROPE_DIM = ATTN_HEAD_DIM // 4
TOK_PER_ROW = LANES // (ROPE_DIM // 2)
RET_THETA = 10000.0
RET_CHUNK = 128
RET_UNROLL = 4
N_GROUPS = 4
EXPERTS_PER_GROUP = 8
N_EXPERTS = N_GROUPS * EXPERTS_PER_GROUP
MOE_BLOCK = 256
LN_EPS = 1e-5
MASK_VALUE = -1e30
ROUTER_ROWS = 48
HI_MASK = 0xFFFF0000
VMEM_LIMIT = 56 * 1024 * 1024


def _cparams(sem, vmem=VMEM_LIMIT):
    return pltpu.CompilerParams(dimension_semantics=sem, vmem_limit_bytes=vmem)


def _ln(x, g, b):
    mu = jnp.mean(x, axis=-1, keepdims=True)
    xc = x - mu
    var = jnp.mean(xc * xc, axis=-1, keepdims=True)
    return xc * lax.rsqrt(var + LN_EPS) * g + b


def _pack_bf16_pairs(h):
    bits = lax.bitcast_convert_type(h.astype(BF16).astype(F32), U32)
    n = h.shape[1] // 2
    return (bits[:, :n] >> 16) | (bits[:, n:] & jnp.uint32(HI_MASK))


def _store_token_tiles(ref, packed):
    m, n = packed.shape
    per = n // LANES
    for s in range(per):
        ref[pl.ds(s, m, stride=per), :] = packed[:, s * LANES:(s + 1) * LANES]


def _load_token_tiles(ref, m, per, dtype=BF16):
    slabs = [ref[pl.ds(s, m, stride=per), :] for s in range(per)]
    lo = [lax.bitcast_convert_type(p << 16, F32) for p in slabs]
    hi = [lax.bitcast_convert_type(p & jnp.uint32(HI_MASK), F32) for p in slabs]
    return jnp.concatenate(lo + hi, axis=-1).astype(dtype)


def _ada_kernel(c_ref, w_ref, b_ref, o_ref):
    cs = jax.nn.silu(c_ref[...])
    o_ref[...] = jnp.dot(cs.astype(BF16), w_ref[...].astype(BF16), preferred_element_type=F32) + b_ref[...]


def _ada(c8, w_ada, b_ada):
    d, n = w_ada.shape
    tn = min(n, 512)
    return pl.pallas_call(
        _ada_kernel,
        grid=(n // tn,),
        in_specs=[pl.BlockSpec((8, d), lambda j: (0, 0)),
                  pl.BlockSpec((d, tn), lambda j: (0, j)),
                  pl.BlockSpec((1, tn), lambda j: (0, j))],
        out_specs=pl.BlockSpec((8, tn), lambda j: (0, j)),
        out_shape=jax.ShapeDtypeStruct((8, n), F32),
        compiler_params=_cparams(("arbitrary",)),
        name="ada",
    )(c8, w_ada, b_ada)


def _prep_kernel(x_ref, pos_ref, posc_ref, mod_ref, g_ref, b_ref, fac_ref, fr_ref, h_ref, tab_ref):
    xn = _ln(x_ref[...], g_ref[...], b_ref[...])
    h_ref[...] = (xn * (1.0 + mod_ref[0, 1:2, :]) + mod_ref[0, 0:1, :]).astype(BF16)
    half = ROPE_DIM // 2
    angc = posc_ref[...] * fac_ref[...]
    cc, sc = jnp.cos(angc), jnp.sin(angc)
    rows_c = angc.shape[0]
    lane = lax.broadcasted_iota(I32, angc.shape, 1)
    first, second = lane < half, (lane >= half) & (lane < 2 * half)
    for ts in range(TOK_PER_ROW):
        shift = (LANES - ts * half) % LANES
        c = pltpu.roll(cc, shift, 1) if shift else cc
        s = pltpu.roll(sc, shift, 1) if shift else sc
        rows = pl.ds(ts, rows_c, stride=TOK_PER_ROW)
        tab_ref.at[0][rows, :] = jnp.where(first, c, jnp.where(second, pltpu.roll(c, half, 1), 1.0))
        tab_ref.at[1][rows, :] = jnp.where(first, -s, 0.0)
        tab_ref.at[2][rows, :] = jnp.where(second, pltpu.roll(s, half, 1), 0.0)
    angr = pos_ref[...] * fr_ref[...]
    tab_ref[3] = jnp.cos(angr)
    tab_ref[4] = jnp.sin(angr)


def _prep(x2, posb, posc, mod3, ln0_g, ln0_b, fac, fr, seq, tm):
    t, d = x2.shape
    per_b = seq // tm
    vec = lambda n: pl.BlockSpec((1, n), lambda i: (0, 0))
    return pl.pallas_call(
        _prep_kernel,
        grid=(t // tm,),
        in_specs=[pl.BlockSpec((tm, d), lambda i: (i, 0)),
                  pl.BlockSpec((tm, LANES), lambda i: (i, 0)),
                  pl.BlockSpec((tm // TOK_PER_ROW, LANES), lambda i: (i, 0)),
                  pl.BlockSpec((1, 6, d), lambda i: (i // per_b, 0, 0)),
                  vec(d), vec(d), vec(LANES), vec(LANES)],
        out_specs=[pl.BlockSpec((tm, d), lambda i: (i, 0)), pl.BlockSpec((5, tm, LANES), lambda i: (0, i, 0))],
        out_shape=[jax.ShapeDtypeStruct((t, d), BF16), jax.ShapeDtypeStruct((5, t, LANES), F32)],
        compiler_params=_cparams(("arbitrary",)),
        name="prep",
    )(x2, posb, posc, mod3, ln0_g, ln0_b, fac, fr)


def _proj_chunks(h_ref, w_ref, n_sub):
    for hc in range(n_sub):
        yield hc, jnp.dot(h_ref[...], w_ref[:, hc * 2 * LANES:(hc + 1) * 2 * LANES], preferred_element_type=F32)


def _proj_attn_kernel(h_ref, w_ref, tab_ref, z_ref, *, n_sub):
    scale = jnp.where(pl.program_id(0) == 0, ATTN_HEAD_DIM ** -0.5, 1.0).astype(F32)
    ca, s1, s2 = tab_ref[0] * scale, tab_ref[1] * scale, tab_ref[2] * scale
    half = ROPE_DIM // 2
    for hc, res in _proj_chunks(h_ref, w_ref, n_sub):
        for lanes in (slice(0, LANES), slice(LANES, 2 * LANES)):
            x = res[:, lanes]
            r = x * ca + pltpu.roll(x, LANES - half, 1) * s1 + pltpu.roll(x, half, 1) * s2
            z_ref[hc, :, lanes] = r.astype(BF16)


def _proj_ret_kernel(h_ref, w_ref, tab_ref, z_ref, *, n_sub):
    scale = jnp.where(pl.program_id(0) == 1, RET_HEAD_DIM ** -0.5, 1.0).astype(F32)
    cr, sr = tab_ref[3] * scale, tab_ref[4] * scale
    for hc, res in _proj_chunks(h_ref, w_ref, n_sub):
        x1, x2 = res[:, :LANES], res[:, LANES:]
        z_ref[hc, :, :LANES] = (x1 * cr - x2 * sr).astype(BF16)
        z_ref[hc, :, LANES:] = (x2 * cr + x1 * sr).astype(BF16)


def _proj_plain_kernel(h_ref, w_ref, z_ref, *, n_sub):
    for hc, res in _proj_chunks(h_ref, w_ref, n_sub):
        z_ref[hc] = res.astype(BF16)


def _proj(body, col_of, n_seg, h, w_in_b, tabs, tm, name):
    t, d = h.shape
    tn = d // 2
    n_sub = tn // (2 * LANES)
    in_specs = [pl.BlockSpec((tm, d), lambda j, i: (i, 0)),
                pl.BlockSpec((d, tn), lambda j, i: (0, col_of(j)))]
    args = [h, w_in_b]
    if tabs is not None:
        in_specs.append(pl.BlockSpec((5, tm, LANES), lambda j, i: (0, i, 0)))
        args.append(tabs)
    return pl.pallas_call(
        functools.partial(body, n_sub=n_sub),
        grid=(n_seg, t // tm),
        in_specs=in_specs,
        out_specs=pl.BlockSpec((n_sub, tm, 2 * LANES), lambda j, i: (j, i, 0)),
        out_shape=jax.ShapeDtypeStruct((n_seg * n_sub, t, 2 * LANES), BF16),
        compiler_params=_cparams(("arbitrary", "arbitrary")),
        name=name,
    )(*args)


def _attn_tile(q, k, v, bias):
    s = lax.dot_general(q, k, (((1,), (1,)), ((), ())), preferred_element_type=F32) + bias
    if s.shape[1] == 2 * LANES:
        m = jnp.max(jnp.maximum(s[:, :LANES], s[:, LANES:]), axis=-1, keepdims=True)
        p = jnp.exp(s - m)
        den = jnp.sum(p[:, :LANES] + p[:, LANES:], axis=-1, keepdims=True)
    else:
        m = jnp.max(s, axis=-1, keepdims=True)
        p = jnp.exp(s - m)
        den = jnp.sum(p, axis=-1, keepdims=True)
    acc = jnp.dot(p.astype(BF16), v, preferred_element_type=F32)
    return acc, jnp.broadcast_to(m, acc.shape), jnp.broadcast_to(den, acc.shape)


def _attn_kernel(q_ref, k_ref, v_ref, bias_ref, p4_ref, p16_ref, o_ref,
                 acc_ref, m_ref, d_ref, qp_ref, kp_ref, vp_ref, *, seq):
    n_items = seq // QT

    def window(n, length):
        qs = n * QT
        ks = jnp.clip(qs - BAND, 0, length - KW)
        return qs, ks, bias_ref[lax.div(qs - ks, BAND)]

    def perm_body(it, carry):
        for u in range(PERM_UNROLL):
            blk = it * PERM_UNROLL + u
            r0 = pl.multiple_of(blk * PERM_ROWS, PERM_ROWS)
            qk = jnp.concatenate([q_ref[0, 0, pl.ds(r0, PERM_ROWS), :], k_ref[0, 0, pl.ds(r0, PERM_ROWS), :]], axis=-1)
            v = v_ref[0, 0, pl.ds(r0, PERM_ROWS), :]
            for slot, (dil, p_ref) in enumerate(((4, p4_ref), (16, p16_ref))):
                length, w = seq // dil, PERM_ROWS // dil
                qkp = jnp.dot(p_ref[...], qk, preferred_element_type=F32).astype(BF16)
                vp = jnp.dot(p_ref[...], v, preferred_element_type=F32).astype(BF16)
                for r in range(dil):
                    rows = pl.ds(pl.multiple_of(r * length + blk * w, w), w)
                    qp_ref[slot, rows, :] = qkp[r * w:(r + 1) * w, :ATTN_HEAD_DIM]
                    kp_ref[slot, rows, :] = qkp[r * w:(r + 1) * w, ATTN_HEAD_DIM:]
                    vp_ref[slot, rows, :] = vp[r * w:(r + 1) * w, :]
        return carry

    lax.fori_loop(0, seq // PERM_ROWS // PERM_UNROLL, perm_body, 0)

    def dilated_tile(item, dil, slot):
        length = seq // dil
        n, r = lax.div(item, dil), lax.rem(item, dil)
        qs, ks, bias = window(n, length)
        qrow = pl.multiple_of(r * length + qs, QT)
        krow = pl.multiple_of(r * length + ks, BAND)
        out = _attn_tile(qp_ref[slot, pl.ds(qrow, QT), :], kp_ref[slot, pl.ds(krow, KW), :],
                         vp_ref[slot, pl.ds(krow, KW), :], bias)
        return out, pl.ds(qs * dil + r, QT, stride=dil)

    def body16(it, carry):
        for g in range(CHAINS):
            (acc, m, den), rows = dilated_tile(it * CHAINS + g, 16, 1)
            acc_ref[rows, :] = acc
            m_ref[rows, :] = m
            d_ref[rows, :] = den
        return carry

    lax.fori_loop(0, n_items // CHAINS, body16, 0)

    def body4(it, carry):
        for g in range(CHAINS):
            (acc, m, den), rows = dilated_tile(it * CHAINS + g, 4, 0)
            m0 = m_ref[rows, :]
            mn = jnp.maximum(m0, m)
            a, b = jnp.exp(m0 - mn), jnp.exp(m - mn)
            acc_ref[rows, :] = a * acc_ref[rows, :] + b * acc
            d_ref[rows, :] = a * d_ref[rows, :] + b * den
            m_ref[rows, :] = mn
        return carry

    lax.fori_loop(0, n_items // CHAINS, body4, 0)

    def body1(it, carry):
        for g in range(CHAINS):
            qs, ks, bias = window(it * CHAINS + g, seq)
            qs, ks = pl.multiple_of(qs, QT), pl.multiple_of(ks, BAND)
            acc, m, den = _attn_tile(q_ref[0, 0, pl.ds(qs, QT), :], k_ref[0, 0, pl.ds(ks, KW), :],
                                     v_ref[0, 0, pl.ds(ks, KW), :], bias)
            rows = pl.ds(qs, QT)
            m0 = m_ref[rows, :]
            mn = jnp.maximum(m0, m)
            a, b = jnp.exp(m0 - mn), jnp.exp(m - mn)
            num = a * acc_ref[rows, :] + b * acc
            o_ref[0, 0, rows, :] = (num / (a * d_ref[rows, :] + b * den)).astype(BF16)
        return carry

    lax.fori_loop(0, n_items // CHAINS, body1, 0)


def _perm_matrix(dil):
    w = PERM_ROWS // dil
    i = jnp.arange(PERM_ROWS)
    src = (i % w) * dil + i // w
    return (src[:, None] == jnp.arange(PERM_ROWS)[None, :]).astype(BF16)


def _attn(zqk, zv_, batch, seq, n_ah):
    n_sub = n_ah // 2
    zqk = zqk.reshape(zqk.shape[0], batch, seq, 2 * LANES)
    zv_ = zv_.reshape(zv_.shape[0], batch, seq, 2 * LANES)
    blk = (1, 1, seq, ATTN_HEAD_DIM)
    i = jnp.arange(QT)[:, None]
    jj = jnp.arange(KW)[None, :]
    bias = jnp.stack([jnp.where(jnp.abs(i + o * BAND - jj) <= BAND, 0.0, MASK_VALUE) for o in range(3)]).astype(F32)
    const = lambda shape: pl.BlockSpec(shape, lambda b, h: (0,) * len(shape))
    o = pl.pallas_call(
        functools.partial(_attn_kernel, seq=seq),
        grid=(batch, n_ah),
        in_specs=[pl.BlockSpec(blk, lambda b, h: (h // 2, b, 0, h % 2)),
                  pl.BlockSpec(blk, lambda b, h: (n_sub + h // 2, b, 0, h % 2)),
                  pl.BlockSpec(blk, lambda b, h: (h // 2, b, 0, h % 2)),
                  const((3, QT, KW)), const((PERM_ROWS, PERM_ROWS)), const((PERM_ROWS, PERM_ROWS))],
        out_specs=pl.BlockSpec(blk, lambda b, h: (h, b, 0, 0)),
        out_shape=jax.ShapeDtypeStruct((n_ah, batch, seq, ATTN_HEAD_DIM), BF16),
        scratch_shapes=[pltpu.VMEM((seq, LANES), F32)] * 3
        + [pltpu.VMEM((2, seq, ATTN_HEAD_DIM), BF16)] * 3,
        compiler_params=_cparams(("arbitrary", "arbitrary")),
        name="attn",
    )(zqk, zqk, zv_, bias, _perm_matrix(4), _perm_matrix(16))
    return o.reshape(n_ah, batch * seq, ATTN_HEAD_DIM)


def _ret_kernel(lg_ref, q_ref, k_ref, v_ref, g_ref, o_ref, y_ref, sf_ref, sb_ref, dmat_ref, *, n_tiles, tile):
    h = pl.program_id(1)
    t = pl.program_id(2)
    c = RET_CHUNK
    lgf = lg_ref[0, h]
    lgb = lg_ref[1, h]
    col = lax.broadcasted_iota(I32, (c, 1), 0).astype(F32)
    row = lax.broadcasted_iota(I32, (1, c), 1).astype(F32)
    n_iter = tile // c // RET_UNROLL

    @pl.when(t == 0)
    def _():
        sf_ref[...] = jnp.zeros_like(sf_ref)
        sb_ref[...] = jnp.zeros_like(sb_ref)
        diff = (lax.broadcasted_iota(I32, (c, c), 0) - lax.broadcasted_iota(I32, (c, c), 1)).astype(F32)
        dmat_ref[...] = jnp.where(diff >= 0, jnp.exp(lgf * jnp.maximum(diff, 0.0)), 0.0) \
            + jnp.where(diff < 0, jnp.exp(lgb * jnp.maximum(-diff, 0.0)), 0.0)

    def chunk(r0):
        rows = pl.ds(pl.multiple_of(r0, c), c)
        return rows, q_ref[0, rows, :], k_ref[0, rows, :], v_ref[0, rows, :]

    @pl.when(t < n_tiles)
    def _():
        xi = jnp.exp(lgf * (col + 1.0))
        zeta = jnp.exp(lgf * (c - 1.0 - row))
        cdec = jnp.exp(lgf * jnp.full((1, 1), float(c), F32))
        base = t * tile

        def body(it, carry):
            state = sf_ref[...]
            for u in range(RET_UNROLL):
                r0 = (it * RET_UNROLL + u) * c
                _, q, k, v = chunk(r0)
                s = lax.dot_general(q, k, (((1,), (1,)), ((), ())), preferred_element_type=F32) * dmat_ref[...]
                y = jnp.dot(s.astype(BF16), v, preferred_element_type=F32)
                y = y + jnp.dot(q, state.astype(BF16), preferred_element_type=F32) * xi
                kt = (k.astype(F32).T * zeta).astype(BF16)
                state = state * cdec + jnp.dot(kt, v, preferred_element_type=F32)
                y_ref[pl.ds(pl.multiple_of(base + r0, c), c), :] = y
            sf_ref[...] = state
            return carry

        lax.fori_loop(0, n_iter, body, 0)

    @pl.when(t >= n_tiles)
    def _():
        xi = jnp.exp(lgb * (c - col))
        zeta = jnp.exp(lgb * row)
        cdec = jnp.exp(lgb * jnp.full((1, 1), float(c), F32))
        base = (2 * n_tiles - 1 - t) * tile

        def body(it, carry):
            state = sb_ref[...]
            for u in range(RET_UNROLL):
                r0 = (tile // c - 1 - (it * RET_UNROLL + u)) * c
                rows, q, k, v = chunk(r0)
                y = y_ref[pl.ds(pl.multiple_of(base + r0, c), c), :]
                y = y + jnp.dot(q, state.astype(BF16), preferred_element_type=F32) * xi
                kt = (k.astype(F32).T * zeta).astype(BF16)
                state = state * cdec + jnp.dot(kt, v, preferred_element_type=F32)
                mu = jnp.mean(y, axis=-1, keepdims=True)
                yc = y - mu
                var = jnp.mean(yc * yc, axis=-1, keepdims=True)
                yn = yc * lax.rsqrt(var + LN_EPS)
                gate = jax.nn.silu(g_ref[0, rows, :].astype(F32))
                o_ref[0, rows, :] = (gate * yn).astype(BF16)
            sb_ref[...] = state
            return carry

        lax.fori_loop(0, n_iter, body, 0)


def _ret(zqk, zvg, lg, batch, seq, n_rh, tile):
    n_tiles = seq // tile
    per_b = seq // tile

    def rows(t):
        return jnp.where(t < n_tiles, t, 2 * n_tiles - 1 - t)

    def spec(seg):
        return pl.BlockSpec((1, tile, RET_HEAD_DIM), lambda b, h, t: (seg * n_rh + h, b * per_b + rows(t), 0))

    return pl.pallas_call(
        functools.partial(_ret_kernel, n_tiles=n_tiles, tile=tile),
        grid=(batch, n_rh, 2 * n_tiles),
        in_specs=[pl.BlockSpec(memory_space=pltpu.SMEM), spec(0), spec(1), spec(1), spec(2)],
        out_specs=pl.BlockSpec((1, tile, RET_HEAD_DIM),
                               lambda b, h, t: (h, b * per_b + jnp.where(t < n_tiles, n_tiles - 1, 2 * n_tiles - 1 - t), 0)),
        out_shape=jax.ShapeDtypeStruct((n_rh, batch * seq, RET_HEAD_DIM), BF16),
        scratch_shapes=[pltpu.VMEM((seq, RET_HEAD_DIM), F32),
                        pltpu.VMEM((RET_HEAD_DIM, RET_HEAD_DIM), F32),
                        pltpu.VMEM((RET_HEAD_DIM, RET_HEAD_DIM), F32),
                        pltpu.VMEM((RET_CHUNK, RET_CHUNK), F32)],
        compiler_params=_cparams(("arbitrary", "arbitrary", "arbitrary")),
        name="ret",
    )(lg, zqk, zqk, zvg, zvg)


def _outproj_kernel(a_ref, r_ref, x_ref, mod_ref, g0_ref, b0_ref, g1_ref, b1_ref, w_ref, x1_ref, hp_ref,
                    *, n_ah, n_rh, alpha):
    mix = jnp.concatenate([a_ref[h] for h in range(n_ah)] + [r_ref[h] for h in range(n_rh)], axis=-1)
    acc = jnp.dot(mix, w_ref[...], preferred_element_type=F32)
    xn = _ln(x_ref[...], g0_ref[...], b0_ref[...])
    y = alpha * xn + (1.0 + mod_ref[0, 2:3, :]) * acc
    x1 = _ln(y, g1_ref[...], b1_ref[...])
    x1_ref[...] = x1
    _store_token_tiles(hp_ref, _pack_bf16_pairs(x1 * (1.0 + mod_ref[0, 4:5, :]) + mod_ref[0, 3:4, :]))


def _outproj(attn, r, x2, mod3, g0, b0, g1, b1, w_out_b, seq, tm, alpha):
    t, d = x2.shape
    n_ah, n_rh = attn.shape[0], r.shape[0]
    per_b = seq // tm
    per = d // 2 // LANES
    row = lambda i: (i, 0)
    vec = pl.BlockSpec((1, d), lambda i: (0, 0))
    return pl.pallas_call(
        functools.partial(_outproj_kernel, n_ah=n_ah, n_rh=n_rh, alpha=alpha),
        grid=(t // tm,),
        in_specs=[pl.BlockSpec((n_ah, tm, ATTN_HEAD_DIM), lambda i: (0, i, 0)),
                  pl.BlockSpec((n_rh, tm, RET_HEAD_DIM), lambda i: (0, i, 0)),
                  pl.BlockSpec((tm, d), row),
                  pl.BlockSpec((1, 6, d), lambda i: (i // per_b, 0, 0)),
                  vec, vec, vec, vec,
                  pl.BlockSpec((d, d), lambda i: (0, 0))],
        out_specs=[pl.BlockSpec((tm, d), row), pl.BlockSpec((tm * per, LANES), row)],
        out_shape=[jax.ShapeDtypeStruct((t, d), F32), jax.ShapeDtypeStruct((t * per, LANES), U32)],
        compiler_params=_cparams(("arbitrary",)),
        name="outproj",
    )(attn, r, x2, mod3, g0, b0, g1, b1, w_out_b)


def _first_argmax(rows):
    best, idx = rows[0], jnp.zeros(rows[0].shape, I32)
    for e in range(1, len(rows)):
        better = rows[e] > best
        idx = jnp.where(better, e, idx)
        best = jnp.maximum(best, rows[e])
    return best, idx


def _router_kernel(h_ref, w_ref, b_ref, u_ref, eid_ref, wt_ref, rank_ref, cnt_ref, carry_ref, *, per):
    i = pl.program_id(0)

    @pl.when(i == 0)
    def _():
        carry_ref[...] = jnp.zeros_like(carry_ref)

    tm = eid_ref.shape[2]
    lt = lax.dot_general(w_ref[...], _load_token_tiles(h_ref, tm, per), (((1,), (1,)), ((), ())),
                         preferred_element_type=F32) + b_ref[...]
    grow = [lt[g:g + 1, :] for g in range(N_GROUPS)]
    gmax, gsel = _first_argmax(grow)
    gsum = grow[0] * 0.0
    for g in range(N_GROUPS):
        gsum = gsum + jnp.exp(grow[g] - gmax)
    pg = 1.0 / gsum
    srow = []
    for e in range(EXPERTS_PER_GROUP):
        r = lt[N_GROUPS + e:N_GROUPS + e + 1, :]
        for g in range(1, N_GROUPS):
            o = N_GROUPS + g * EXPERTS_PER_GROUP + e
            r = jnp.where(gsel == g, lt[o:o + 1, :], r)
        srow.append(r)
    v1, i1 = _first_argmax(srow)
    v2, i2 = _first_argmax([jnp.where(i1 == e, -jnp.inf, srow[e]) for e in range(EXPERTS_PER_GROUP)])
    e2 = jnp.exp(v2 - v1)
    den = 1.0 + e2
    wt_ref[0:1, :] = (1.0 / den) * pg
    wt_ref[1:2, :] = (e2 / den) * pg
    eid0 = gsel * EXPERTS_PER_GROUP + i1
    eid1 = gsel * EXPERTS_PER_GROUP + i2
    eid_ref[0, 0:1, :] = eid0
    eid_ref[0, 1:2, :] = eid1
    erow = lax.broadcasted_iota(I32, (N_EXPERTS, tm), 0)
    oh0 = (erow == eid0).astype(F32)
    oh1 = (erow == eid1).astype(F32)
    oh = oh0 + oh1
    incl = jnp.dot(oh.astype(BF16), u_ref[...], preferred_element_type=F32)
    before = carry_ref[:, 0:1] + incl - oh
    rank_ref[0, 0:1, :] = jnp.sum(oh0 * before, axis=0, keepdims=True).astype(I32)
    rank_ref[0, 1:2, :] = jnp.sum(oh1 * before, axis=0, keepdims=True).astype(I32)
    carry = carry_ref[...] + jnp.sum(oh, axis=1, keepdims=True)
    carry_ref[...] = carry
    cnt_ref[...] = carry


def _router(hp, wr, br, tm):
    d = wr.shape[1]
    per = d // 2 // LANES
    t = hp.shape[0] // per
    tri = (lax.broadcasted_iota(I32, (tm, tm), 0) <= lax.broadcasted_iota(I32, (tm, tm), 1)).astype(BF16)
    tile3 = pl.BlockSpec((1, 2, tm), lambda i: (i, 0, 0))
    return pl.pallas_call(
        functools.partial(_router_kernel, per=per),
        grid=(t // tm,),
        in_specs=[pl.BlockSpec((tm * per, LANES), lambda i: (i, 0)),
                  pl.BlockSpec((ROUTER_ROWS, d), lambda i: (0, 0)),
                  pl.BlockSpec((ROUTER_ROWS, 1), lambda i: (0, 0)),
                  pl.BlockSpec((tm, tm), lambda i: (0, 0))],
        out_specs=[tile3, pl.BlockSpec((2, tm), lambda i: (0, i)), tile3,
                   pl.BlockSpec((N_EXPERTS, LANES), lambda i: (0, 0))],
        out_shape=[jax.ShapeDtypeStruct((t // tm, 2, tm), I32), jax.ShapeDtypeStruct((2, t), F32),
                   jax.ShapeDtypeStruct((t // tm, 2, tm), I32), jax.ShapeDtypeStruct((N_EXPERTS, LANES), F32)],
        scratch_shapes=[pltpu.VMEM((N_EXPERTS, LANES), F32)],
        compiler_params=_cparams(("arbitrary",)),
        name="router",
    )(hp, wr, br, tri)


def _tile_copy(src, s_row, dst, d_row, per, sem):
    return pltpu.make_async_copy(src.at[pl.ds(pl.multiple_of(s_row, per), per), :],
                                 dst.at[pl.ds(pl.multiple_of(d_row, per), per), :], sem)


def _dispatch_kernel(lo_ref, hi_ref, dst_ref, h_ref, xb_hbm, stage, zero, sem, *, per):
    i = pl.program_id(0)
    n = pl.num_programs(0)
    slot = i % 2
    tm = dst_ref.shape[2] // 2

    def wait_tile(s):
        pltpu.make_async_copy(xb_hbm.at[pl.ds(0, 2 * tm * per), :], xb_hbm.at[pl.ds(0, 2 * tm * per), :], sem.at[s]).wait()

    @pl.when(i >= 2)
    def _():
        wait_tile(slot)

    stage[slot] = h_ref[...]

    def body(r, carry):
        for k in range(2):
            _tile_copy(stage.at[slot], r * per, xb_hbm, dst_ref[0, 0, k * tm + r], per, sem.at[slot]).start(priority=k)
        return carry

    lax.fori_loop(0, tm, body, 0, unroll=8)

    @pl.when(i == n - 1)
    def _():
        @pl.when(n >= 2)
        def _():
            wait_tile(1 - slot)
        wait_tile(slot)
        zero[...] = jnp.zeros_like(zero)

        def fill(e, carry):
            def one(s, c):
                _tile_copy(zero, 0, xb_hbm, s * per, per, sem.at[2]).start()
                return c
            lax.fori_loop(lo_ref[e], hi_ref[e], one, 0)

            def one_wait(s, c):
                _tile_copy(zero, 0, xb_hbm, s * per, per, sem.at[2]).wait()
                return c
            lax.fori_loop(lo_ref[e], hi_ref[e], one_wait, 0)
            return carry

        lax.fori_loop(0, N_EXPERTS + 1, fill, 0)


def _dispatch(fill_lo, fill_hi, drow3, hp, n_slots, per):
    n, tm = drow3.shape[0], drow3.shape[2] // 2
    smem3 = pl.BlockSpec((1, 1, 2 * tm), lambda i, *_: (i, 0, 0), memory_space=pltpu.SMEM)
    grid_spec = pltpu.PrefetchScalarGridSpec(
        num_scalar_prefetch=2,
        grid=(n,),
        in_specs=[smem3, pl.BlockSpec((tm * per, LANES), lambda i, *_: (i, 0))],
        out_specs=pl.BlockSpec(memory_space=pl.ANY),
        scratch_shapes=[pltpu.VMEM((2, tm * per, LANES), U32), pltpu.VMEM((per, LANES), U32),
                        pltpu.SemaphoreType.DMA((3,))])
    return pl.pallas_call(
        functools.partial(_dispatch_kernel, per=per),
        grid_spec=grid_spec,
        out_shape=jax.ShapeDtypeStruct((n_slots * per, LANES), U32),
        compiler_params=_cparams(("arbitrary",)),
        name="dispatch",
    )(fill_lo, fill_hi, drow3, hp)


def _expert_kernel(be_ref, first_ref, par_ref, nxt_ref, has_ref, x_ref, w1_hbm, w3_hbm, w2_hbm, y_ref,
                   wf1, wf3, wf2, w1b, w3b, w2b, sem, *, per):
    i = pl.program_id(0)

    def weight_copies(e, s):
        return [pltpu.make_async_copy(src.at[e], dst.at[s], sem.at[s])
                for src, dst in ((w1_hbm, wf1), (w3_hbm, wf3), (w2_hbm, wf2))]

    @pl.when(i == 0)
    def _():
        for cp in weight_copies(be_ref[0], 0):
            cp.start()

    @pl.when(first_ref[i] == 1)
    def _():
        s = par_ref[i]
        for cp in weight_copies(be_ref[i], s):
            cp.wait()

        @pl.when(has_ref[i] == 1)
        def _():
            for cp in weight_copies(nxt_ref[i], 1 - s):
                cp.start()

        w1b[...] = wf1[s].astype(BF16)
        w3b[...] = wf3[s].astype(BF16)
        w2b[...] = wf2[s].astype(BF16)

    x = _load_token_tiles(x_ref, MOE_BLOCK, per)
    a = jnp.dot(x, w1b[...], preferred_element_type=F32)
    b = jnp.dot(x, w3b[...], preferred_element_type=F32)
    mid = (jax.nn.silu(a) * b).astype(BF16)
    _store_token_tiles(y_ref, _pack_bf16_pairs(jnp.dot(mid, w2b[...], preferred_element_type=F32)))


def _experts(blk_e, xb, w1, w3, w2):
    d, ff = w1.shape[1], w1.shape[2]
    per = d // 2 // LANES
    nblk = xb.shape[0] // per // MOE_BLOCK
    rows = pl.BlockSpec((MOE_BLOCK * per, LANES), lambda i, *_: (i, 0))
    first = jnp.concatenate([jnp.ones((1,), I32), (blk_e[1:] != blk_e[:-1]).astype(I32)])
    parity = (jnp.cumsum(first) - 1) % 2
    nxt = jnp.min(jnp.where(blk_e[None, :] > blk_e[:, None], blk_e[None, :], N_EXPERTS), axis=1)
    has_next = (nxt < N_EXPERTS).astype(I32)
    nxt = jnp.minimum(nxt, N_EXPERTS - 1)
    grid_spec = pltpu.PrefetchScalarGridSpec(
        num_scalar_prefetch=5,
        grid=(nblk,),
        in_specs=[rows] + [pl.BlockSpec(memory_space=pl.ANY)] * 3,
        out_specs=rows,
        scratch_shapes=[pltpu.VMEM((2, d, ff), F32), pltpu.VMEM((2, d, ff), F32), pltpu.VMEM((2, ff, d), F32),
                        pltpu.VMEM((d, ff), BF16), pltpu.VMEM((d, ff), BF16), pltpu.VMEM((ff, d), BF16),
                        pltpu.SemaphoreType.DMA((2,))])
    return pl.pallas_call(
        functools.partial(_expert_kernel, per=per),
        grid_spec=grid_spec,
        out_shape=jax.ShapeDtypeStruct(xb.shape, U32),
        compiler_params=_cparams(("arbitrary",)),
        name="experts",
    )(blk_e, first, parity.astype(I32), nxt.astype(I32), has_next, xb, w1, w3, w2)


def _combine_kernel(src_ref, srcn_ref, y_hbm, wt_ref, x1_ref, mod_ref, g_ref, b_ref, o_ref, ybuf, sem, *, alpha, per):
    i = pl.program_id(0)
    n = pl.num_programs(0)
    slot = i % 2
    tm = x1_ref.shape[0]

    def wait(s):
        for k in range(2):
            pltpu.make_async_copy(y_hbm.at[pl.ds(0, tm * per), :], ybuf.at[s, k], sem.at[s]).wait()

    @pl.when(i == 0)
    def _():
        def body(r, carry):
            for k in range(2):
                _tile_copy(y_hbm, src_ref[0, 0, k * tm + r], ybuf.at[0, k], r * per, per, sem.at[0]).start(priority=k)
            return carry
        lax.fori_loop(0, tm, body, 0, unroll=8)

    for r in range(tm):
        for k in range(2):
            _tile_copy(y_hbm, srcn_ref[0, 0, k * tm + r], ybuf.at[1 - slot, k], r * per, per, sem.at[1 - slot]).start(priority=k)

    wait(slot)
    wt = wt_ref[...]
    y0 = _load_token_tiles(ybuf.at[slot, 0], tm, per, F32)
    y1 = _load_token_tiles(ybuf.at[slot, 1], tm, per, F32)
    ffn = wt[:, 0:1] * y0 + wt[:, 1:2] * y1
    y = alpha * x1_ref[...] + (1.0 + mod_ref[0, 5:6, :]) * ffn
    o_ref[...] = _ln(y, g_ref[...], b_ref[...])

    @pl.when(i == n - 1)
    def _():
        wait(1 - slot)


def _combine(drow3, yb, wt_t, x1, mod3, g2, b2, seq, alpha):
    t, d = x1.shape
    per = d // 2 // LANES
    n, tm = drow3.shape[0], drow3.shape[2] // 2
    per_b = seq // tm
    cur = pl.BlockSpec((1, 1, 2 * tm), lambda i: (i, 0, 0), memory_space=pltpu.SMEM)
    nxt = pl.BlockSpec((1, 1, 2 * tm), lambda i: (jnp.minimum(i + 1, n - 1), 0, 0), memory_space=pltpu.SMEM)
    vec = pl.BlockSpec((1, d), lambda i: (0, 0))
    return pl.pallas_call(
        functools.partial(_combine_kernel, alpha=alpha, per=per),
        grid=(n,),
        in_specs=[cur, nxt,
                  pl.BlockSpec(memory_space=pl.ANY),
                  pl.BlockSpec((tm, 2), lambda i: (i, 0)),
                  pl.BlockSpec((tm, d), lambda i: (i, 0)),
                  pl.BlockSpec((1, 6, d), lambda i: (i // per_b, 0, 0)),
                  vec, vec],
        out_specs=pl.BlockSpec((tm, d), lambda i: (i, 0)),
        out_shape=jax.ShapeDtypeStruct((t, d), F32),
        scratch_shapes=[pltpu.VMEM((2, 2, tm * per, LANES), U32), pltpu.SemaphoreType.DMA((2,))],
        compiler_params=_cparams(("arbitrary",)),
        name="combine",
    )(drow3, drow3, yb, wt_t, x1, mod3, g2, b2)


def _pick_tile(n, want):
    tm = min(n, want)
    assert n % tm == 0
    return tm


def kernel(x, c, positions, ln0_g, ln0_b, w_ada, b_ada, w_in, w_out, ret_log_decay_f, ret_log_decay_b,
           ln1_g, ln1_b, w_group, b_group, w_sub, b_sub, w1, w3, w2, ln2_g, ln2_b):
    batch, seq, d = x.shape
    depth = w_ada.shape[0]
    t = batch * seq
    n_ah = d // 2 // ATTN_HEAD_DIM
    n_rh = d // 2 // RET_HEAD_DIM
    assert depth == 1 and d % (2 * RET_HEAD_DIM) == 0 and batch <= 8
    assert seq % (QT * CHAINS) == 0 and seq % (max(DILATIONS) * PERM_ROWS) == 0 and seq // max(DILATIONS) >= KW
    alpha = (2 * depth) ** 0.25

    inv_rope = ROPE_THETA ** (-jnp.arange(0, ROPE_DIM, 2, dtype=F32) / ROPE_DIM)
    inv_ret = RET_THETA ** (-jnp.linspace(0.0, 1.0, RET_HEAD_DIM // 2, dtype=F32))
    fac = jnp.tile(inv_rope, TOK_PER_ROW).reshape(1, LANES)
    fr = inv_ret.reshape(1, LANES)
    posb = jnp.broadcast_to(positions.astype(F32).reshape(t, 1), (t, LANES))
    posc = jnp.repeat(positions.astype(F32).reshape(t // TOK_PER_ROW, TOK_PER_ROW), ROPE_DIM // 2, axis=1)
    c8 = jnp.zeros((8, d), F32).at[:batch].set(c)
    row = lambda v: v.reshape(1, d)

    xs = x.reshape(t, d)
    mod = _ada(c8, w_ada[0], b_ada[0].reshape(1, -1))
    mod3 = mod[:batch].reshape(batch, 6, d)
    h, tabs = _prep(xs, posb, posc, mod3, row(ln0_g), row(ln0_b), fac, fr, seq, _pick_tile(seq, 512))
    w_in_b = w_in[0].astype(BF16)
    tmp = _pick_tile(seq, 2048)
    z_aqk = _proj(_proj_attn_kernel, lambda j: j, 2, h, w_in_b, tabs, tmp, "proj_attn")
    z_rqk = _proj(_proj_ret_kernel, lambda j: 3 + j, 2, h, w_in_b, tabs, _pick_tile(seq, 1024), "proj_ret")
    z_pl = _proj(_proj_plain_kernel, lambda j: 2 + 3 * jnp.minimum(j, 1) + jnp.maximum(j - 1, 0), 3,
                 h, w_in_b, None, tmp, "proj_plain")
    attn = _attn(z_aqk, z_pl, batch, seq, n_ah)
    lg = jnp.stack([ret_log_decay_f[0], ret_log_decay_b[0]]).astype(F32)
    r = _ret(z_rqk, z_pl, lg, batch, seq, n_rh, _pick_tile(seq, 2048))
    x1, hp = _outproj(attn, r, xs, mod3, row(ln0_g), row(ln0_b), row(ln1_g[0]), row(ln1_b[0]),
                      w_out[0].astype(BF16), seq, _pick_tile(seq, 512), alpha)
    wr = jnp.zeros((ROUTER_ROWS, d), F32)
    wr = wr.at[:N_GROUPS].set(w_group[0].T)
    wr = wr.at[N_GROUPS:N_GROUPS + N_EXPERTS].set(w_sub[0].transpose(0, 2, 1).reshape(N_EXPERTS, d))
    br = jnp.zeros((ROUTER_ROWS, 1), F32)
    br = br.at[:N_GROUPS, 0].set(b_group[0]).at[N_GROUPS:N_GROUPS + N_EXPERTS, 0].set(b_sub[0].reshape(-1))
    eid3, wt, rank3, cnt = _router(hp, wr.astype(BF16), br, _pick_tile(seq, 512))
    counts = cnt[:, 0].astype(I32)
    padded = (counts + MOE_BLOCK - 1) // MOE_BLOCK * MOE_BLOCK
    pend = jnp.cumsum(padded)
    pstart = pend - padded
    n_slots = 2 * t + N_EXPERTS * MOE_BLOCK
    nblk = n_slots // MOE_BLOCK
    starts = jnp.arange(nblk, dtype=I32) * MOE_BLOCK
    blk_e = jnp.minimum(jnp.sum((pend[None, :] <= starts[:, None]).astype(I32), axis=1), N_EXPERTS - 1)
    fill_lo = jnp.concatenate([pstart + counts, pend[-1:]])
    fill_hi = jnp.concatenate([pend, jnp.full((1,), n_slots, I32)])
    per = d // 2 // LANES
    onehot = eid3[..., None] == jnp.arange(N_EXPERTS, dtype=I32)
    drow3 = ((rank3 + jnp.sum(jnp.where(onehot, pstart, 0), axis=-1)) * per).reshape(eid3.shape[0], 1, -1)
    xb = _dispatch(fill_lo, fill_hi, drow3, hp, n_slots, per)
    yb = _experts(blk_e, xb, w1[0], w3[0], w2[0])
    out = _combine(drow3, yb, wt.T, x1, mod3, row(ln2_g[0]), row(ln2_b[0]), seq, alpha)
    return out.reshape(batch, seq, d)
```

```python
import functools

import jax
import jax.numpy as jnp
from jax import lax
from jax.experimental import pallas as pl
from jax.experimental.pallas import tpu as pltpu

F32 = jnp.float32
BF16 = jnp.bfloat16
I32 = jnp.int32
U32 = jnp.uint32

LANES = 128
ATTN_HEAD_DIM = 128
RET_HEAD_DIM = 256
DILATIONS = (1, 4, 16)
BAND = 64
QT = 128
KW = QT + 2 * BAND
CHAINS = 32
PERM_ROWS = 256
PERM_UNROLL = 8
ROPE_THETA = 500000.0
ROPE_DIM = ATTN_HEAD_DIM // 4
TOK_PER_ROW = LANES // (ROPE_DIM // 2)
RET_THETA = 10000.0
RET_CHUNK = 128
RET_UNROLL = 4
N_GROUPS = 4
EXPERTS_PER_GROUP = 8
N_EXPERTS = N_GROUPS * EXPERTS_PER_GROUP
MOE_BLOCK = 256
LN_EPS = 1e-5
MASK_VALUE = -1e30
ROUTER_ROWS = 48
HI_MASK = 0xFFFF0000
VMEM_LIMIT = 56 * 1024 * 1024


def _cparams(sem, vmem=VMEM_LIMIT):
    return pltpu.CompilerParams(dimension_semantics=sem, vmem_limit_bytes=vmem)


def _ln(x, g, b):
    mu = jnp.mean(x, axis=-1, keepdims=True)
    xc = x - mu
    var = jnp.mean(xc * xc, axis=-1, keepdims=True)
    return xc * lax.rsqrt(var + LN_EPS) * g + b


def _pack_bf16_pairs(h):
    bits = lax.bitcast_convert_type(h.astype(BF16).astype(F32), U32)
    n = h.shape[1] // 2
    return (bits[:, :n] >> 16) | (bits[:, n:] & jnp.uint32(HI_MASK))


def _store_token_tiles(ref, packed):
    m, n = packed.shape
    per = n // LANES
    for s in range(per):
        ref[pl.ds(s, m, stride=per), :] = packed[:, s * LANES:(s + 1) * LANES]


def _load_token_tiles(ref, m, per, dtype=BF16):
    slabs = [ref[pl.ds(s, m, stride=per), :] for s in range(per)]
    lo = [lax.bitcast_convert_type(p << 16, F32) for p in slabs]
    hi = [lax.bitcast_convert_type(p & jnp.uint32(HI_MASK), F32) for p in slabs]
    return jnp.concatenate(lo + hi, axis=-1).astype(dtype)


def _ada_kernel(c_ref, w_ref, b_ref, o_ref):
    cs = jax.nn.silu(c_ref[...])
    o_ref[...] = jnp.dot(cs.astype(BF16), w_ref[...].astype(BF16), preferred_element_type=F32) + b_ref[...]


def _ada(c8, w_ada, b_ada):
    d, n = w_ada.shape
    tn = min(n, 512)
    return pl.pallas_call(
        _ada_kernel,
        grid=(n // tn,),
        in_specs=[pl.BlockSpec((8, d), lambda j: (0, 0)),
                  pl.BlockSpec((d, tn), lambda j: (0, j)),
                  pl.BlockSpec((1, tn), lambda j: (0, j))],
        out_specs=pl.BlockSpec((8, tn), lambda j: (0, j)),
        out_shape=jax.ShapeDtypeStruct((8, n), F32),
        compiler_params=_cparams(("arbitrary",)),
        name="ada",
    )(c8, w_ada, b_ada)


def _prep_kernel(x_ref, pos_ref, posc_ref, mod_ref, g_ref, b_ref, fac_ref, fr_ref, h_ref, tab_ref):
    xn = _ln(x_ref[...], g_ref[...], b_ref[...])
    h_ref[...] = (xn * (1.0 + mod_ref[0, 1:2, :]) + mod_ref[0, 0:1, :]).astype(BF16)
    half = ROPE_DIM // 2
    angc = posc_ref[...] * fac_ref[...]
    cc, sc = jnp.cos(angc), jnp.sin(angc)
    rows_c = angc.shape[0]
    lane = lax.broadcasted_iota(I32, angc.shape, 1)
    first, second = lane < half, (lane >= half) & (lane < 2 * half)
    for ts in range(TOK_PER_ROW):
        shift = (LANES - ts * half) % LANES
        c = pltpu.roll(cc, shift, 1) if shift else cc
        s = pltpu.roll(sc, shift, 1) if shift else sc
        rows = pl.ds(ts, rows_c, stride=TOK_PER_ROW)
        tab_ref.at[0][rows, :] = jnp.where(first, c, jnp.where(second, pltpu.roll(c, half, 1), 1.0))
        tab_ref.at[1][rows, :] = jnp.where(first, -s, 0.0)
        tab_ref.at[2][rows, :] = jnp.where(second, pltpu.roll(s, half, 1), 0.0)
    angr = pos_ref[...] * fr_ref[...]
    tab_ref[3] = jnp.cos(angr)
    tab_ref[4] = jnp.sin(angr)


def _prep(x2, posb, posc, mod3, ln0_g, ln0_b, fac, fr, seq, tm):
    t, d = x2.shape
    per_b = seq // tm
    vec = lambda n: pl.BlockSpec((1, n), lambda i: (0, 0))
    return pl.pallas_call(
        _prep_kernel,
        grid=(t // tm,),
        in_specs=[pl.BlockSpec((tm, d), lambda i: (i, 0)),
                  pl.BlockSpec((tm, LANES), lambda i: (i, 0)),
                  pl.BlockSpec((tm // TOK_PER_ROW, LANES), lambda i: (i, 0)),
                  pl.BlockSpec((1, 6, d), lambda i: (i // per_b, 0, 0)),
                  vec(d), vec(d), vec(LANES), vec(LANES)],
        out_specs=[pl.BlockSpec((tm, d), lambda i: (i, 0)), pl.BlockSpec((5, tm, LANES), lambda i: (0, i, 0))],
        out_shape=[jax.ShapeDtypeStruct((t, d), BF16), jax.ShapeDtypeStruct((5, t, LANES), F32)],
        compiler_params=_cparams(("arbitrary",)),
        name="prep",
    )(x2, posb, posc, mod3, ln0_g, ln0_b, fac, fr)


def _proj_chunks(h_ref, w_ref, n_sub):
    for hc in range(n_sub):
        yield hc, jnp.dot(h_ref[...], w_ref[:, hc * 2 * LANES:(hc + 1) * 2 * LANES], preferred_element_type=F32)


def _proj_attn_kernel(h_ref, w_ref, tab_ref, z_ref, *, n_sub):
    scale = jnp.where(pl.program_id(0) == 0, ATTN_HEAD_DIM ** -0.5, 1.0).astype(F32)
    ca, s1, s2 = tab_ref[0] * scale, tab_ref[1] * scale, tab_ref[2] * scale
    half = ROPE_DIM // 2
    for hc, res in _proj_chunks(h_ref, w_ref, n_sub):
        for lanes in (slice(0, LANES), slice(LANES, 2 * LANES)):
            x = res[:, lanes]
            r = x * ca + pltpu.roll(x, LANES - half, 1) * s1 + pltpu.roll(x, half, 1) * s2
            z_ref[hc, :, lanes] = r.astype(BF16)


def _proj_ret_kernel(h_ref, w_ref, tab_ref, z_ref, *, n_sub):
    scale = jnp.where(pl.program_id(0) == 1, RET_HEAD_DIM ** -0.5, 1.0).astype(F32)
    cr, sr = tab_ref[3] * scale, tab_ref[4] * scale
    for hc, res in _proj_chunks(h_ref, w_ref, n_sub):
        x1, x2 = res[:, :LANES], res[:, LANES:]
        z_ref[hc, :, :LANES] = (x1 * cr - x2 * sr).astype(BF16)
        z_ref[hc, :, LANES:] = (x2 * cr + x1 * sr).astype(BF16)


def _proj_plain_kernel(h_ref, w_ref, z_ref, *, n_sub):
    for hc, res in _proj_chunks(h_ref, w_ref, n_sub):
        z_ref[hc] = res.astype(BF16)


def _proj(body, col_of, n_seg, h, w_in_b, tabs, tm, name):
    t, d = h.shape
    tn = d // 2
    n_sub = tn // (2 * LANES)
    in_specs = [pl.BlockSpec((tm, d), lambda j, i: (i, 0)),
                pl.BlockSpec((d, tn), lambda j, i: (0, col_of(j)))]
    args = [h, w_in_b]
    if tabs is not None:
        in_specs.append(pl.BlockSpec((5, tm, LANES), lambda j, i: (0, i, 0)))
        args.append(tabs)
    return pl.pallas_call(
        functools.partial(body, n_sub=n_sub),
        grid=(n_seg, t // tm),
        in_specs=in_specs,
        out_specs=pl.BlockSpec((n_sub, tm, 2 * LANES), lambda j, i: (j, i, 0)),
        out_shape=jax.ShapeDtypeStruct((n_seg * n_sub, t, 2 * LANES), BF16),
        compiler_params=_cparams(("arbitrary", "arbitrary")),
        name=name,
    )(*args)


def _attn_tile(q, k, v, bias):
    s = lax.dot_general(q, k, (((1,), (1,)), ((), ())), preferred_element_type=F32) + bias
    if s.shape[1] == 2 * LANES:
        m = jnp.max(jnp.maximum(s[:, :LANES], s[:, LANES:]), axis=-1, keepdims=True)
        p = jnp.exp(s - m)
        den = jnp.sum(p[:, :LANES] + p[:, LANES:], axis=-1, keepdims=True)
    else:
        m = jnp.max(s, axis=-1, keepdims=True)
        p = jnp.exp(s - m)
        den = jnp.sum(p, axis=-1, keepdims=True)
    acc = jnp.dot(p.astype(BF16), v, preferred_element_type=F32)
    return acc, jnp.broadcast_to(m, acc.shape), jnp.broadcast_to(den, acc.shape)


def _attn_kernel(q_ref, k_ref, v_ref, bias_ref, p4_ref, p16_ref, o_ref,
                 acc_ref, m_ref, d_ref, qp_ref, kp_ref, vp_ref, *, seq):
    n_items = seq // QT

    def window(n, length):
        qs = n * QT
        ks = jnp.clip(qs - BAND, 0, length - KW)
        return qs, ks, bias_ref[lax.div(qs - ks, BAND)]

    def perm_body(it, carry):
        for u in range(PERM_UNROLL):
            blk = it * PERM_UNROLL + u
            r0 = pl.multiple_of(blk * PERM_ROWS, PERM_ROWS)
            qk = jnp.concatenate([q_ref[0, 0, pl.ds(r0, PERM_ROWS), :], k_ref[0, 0, pl.ds(r0, PERM_ROWS), :]], axis=-1)
            v = v_ref[0, 0, pl.ds(r0, PERM_ROWS), :]
            for slot, (dil, p_ref) in enumerate(((4, p4_ref), (16, p16_ref))):
                length, w = seq // dil, PERM_ROWS // dil
                qkp = jnp.dot(p_ref[...], qk, preferred_element_type=F32).astype(BF16)
                vp = jnp.dot(p_ref[...], v, preferred_element_type=F32).astype(BF16)
                for r in range(dil):
                    rows = pl.ds(pl.multiple_of(r * length + blk * w, w), w)
                    qp_ref[slot, rows, :] = qkp[r * w:(r + 1) * w, :ATTN_HEAD_DIM]
                    kp_ref[slot, rows, :] = qkp[r * w:(r + 1) * w, ATTN_HEAD_DIM:]
                    vp_ref[slot, rows, :] = vp[r * w:(r + 1) * w, :]
        return carry

    lax.fori_loop(0, seq // PERM_ROWS // PERM_UNROLL, perm_body, 0)

    def dilated_tile(item, dil, slot):
        length = seq // dil
        n, r = lax.div(item, dil), lax.rem(item, dil)
        qs, ks, bias = window(n, length)
        qrow = pl.multiple_of(r * length + qs, QT)
        krow = pl.multiple_of(r * length + ks, BAND)
        out = _attn_tile(qp_ref[slot, pl.ds(qrow, QT), :], kp_ref[slot, pl.ds(krow, KW), :],
                         vp_ref[slot, pl.ds(krow, KW), :], bias)
        return out, pl.ds(qs * dil + r, QT, stride=dil)

    def body16(it, carry):
        for g in range(CHAINS):
            (acc, m, den), rows = dilated_tile(it * CHAINS + g, 16, 1)
            acc_ref[rows, :] = acc
            m_ref[rows, :] = m
            d_ref[rows, :] = den
        return carry

    lax.fori_loop(0, n_items // CHAINS, body16, 0)

    def body4(it, carry):
        for g in range(CHAINS):
            (acc, m, den), rows = dilated_tile(it * CHAINS + g, 4, 0)
            m0 = m_ref[rows, :]
            mn = jnp.maximum(m0, m)
            a, b = jnp.exp(m0 - mn), jnp.exp(m - mn)
            acc_ref[rows, :] = a * acc_ref[rows, :] + b * acc
            d_ref[rows, :] = a * d_ref[rows, :] + b * den
            m_ref[rows, :] = mn
        return carry

    lax.fori_loop(0, n_items // CHAINS, body4, 0)

    def body1(it, carry):
        for g in range(CHAINS):
            qs, ks, bias = window(it * CHAINS + g, seq)
            qs, ks = pl.multiple_of(qs, QT), pl.multiple_of(ks, BAND)
            acc, m, den = _attn_tile(q_ref[0, 0, pl.ds(qs, QT), :], k_ref[0, 0, pl.ds(ks, KW), :],
                                     v_ref[0, 0, pl.ds(ks, KW), :], bias)
            rows = pl.ds(qs, QT)
            m0 = m_ref[rows, :]
            mn = jnp.maximum(m0, m)
            a, b = jnp.exp(m0 - mn), jnp.exp(m - mn)
            num = a * acc_ref[rows, :] + b * acc
            o_ref[0, 0, rows, :] = (num / (a * d_ref[rows, :] + b * den)).astype(BF16)
        return carry

    lax.fori_loop(0, n_items // CHAINS, body1, 0)


def _perm_matrix(dil):
    w = PERM_ROWS // dil
    i = jnp.arange(PERM_ROWS)
    src = (i % w) * dil + i // w
    return (src[:, None] == jnp.arange(PERM_ROWS)[None, :]).astype(BF16)


def _attn(zqk, zv_, batch, seq, n_ah):
    n_sub = n_ah // 2
    zqk = zqk.reshape(zqk.shape[0], batch, seq, 2 * LANES)
    zv_ = zv_.reshape(zv_.shape[0], batch, seq, 2 * LANES)
    blk = (1, 1, seq, ATTN_HEAD_DIM)
    i = jnp.arange(QT)[:, None]
    jj = jnp.arange(KW)[None, :]
    bias = jnp.stack([jnp.where(jnp.abs(i + o * BAND - jj) <= BAND, 0.0, MASK_VALUE) for o in range(3)]).astype(F32)
    const = lambda shape: pl.BlockSpec(shape, lambda b, h: (0,) * len(shape))
    o = pl.pallas_call(
        functools.partial(_attn_kernel, seq=seq),
        grid=(batch, n_ah),
        in_specs=[pl.BlockSpec(blk, lambda b, h: (h // 2, b, 0, h % 2)),
                  pl.BlockSpec(blk, lambda b, h: (n_sub + h // 2, b, 0, h % 2)),
                  pl.BlockSpec(blk, lambda b, h: (h // 2, b, 0, h % 2)),
                  const((3, QT, KW)), const((PERM_ROWS, PERM_ROWS)), const((PERM_ROWS, PERM_ROWS))],
        out_specs=pl.BlockSpec(blk, lambda b, h: (h, b, 0, 0)),
        out_shape=jax.ShapeDtypeStruct((n_ah, batch, seq, ATTN_HEAD_DIM), BF16),
        scratch_shapes=[pltpu.VMEM((seq, LANES), F32)] * 3
        + [pltpu.VMEM((2, seq, ATTN_HEAD_DIM), BF16)] * 3,
        compiler_params=_cparams(("arbitrary", "arbitrary")),
        name="attn",
    )(zqk, zqk, zv_, bias, _perm_matrix(4), _perm_matrix(16))
    return o.reshape(n_ah, batch * seq, ATTN_HEAD_DIM)


def _ret_kernel(lg_ref, q_ref, k_ref, v_ref, g_ref, o_ref, y_ref, sf_ref, sb_ref, dmat_ref, *, n_tiles, tile, heads):
    hp = pl.program_id(1)
    t = pl.program_id(2)
    c = RET_CHUNK
    lgf = [lg_ref[0, hp * heads + hd] for hd in range(heads)]
    lgb = [lg_ref[1, hp * heads + hd] for hd in range(heads)]
    col = lax.broadcasted_iota(I32, (c, 1), 0).astype(F32)
    row = lax.broadcasted_iota(I32, (1, c), 1).astype(F32)
    n_iter = tile // c // RET_UNROLL

    @pl.when(t == 0)
    def _():
        sf_ref[...] = jnp.zeros_like(sf_ref)
        sb_ref[...] = jnp.zeros_like(sb_ref)
        diff = (lax.broadcasted_iota(I32, (c, c), 0) - lax.broadcasted_iota(I32, (c, c), 1)).astype(F32)
        for hd in range(heads):
            dmat_ref[hd] = jnp.where(diff >= 0, jnp.exp(lgf[hd] * jnp.maximum(diff, 0.0)), 0.0) \
                + jnp.where(diff < 0, jnp.exp(lgb[hd] * jnp.maximum(-diff, 0.0)), 0.0)

    def chunk(hd, r0):
        rows = pl.ds(pl.multiple_of(r0, c), c)
        return rows, q_ref[hd, rows, :], k_ref[hd, rows, :], v_ref[hd, rows, :]

    @pl.when(t < n_tiles)
    def _():
        xi = [jnp.exp(lg * (col + 1.0)) for lg in lgf]
        zeta = [jnp.exp(lg * (c - 1.0 - row)) for lg in lgf]
        cdec = [jnp.exp(lg * jnp.full((1, 1), float(c), F32)) for lg in lgf]
        base = t * tile

        def body(it, carry):
            state = [sf_ref[hd] for hd in range(heads)]
            for u in range(RET_UNROLL):
                r0 = (it * RET_UNROLL + u) * c
                for hd in range(heads):
                    _, q, k, v = chunk(hd, r0)
                    s = lax.dot_general(q, k, (((1,), (1,)), ((), ())), preferred_element_type=F32) * dmat_ref[hd]
                    y = jnp.dot(s.astype(BF16), v, preferred_element_type=F32)
                    y = y + jnp.dot(q, state[hd].astype(BF16), preferred_element_type=F32) * xi[hd]
                    kt = (k.astype(F32).T * zeta[hd]).astype(BF16)
                    state[hd] = state[hd] * cdec[hd] + jnp.dot(kt, v, preferred_element_type=F32)
                    y_ref[hd, pl.ds(pl.multiple_of(base + r0, c), c), :] = y
            for hd in range(heads):
                sf_ref[hd] = state[hd]
            return carry

        lax.fori_loop(0, n_iter, body, 0)

    @pl.when(t >= n_tiles)
    def _():
        xi = [jnp.exp(lg * (c - col)) for lg in lgb]
        zeta = [jnp.exp(lg * row) for lg in lgb]
        cdec = [jnp.exp(lg * jnp.full((1, 1), float(c), F32)) for lg in lgb]
        base = (2 * n_tiles - 1 - t) * tile

        def body(it, carry):
            state = [sb_ref[hd] for hd in range(heads)]
            for u in range(RET_UNROLL):
                r0 = (tile // c - 1 - (it * RET_UNROLL + u)) * c
                for hd in range(heads):
                    rows, q, k, v = chunk(hd, r0)
                    y = y_ref[hd, pl.ds(pl.multiple_of(base + r0, c), c), :]
                    y = y + jnp.dot(q, state[hd].astype(BF16), preferred_element_type=F32) * xi[hd]
                    kt = (k.astype(F32).T * zeta[hd]).astype(BF16)
                    state[hd] = state[hd] * cdec[hd] + jnp.dot(kt, v, preferred_element_type=F32)
                    mu = jnp.mean(y, axis=-1, keepdims=True)
                    yc = y - mu
                    var = jnp.mean(yc * yc, axis=-1, keepdims=True)
                    yn = yc * lax.rsqrt(var + LN_EPS)
                    gate = jax.nn.silu(g_ref[hd, rows, :].astype(F32))
                    o_ref[hd, rows, :] = (gate * yn).astype(BF16)
            for hd in range(heads):
                sb_ref[hd] = state[hd]
            return carry

        lax.fori_loop(0, n_iter, body, 0)


def _ret(zqk, zvg, lg, batch, seq, n_rh, tile):
    n_tiles = seq // tile
    per_b = seq // tile
    heads = 2 if n_rh % 2 == 0 else 1
    groups = n_rh // heads

    def rows(t):
        return jnp.where(t < n_tiles, t, 2 * n_tiles - 1 - t)

    def spec(seg):
        return pl.BlockSpec((heads, tile, RET_HEAD_DIM), lambda b, h, t: (seg * groups + h, b * per_b + rows(t), 0))

    return pl.pallas_call(
        functools.partial(_ret_kernel, n_tiles=n_tiles, tile=tile, heads=heads),
        grid=(batch, groups, 2 * n_tiles),
        in_specs=[pl.BlockSpec(memory_space=pltpu.SMEM), spec(0), spec(1), spec(1), spec(2)],
        out_specs=pl.BlockSpec((heads, tile, RET_HEAD_DIM),
                               lambda b, h, t: (h, b * per_b + jnp.where(t < n_tiles, n_tiles - 1, 2 * n_tiles - 1 - t), 0)),
        out_shape=jax.ShapeDtypeStruct((n_rh, batch * seq, RET_HEAD_DIM), BF16),
        scratch_shapes=[pltpu.VMEM((heads, seq, RET_HEAD_DIM), F32),
                        pltpu.VMEM((heads, RET_HEAD_DIM, RET_HEAD_DIM), F32),
                        pltpu.VMEM((heads, RET_HEAD_DIM, RET_HEAD_DIM), F32),
                        pltpu.VMEM((heads, RET_CHUNK, RET_CHUNK), F32)],
        compiler_params=_cparams(("arbitrary", "arbitrary", "arbitrary")),
        name="ret",
    )(lg, zqk, zqk, zvg, zvg)


def _outproj_kernel(a_ref, r_ref, x_ref, mod_ref, g0_ref, b0_ref, g1_ref, b1_ref, w_ref, x1_ref, hp_ref,
                    *, n_ah, n_rh, alpha):
    mix = jnp.concatenate([a_ref[h] for h in range(n_ah)] + [r_ref[h] for h in range(n_rh)], axis=-1)
    acc = jnp.dot(mix, w_ref[...], preferred_element_type=F32)
    xn = _ln(x_ref[...], g0_ref[...], b0_ref[...])
    y = alpha * xn + (1.0 + mod_ref[0, 2:3, :]) * acc
    x1 = _ln(y, g1_ref[...], b1_ref[...])
    x1_ref[...] = x1
    _store_token_tiles(hp_ref, _pack_bf16_pairs(x1 * (1.0 + mod_ref[0, 4:5, :]) + mod_ref[0, 3:4, :]))


def _outproj(attn, r, x2, mod3, g0, b0, g1, b1, w_out_b, seq, tm, alpha):
    t, d = x2.shape
    n_ah, n_rh = attn.shape[0], r.shape[0]
    per_b = seq // tm
    per = d // 2 // LANES
    row = lambda i: (i, 0)
    vec = pl.BlockSpec((1, d), lambda i: (0, 0))
    return pl.pallas_call(
        functools.partial(_outproj_kernel, n_ah=n_ah, n_rh=n_rh, alpha=alpha),
        grid=(t // tm,),
        in_specs=[pl.BlockSpec((n_ah, tm, ATTN_HEAD_DIM), lambda i: (0, i, 0)),
                  pl.BlockSpec((n_rh, tm, RET_HEAD_DIM), lambda i: (0, i, 0)),
                  pl.BlockSpec((tm, d), row),
                  pl.BlockSpec((1, 6, d), lambda i: (i // per_b, 0, 0)),
                  vec, vec, vec, vec,
                  pl.BlockSpec((d, d), lambda i: (0, 0))],
        out_specs=[pl.BlockSpec((tm, d), row), pl.BlockSpec((tm * per, LANES), row)],
        out_shape=[jax.ShapeDtypeStruct((t, d), F32), jax.ShapeDtypeStruct((t * per, LANES), U32)],
        compiler_params=_cparams(("arbitrary",)),
        name="outproj",
    )(attn, r, x2, mod3, g0, b0, g1, b1, w_out_b)


def _first_argmax(rows):
    best, idx = rows[0], jnp.zeros(rows[0].shape, I32)
    for e in range(1, len(rows)):
        better = rows[e] > best
        idx = jnp.where(better, e, idx)
        best = jnp.maximum(best, rows[e])
    return best, idx


def _router_kernel(h_ref, w_ref, b_ref, u_ref, eid_ref, wt_ref, rank_ref, cnt_ref, carry_ref, *, per):
    i = pl.program_id(0)

    @pl.when(i == 0)
    def _():
        carry_ref[...] = jnp.zeros_like(carry_ref)

    tm = eid_ref.shape[2]
    lt = lax.dot_general(w_ref[...], _load_token_tiles(h_ref, tm, per), (((1,), (1,)), ((), ())),
                         preferred_element_type=F32) + b_ref[...]
    grow = [lt[g:g + 1, :] for g in range(N_GROUPS)]
    gmax, gsel = _first_argmax(grow)
    gsum = grow[0] * 0.0
    for g in range(N_GROUPS):
        gsum = gsum + jnp.exp(grow[g] - gmax)
    pg = 1.0 / gsum
    srow = []
    for e in range(EXPERTS_PER_GROUP):
        r = lt[N_GROUPS + e:N_GROUPS + e + 1, :]
        for g in range(1, N_GROUPS):
            o = N_GROUPS + g * EXPERTS_PER_GROUP + e
            r = jnp.where(gsel == g, lt[o:o + 1, :], r)
        srow.append(r)
    v1, i1 = _first_argmax(srow)
    v2, i2 = _first_argmax([jnp.where(i1 == e, -jnp.inf, srow[e]) for e in range(EXPERTS_PER_GROUP)])
    e2 = jnp.exp(v2 - v1)
    den = 1.0 + e2
    wt_ref[0:1, :] = (1.0 / den) * pg
    wt_ref[1:2, :] = (e2 / den) * pg
    eid0 = gsel * EXPERTS_PER_GROUP + i1
    eid1 = gsel * EXPERTS_PER_GROUP + i2
    eid_ref[0, 0:1, :] = eid0
    eid_ref[0, 1:2, :] = eid1
    erow = lax.broadcasted_iota(I32, (N_EXPERTS, tm), 0)
    oh0 = (erow == eid0).astype(F32)
    oh1 = (erow == eid1).astype(F32)
    oh = oh0 + oh1
    incl = jnp.dot(oh.astype(BF16), u_ref[...], preferred_element_type=F32)
    before = carry_ref[:, 0:1] + incl - oh
    rank_ref[0, 0:1, :] = jnp.sum(oh0 * before, axis=0, keepdims=True).astype(I32)
    rank_ref[0, 1:2, :] = jnp.sum(oh1 * before, axis=0, keepdims=True).astype(I32)
    carry = carry_ref[...] + jnp.sum(oh, axis=1, keepdims=True)
    carry_ref[...] = carry
    cnt_ref[...] = carry


def _router(hp, wr, br, tm):
    d = wr.shape[1]
    per = d // 2 // LANES
    t = hp.shape[0] // per
    tri = (lax.broadcasted_iota(I32, (tm, tm), 0) <= lax.broadcasted_iota(I32, (tm, tm), 1)).astype(BF16)
    tile3 = pl.BlockSpec((1, 2, tm), lambda i: (i, 0, 0))
    return pl.pallas_call(
        functools.partial(_router_kernel, per=per),
        grid=(t // tm,),
        in_specs=[pl.BlockSpec((tm * per, LANES), lambda i: (i, 0)),
                  pl.BlockSpec((ROUTER_ROWS, d), lambda i: (0, 0)),
                  pl.BlockSpec((ROUTER_ROWS, 1), lambda i: (0, 0)),
                  pl.BlockSpec((tm, tm), lambda i: (0, 0))],
        out_specs=[tile3, pl.BlockSpec((2, tm), lambda i: (0, i)), tile3,
                   pl.BlockSpec((N_EXPERTS, LANES), lambda i: (0, 0))],
        out_shape=[jax.ShapeDtypeStruct((t // tm, 2, tm), I32), jax.ShapeDtypeStruct((2, t), F32),
                   jax.ShapeDtypeStruct((t // tm, 2, tm), I32), jax.ShapeDtypeStruct((N_EXPERTS, LANES), F32)],
        scratch_shapes=[pltpu.VMEM((N_EXPERTS, LANES), F32)],
        compiler_params=_cparams(("arbitrary",)),
        name="router",
    )(hp, wr, br, tri)


def _tile_copy(src, s_row, dst, d_row, per, sem):
    return pltpu.make_async_copy(src.at[pl.ds(pl.multiple_of(s_row, per), per), :],
                                 dst.at[pl.ds(pl.multiple_of(d_row, per), per), :], sem)


def _dispatch_kernel(lo_ref, hi_ref, dst_ref, h_ref, xb_hbm, stage, zero, sem, *, per):
    i = pl.program_id(0)
    n = pl.num_programs(0)
    slot = i % 2
    tm = dst_ref.shape[2] // 2

    def wait_tile(s):
        pltpu.make_async_copy(xb_hbm.at[pl.ds(0, 2 * tm * per), :], xb_hbm.at[pl.ds(0, 2 * tm * per), :], sem.at[s]).wait()

    @pl.when(i >= 2)
    def _():
        wait_tile(slot)

    stage[slot] = h_ref[...]

    def body(r, carry):
        for k in range(2):
            _tile_copy(stage.at[slot], r * per, xb_hbm, dst_ref[0, 0, k * tm + r], per, sem.at[slot]).start(priority=k)
        return carry

    lax.fori_loop(0, tm, body, 0, unroll=8)

    @pl.when(i == n - 1)
    def _():
        @pl.when(n >= 2)
        def _():
            wait_tile(1 - slot)
        wait_tile(slot)
        zero[...] = jnp.zeros_like(zero)

        def fill(e, carry):
            def one(s, c):
                _tile_copy(zero, 0, xb_hbm, s * per, per, sem.at[2]).start()
                return c
            lax.fori_loop(lo_ref[e], hi_ref[e], one, 0)

            def one_wait(s, c):
                _tile_copy(zero, 0, xb_hbm, s * per, per, sem.at[2]).wait()
                return c
            lax.fori_loop(lo_ref[e], hi_ref[e], one_wait, 0)
            return carry

        lax.fori_loop(0, N_EXPERTS + 1, fill, 0)


def _dispatch(fill_lo, fill_hi, drow3, hp, n_slots, per):
    n, tm = drow3.shape[0], drow3.shape[2] // 2
    smem3 = pl.BlockSpec((1, 1, 2 * tm), lambda i, *_: (i, 0, 0), memory_space=pltpu.SMEM)
    grid_spec = pltpu.PrefetchScalarGridSpec(
        num_scalar_prefetch=2,
        grid=(n,),
        in_specs=[smem3, pl.BlockSpec((tm * per, LANES), lambda i, *_: (i, 0))],
        out_specs=pl.BlockSpec(memory_space=pl.ANY),
        scratch_shapes=[pltpu.VMEM((2, tm * per, LANES), U32), pltpu.VMEM((per, LANES), U32),
                        pltpu.SemaphoreType.DMA((3,))])
    return pl.pallas_call(
        functools.partial(_dispatch_kernel, per=per),
        grid_spec=grid_spec,
        out_shape=jax.ShapeDtypeStruct((n_slots * per, LANES), U32),
        compiler_params=_cparams(("arbitrary",)),
        name="dispatch",
    )(fill_lo, fill_hi, drow3, hp)


def _expert_kernel(be_ref, first_ref, par_ref, nxt_ref, has_ref, x_ref, w1_hbm, w3_hbm, w2_hbm, y_ref,
                   wf1, wf3, wf2, w1b, w3b, w2b, sem, *, per):
    i = pl.program_id(0)

    def weight_copies(e, s):
        return [pltpu.make_async_copy(src.at[e], dst.at[s], sem.at[s])
                for src, dst in ((w1_hbm, wf1), (w3_hbm, wf3), (w2_hbm, wf2))]

    @pl.when(i == 0)
    def _():
        for cp in weight_copies(be_ref[0], 0):
            cp.start()

    @pl.when(first_ref[i] == 1)
    def _():
        s = par_ref[i]
        for cp in weight_copies(be_ref[i], s):
            cp.wait()

        @pl.when(has_ref[i] == 1)
        def _():
            for cp in weight_copies(nxt_ref[i], 1 - s):
                cp.start()

        w1b[...] = wf1[s].astype(BF16)
        w3b[...] = wf3[s].astype(BF16)
        w2b[...] = wf2[s].astype(BF16)

    x = _load_token_tiles(x_ref, MOE_BLOCK, per)
    a = jnp.dot(x, w1b[...], preferred_element_type=F32)
    b = jnp.dot(x, w3b[...], preferred_element_type=F32)
    mid = (jax.nn.silu(a) * b).astype(BF16)
    _store_token_tiles(y_ref, _pack_bf16_pairs(jnp.dot(mid, w2b[...], preferred_element_type=F32)))


def _experts(blk_e, xb, w1, w3, w2):
    d, ff = w1.shape[1], w1.shape[2]
    per = d // 2 // LANES
    nblk = xb.shape[0] // per // MOE_BLOCK
    rows = pl.BlockSpec((MOE_BLOCK * per, LANES), lambda i, *_: (i, 0))
    first = jnp.concatenate([jnp.ones((1,), I32), (blk_e[1:] != blk_e[:-1]).astype(I32)])
    parity = (jnp.cumsum(first) - 1) % 2
    nxt = jnp.min(jnp.where(blk_e[None, :] > blk_e[:, None], blk_e[None, :], N_EXPERTS), axis=1)
    has_next = (nxt < N_EXPERTS).astype(I32)
    nxt = jnp.minimum(nxt, N_EXPERTS - 1)
    grid_spec = pltpu.PrefetchScalarGridSpec(
        num_scalar_prefetch=5,
        grid=(nblk,),
        in_specs=[rows] + [pl.BlockSpec(memory_space=pl.ANY)] * 3,
        out_specs=rows,
        scratch_shapes=[pltpu.VMEM((2, d, ff), F32), pltpu.VMEM((2, d, ff), F32), pltpu.VMEM((2, ff, d), F32),
                        pltpu.VMEM((d, ff), BF16), pltpu.VMEM((d, ff), BF16), pltpu.VMEM((ff, d), BF16),
                        pltpu.SemaphoreType.DMA((2,))])
    return pl.pallas_call(
        functools.partial(_expert_kernel, per=per),
        grid_spec=grid_spec,
        out_shape=jax.ShapeDtypeStruct(xb.shape, U32),
        compiler_params=_cparams(("arbitrary",)),
        name="experts",
    )(blk_e, first, parity.astype(I32), nxt.astype(I32), has_next, xb, w1, w3, w2)


def _combine_kernel(src_ref, srcn_ref, y_hbm, wt_ref, x1_ref, mod_ref, g_ref, b_ref, o_ref, ybuf, sem, *, alpha, per):
    i = pl.program_id(0)
    n = pl.num_programs(0)
    slot = i % 2
    tm = x1_ref.shape[0]

    def wait(s):
        for k in range(2):
            pltpu.make_async_copy(y_hbm.at[pl.ds(0, tm * per), :], ybuf.at[s, k], sem.at[s]).wait()

    @pl.when(i == 0)
    def _():
        def body(r, carry):
            for k in range(2):
                _tile_copy(y_hbm, src_ref[0, 0, k * tm + r], ybuf.at[0, k], r * per, per, sem.at[0]).start(priority=k)
            return carry
        lax.fori_loop(0, tm, body, 0, unroll=8)

    for r in range(tm):
        for k in range(2):
            _tile_copy(y_hbm, srcn_ref[0, 0, k * tm + r], ybuf.at[1 - slot, k], r * per, per, sem.at[1 - slot]).start(priority=k)

    wait(slot)
    wt = wt_ref[...]
    y0 = _load_token_tiles(ybuf.at[slot, 0], tm, per, F32)
    y1 = _load_token_tiles(ybuf.at[slot, 1], tm, per, F32)
    ffn = wt[:, 0:1] * y0 + wt[:, 1:2] * y1
    y = alpha * x1_ref[...] + (1.0 + mod_ref[0, 5:6, :]) * ffn
    o_ref[...] = _ln(y, g_ref[...], b_ref[...])

    @pl.when(i == n - 1)
    def _():
        wait(1 - slot)


def _combine(drow3, yb, wt_t, x1, mod3, g2, b2, seq, alpha):
    t, d = x1.shape
    per = d // 2 // LANES
    n, tm = drow3.shape[0], drow3.shape[2] // 2
    per_b = seq // tm
    cur = pl.BlockSpec((1, 1, 2 * tm), lambda i: (i, 0, 0), memory_space=pltpu.SMEM)
    nxt = pl.BlockSpec((1, 1, 2 * tm), lambda i: (jnp.minimum(i + 1, n - 1), 0, 0), memory_space=pltpu.SMEM)
    vec = pl.BlockSpec((1, d), lambda i: (0, 0))
    return pl.pallas_call(
        functools.partial(_combine_kernel, alpha=alpha, per=per),
        grid=(n,),
        in_specs=[cur, nxt,
                  pl.BlockSpec(memory_space=pl.ANY),
                  pl.BlockSpec((tm, 2), lambda i: (i, 0)),
                  pl.BlockSpec((tm, d), lambda i: (i, 0)),
                  pl.BlockSpec((1, 6, d), lambda i: (i // per_b, 0, 0)),
                  vec, vec],
        out_specs=pl.BlockSpec((tm, d), lambda i: (i, 0)),
        out_shape=jax.ShapeDtypeStruct((t, d), F32),
        scratch_shapes=[pltpu.VMEM((2, 2, tm * per, LANES), U32), pltpu.SemaphoreType.DMA((2,))],
        compiler_params=_cparams(("arbitrary",)),
        name="combine",
    )(drow3, drow3, yb, wt_t, x1, mod3, g2, b2)


def _pick_tile(n, want):
    tm = min(n, want)
    assert n % tm == 0
    return tm


def kernel(x, c, positions, ln0_g, ln0_b, w_ada, b_ada, w_in, w_out, ret_log_decay_f, ret_log_decay_b,
           ln1_g, ln1_b, w_group, b_group, w_sub, b_sub, w1, w3, w2, ln2_g, ln2_b):
    batch, seq, d = x.shape
    depth = w_ada.shape[0]
    t = batch * seq
    n_ah = d // 2 // ATTN_HEAD_DIM
    n_rh = d // 2 // RET_HEAD_DIM
    assert depth == 1 and d % (2 * RET_HEAD_DIM) == 0 and batch <= 8
    assert seq % (QT * CHAINS) == 0 and seq % (max(DILATIONS) * PERM_ROWS) == 0 and seq // max(DILATIONS) >= KW
    alpha = (2 * depth) ** 0.25

    inv_rope = ROPE_THETA ** (-jnp.arange(0, ROPE_DIM, 2, dtype=F32) / ROPE_DIM)
    inv_ret = RET_THETA ** (-jnp.linspace(0.0, 1.0, RET_HEAD_DIM // 2, dtype=F32))
    fac = jnp.tile(inv_rope, TOK_PER_ROW).reshape(1, LANES)
    fr = inv_ret.reshape(1, LANES)
    posb = jnp.broadcast_to(positions.astype(F32).reshape(t, 1), (t, LANES))
    posc = jnp.repeat(positions.astype(F32).reshape(t // TOK_PER_ROW, TOK_PER_ROW), ROPE_DIM // 2, axis=1)
    c8 = jnp.zeros((8, d), F32).at[:batch].set(c)
    row = lambda v: v.reshape(1, d)

    xs = x.reshape(t, d)
    mod = _ada(c8, w_ada[0], b_ada[0].reshape(1, -1))
    mod3 = mod[:batch].reshape(batch, 6, d)
    h, tabs = _prep(xs, posb, posc, mod3, row(ln0_g), row(ln0_b), fac, fr, seq, _pick_tile(seq, 512))
    w_in_b = w_in[0].astype(BF16)
    tmp = _pick_tile(seq, 2048)
    z_aqk = _proj(_proj_attn_kernel, lambda j: j, 2, h, w_in_b, tabs, tmp, "proj_attn")
    z_rqk = _proj(_proj_ret_kernel, lambda j: 3 + j, 2, h, w_in_b, tabs, _pick_tile(seq, 1024), "proj_ret")
    z_pl = _proj(_proj_plain_kernel, lambda j: 2 + 3 * jnp.minimum(j, 1) + jnp.maximum(j - 1, 0), 3,
                 h, w_in_b, None, tmp, "proj_plain")
    attn = _attn(z_aqk, z_pl, batch, seq, n_ah)
    lg = jnp.stack([ret_log_decay_f[0], ret_log_decay_b[0]]).astype(F32)
    r = _ret(z_rqk, z_pl, lg, batch, seq, n_rh, _pick_tile(seq, 2048))
    x1, hp = _outproj(attn, r, xs, mod3, row(ln0_g), row(ln0_b), row(ln1_g[0]), row(ln1_b[0]),
                      w_out[0].astype(BF16), seq, _pick_tile(seq, 512), alpha)
    wr = jnp.zeros((ROUTER_ROWS, d), F32)
    wr = wr.at[:N_GROUPS].set(w_group[0].T)
    wr = wr.at[N_GROUPS:N_GROUPS + N_EXPERTS].set(w_sub[0].transpose(0, 2, 1).reshape(N_EXPERTS, d))
    br = jnp.zeros((ROUTER_ROWS, 1), F32)
    br = br.at[:N_GROUPS, 0].set(b_group[0]).at[N_GROUPS:N_GROUPS + N_EXPERTS, 0].set(b_sub[0].reshape(-1))
    eid3, wt, rank3, cnt = _router(hp, wr.astype(BF16), br, _pick_tile(seq, 512))
    counts = cnt[:, 0].astype(I32)
    padded = (counts + MOE_BLOCK - 1) // MOE_BLOCK * MOE_BLOCK
    pend = jnp.cumsum(padded)
    pstart = pend - padded
    n_slots = 2 * t + N_EXPERTS * MOE_BLOCK
    nblk = n_slots // MOE_BLOCK
    starts = jnp.arange(nblk, dtype=I32) * MOE_BLOCK
    blk_e = jnp.minimum(jnp.sum((pend[None, :] <= starts[:, None]).astype(I32), axis=1), N_EXPERTS - 1)
    fill_lo = jnp.concatenate([pstart + counts, pend[-1:]])
    fill_hi = jnp.concatenate([pend, jnp.full((1,), n_slots, I32)])
    per = d // 2 // LANES
    onehot = eid3[..., None] == jnp.arange(N_EXPERTS, dtype=I32)
    drow3 = ((rank3 + jnp.sum(jnp.where(onehot, pstart, 0), axis=-1)) * per).reshape(eid3.shape[0], 1, -1)
    xb = _dispatch(fill_lo, fill_hi, drow3, hp, n_slots, per)
    yb = _experts(blk_e, xb, w1[0], w3[0], w2[0])
    out = _combine(drow3, yb, wt.T, x1, mod3, row(ln2_g[0]), row(ln2_b[0]), seq, alpha)
    return out.reshape(batch, seq, d)
```

```python
import functools

import jax
import jax.numpy as jnp
from jax import lax
from jax.experimental import pallas as pl
from jax.experimental.pallas import tpu as pltpu

F32 = jnp.float32
BF16 = jnp.bfloat16
I32 = jnp.int32
U32 = jnp.uint32

LANES = 128
ATTN_HEAD_DIM = 128
RET_HEAD_DIM = 256
DILATIONS = (1, 4, 16)
BAND = 64
QT = 128
KW = QT + 2 * BAND
CHAINS = 64
PERM_ROWS = 256
PERM_UNROLL = 8
ROPE_THETA = 500000.0
ROPE_DIM = ATTN_HEAD_DIM // 4
TOK_PER_ROW = LANES // (ROPE_DIM // 2)
RET_THETA = 10000.0
RET_CHUNK = 128
RET_UNROLL = 8
N_GROUPS = 4
EXPERTS_PER_GROUP = 8
N_EXPERTS = N_GROUPS * EXPERTS_PER_GROUP
MOE_BLOCK = 256
LN_EPS = 1e-5
MASK_VALUE = -1e30
ROUTER_ROWS = 48
HI_MASK = 0xFFFF0000
VMEM_LIMIT = 56 * 1024 * 1024


def _cparams(sem, vmem=VMEM_LIMIT):
    return pltpu.CompilerParams(dimension_semantics=sem, vmem_limit_bytes=vmem)


def _ln(x, g, b):
    mu = jnp.mean(x, axis=-1, keepdims=True)
    xc = x - mu
    var = jnp.mean(xc * xc, axis=-1, keepdims=True)
    return xc * lax.rsqrt(var + LN_EPS) * g + b


def _pack_bf16_pairs(h):
    bits = lax.bitcast_convert_type(h.astype(BF16).astype(F32), U32)
    n = h.shape[1] // 2
    return (bits[:, :n] >> 16) | (bits[:, n:] & jnp.uint32(HI_MASK))


def _store_token_tiles(ref, packed):
    m, n = packed.shape
    per = n // LANES
    for s in range(per):
        ref[pl.ds(s, m, stride=per), :] = packed[:, s * LANES:(s + 1) * LANES]


def _load_token_tiles(ref, m, per, dtype=BF16):
    slabs = [ref[pl.ds(s, m, stride=per), :] for s in range(per)]
    lo = [lax.bitcast_convert_type(p << 16, F32) for p in slabs]
    hi = [lax.bitcast_convert_type(p & jnp.uint32(HI_MASK), F32) for p in slabs]
    return jnp.concatenate(lo + hi, axis=-1).astype(dtype)


def _ada_kernel(c_ref, w_ref, b_ref, o_ref):
    cs = jax.nn.silu(c_ref[...])
    o_ref[...] = jnp.dot(cs.astype(BF16), w_ref[...].astype(BF16), preferred_element_type=F32) + b_ref[...]


def _ada(c8, w_ada, b_ada):
    d, n = w_ada.shape
    tn = min(n, 512)
    return pl.pallas_call(
        _ada_kernel,
        grid=(n // tn,),
        in_specs=[pl.BlockSpec((8, d), lambda j: (0, 0)),
                  pl.BlockSpec((d, tn), lambda j: (0, j)),
                  pl.BlockSpec((1, tn), lambda j: (0, j))],
        out_specs=pl.BlockSpec((8, tn), lambda j: (0, j)),
        out_shape=jax.ShapeDtypeStruct((8, n), F32),
        compiler_params=_cparams(("arbitrary",)),
        name="ada",
    )(c8, w_ada, b_ada)


def _prep_kernel(x_ref, pos_ref, posc_ref, mod_ref, g_ref, b_ref, fac_ref, fr_ref, h_ref, tab_ref):
    xn = _ln(x_ref[...], g_ref[...], b_ref[...])
    h_ref[...] = (xn * (1.0 + mod_ref[0, 1:2, :]) + mod_ref[0, 0:1, :]).astype(BF16)
    half = ROPE_DIM // 2
    angc = posc_ref[...] * fac_ref[...]
    cc, sc = jnp.cos(angc), jnp.sin(angc)
    rows_c = angc.shape[0]
    lane = lax.broadcasted_iota(I32, angc.shape, 1)
    first, second = lane < half, (lane >= half) & (lane < 2 * half)
    for ts in range(TOK_PER_ROW):
        shift = (LANES - ts * half) % LANES
        c = pltpu.roll(cc, shift, 1) if shift else cc
        s = pltpu.roll(sc, shift, 1) if shift else sc
        rows = pl.ds(ts, rows_c, stride=TOK_PER_ROW)
        tab_ref.at[0][rows, :] = jnp.where(first, c, jnp.where(second, pltpu.roll(c, half, 1), 1.0))
        tab_ref.at[1][rows, :] = jnp.where(first, -s, 0.0)
        tab_ref.at[2][rows, :] = jnp.where(second, pltpu.roll(s, half, 1), 0.0)
    angr = pos_ref[...] * fr_ref[...]
    tab_ref[3] = jnp.cos(angr)
    tab_ref[4] = jnp.sin(angr)


def _prep(x2, posb, posc, mod3, ln0_g, ln0_b, fac, fr, seq, tm):
    t, d = x2.shape
    per_b = seq // tm
    vec = lambda n: pl.BlockSpec((1, n), lambda i: (0, 0))
    return pl.pallas_call(
        _prep_kernel,
        grid=(t // tm,),
        in_specs=[pl.BlockSpec((tm, d), lambda i: (i, 0)),
                  pl.BlockSpec((tm, LANES), lambda i: (i, 0)),
                  pl.BlockSpec((tm // TOK_PER_ROW, LANES), lambda i: (i, 0)),
                  pl.BlockSpec((1, 6, d), lambda i: (i // per_b, 0, 0)),
                  vec(d), vec(d), vec(LANES), vec(LANES)],
        out_specs=[pl.BlockSpec((tm, d), lambda i: (i, 0)), pl.BlockSpec((5, tm, LANES), lambda i: (0, i, 0))],
        out_shape=[jax.ShapeDtypeStruct((t, d), BF16), jax.ShapeDtypeStruct((5, t, LANES), F32)],
        compiler_params=_cparams(("arbitrary",)),
        name="prep",
    )(x2, posb, posc, mod3, ln0_g, ln0_b, fac, fr)


def _proj_chunks(h_ref, w_ref, n_sub):
    for hc in range(n_sub):
        yield hc, jnp.dot(h_ref[...], w_ref[:, hc * 2 * LANES:(hc + 1) * 2 * LANES], preferred_element_type=F32)


def _proj_attn_kernel(h_ref, w_ref, tab_ref, z_ref, *, n_sub):
    scale = jnp.where(pl.program_id(0) == 0, ATTN_HEAD_DIM ** -0.5, 1.0).astype(F32)
    ca, s1, s2 = tab_ref[0] * scale, tab_ref[1] * scale, tab_ref[2] * scale
    half = ROPE_DIM // 2
    for hc, res in _proj_chunks(h_ref, w_ref, n_sub):
        for lanes in (slice(0, LANES), slice(LANES, 2 * LANES)):
            x = res[:, lanes]
            r = x * ca + pltpu.roll(x, LANES - half, 1) * s1 + pltpu.roll(x, half, 1) * s2
            z_ref[hc, :, lanes] = r.astype(BF16)


def _proj_ret_kernel(h_ref, w_ref, tab_ref, z_ref, *, n_sub):
    scale = jnp.where(pl.program_id(0) == 1, RET_HEAD_DIM ** -0.5, 1.0).astype(F32)
    cr, sr = tab_ref[3] * scale, tab_ref[4] * scale
    for hc, res in _proj_chunks(h_ref, w_ref, n_sub):
        x1, x2 = res[:, :LANES], res[:, LANES:]
        z_ref[hc, :, :LANES] = (x1 * cr - x2 * sr).astype(BF16)
        z_ref[hc, :, LANES:] = (x2 * cr + x1 * sr).astype(BF16)


def _proj_plain_kernel(h_ref, w_ref, z_ref, *, n_sub):
    for hc, res in _proj_chunks(h_ref, w_ref, n_sub):
        z_ref[hc] = res.astype(BF16)


def _proj(body, col_of, n_seg, h, w_in_b, tabs, tm, name):
    t, d = h.shape
    tn = d // 2
    n_sub = tn // (2 * LANES)
    in_specs = [pl.BlockSpec((tm, d), lambda j, i: (i, 0)),
                pl.BlockSpec((d, tn), lambda j, i: (0, col_of(j)))]
    args = [h, w_in_b]
    if tabs is not None:
        in_specs.append(pl.BlockSpec((5, tm, LANES), lambda j, i: (0, i, 0)))
        args.append(tabs)
    return pl.pallas_call(
        functools.partial(body, n_sub=n_sub),
        grid=(n_seg, t // tm),
        in_specs=in_specs,
        out_specs=pl.BlockSpec((n_sub, tm, 2 * LANES), lambda j, i: (j, i, 0)),
        out_shape=jax.ShapeDtypeStruct((n_seg * n_sub, t, 2 * LANES), BF16),
        compiler_params=_cparams(("arbitrary", "arbitrary")),
        name=name,
    )(*args)


def _attn_tile(q, k, v, bias):
    s = lax.dot_general(q, k, (((1,), (1,)), ((), ())), preferred_element_type=F32) + bias
    if s.shape[1] == 2 * LANES:
        m = jnp.max(jnp.maximum(s[:, :LANES], s[:, LANES:]), axis=-1, keepdims=True)
        p = jnp.exp(s - m)
        den = jnp.sum(p[:, :LANES] + p[:, LANES:], axis=-1, keepdims=True)
    else:
        m = jnp.max(s, axis=-1, keepdims=True)
        p = jnp.exp(s - m)
        den = jnp.sum(p, axis=-1, keepdims=True)
    acc = jnp.dot(p.astype(BF16), v, preferred_element_type=F32)
    return acc, jnp.broadcast_to(m, acc.shape), jnp.broadcast_to(den, acc.shape)


def _attn_kernel(q_ref, k_ref, v_ref, bias_ref, p4_ref, p16_ref, o_ref,
                 acc_ref, m_ref, d_ref, qp_ref, kp_ref, vp_ref, *, seq):
    n_items = seq // QT

    def window(n, length):
        qs = n * QT
        ks = jnp.clip(qs - BAND, 0, length - KW)
        return qs, ks, bias_ref[lax.div(qs - ks, BAND)]

    def perm_body(it, carry):
        for u in range(PERM_UNROLL):
            blk = it * PERM_UNROLL + u
            r0 = pl.multiple_of(blk * PERM_ROWS, PERM_ROWS)
            qk = jnp.concatenate([q_ref[0, 0, pl.ds(r0, PERM_ROWS), :], k_ref[0, 0, pl.ds(r0, PERM_ROWS), :]], axis=-1)
            v = v_ref[0, 0, pl.ds(r0, PERM_ROWS), :]
            for slot, (dil, p_ref) in enumerate(((4, p4_ref), (16, p16_ref))):
                length, w = seq // dil, PERM_ROWS // dil
                qkp = jnp.dot(p_ref[...], qk, preferred_element_type=F32).astype(BF16)
                vp = jnp.dot(p_ref[...], v, preferred_element_type=F32).astype(BF16)
                for r in range(dil):
                    rows = pl.ds(pl.multiple_of(r * length + blk * w, w), w)
                    qp_ref[slot, rows, :] = qkp[r * w:(r + 1) * w, :ATTN_HEAD_DIM]
                    kp_ref[slot, rows, :] = qkp[r * w:(r + 1) * w, ATTN_HEAD_DIM:]
                    vp_ref[slot, rows, :] = vp[r * w:(r + 1) * w, :]
        return carry

    lax.fori_loop(0, seq // PERM_ROWS // PERM_UNROLL, perm_body, 0)

    def dilated_tile(item, dil, slot):
        length = seq // dil
        n, r = lax.div(item, dil), lax.rem(item, dil)
        qs, ks, bias = window(n, length)
        qrow = pl.multiple_of(r * length + qs, QT)
        krow = pl.multiple_of(r * length + ks, BAND)
        out = _attn_tile(qp_ref[slot, pl.ds(qrow, QT), :], kp_ref[slot, pl.ds(krow, KW), :],
                         vp_ref[slot, pl.ds(krow, KW), :], bias)
        return out, pl.ds(qs * dil + r, QT, stride=dil)

    def body16(it, carry):
        for g in range(CHAINS):
            (acc, m, den), rows = dilated_tile(it * CHAINS + g, 16, 1)
            acc_ref[rows, :] = acc
            m_ref[rows, :] = m
            d_ref[rows, :] = den
        return carry

    lax.fori_loop(0, n_items // CHAINS, body16, 0)

    def body4(it, carry):
        for g in range(CHAINS):
            (acc, m, den), rows = dilated_tile(it * CHAINS + g, 4, 0)
            m0 = m_ref[rows, :]
            mn = jnp.maximum(m0, m)
            a, b = jnp.exp(m0 - mn), jnp.exp(m - mn)
            acc_ref[rows, :] = a * acc_ref[rows, :] + b * acc
            d_ref[rows, :] = a * d_ref[rows, :] + b * den
            m_ref[rows, :] = mn
        return carry

    lax.fori_loop(0, n_items // CHAINS, body4, 0)

    def body1(it, carry):
        for g in range(CHAINS):
            qs, ks, bias = window(it * CHAINS + g, seq)
            qs, ks = pl.multiple_of(qs, QT), pl.multiple_of(ks, BAND)
            acc, m, den = _attn_tile(q_ref[0, 0, pl.ds(qs, QT), :], k_ref[0, 0, pl.ds(ks, KW), :],
                                     v_ref[0, 0, pl.ds(ks, KW), :], bias)
            rows = pl.ds(qs, QT)
            m0 = m_ref[rows, :]
            mn = jnp.maximum(m0, m)
            a, b = jnp.exp(m0 - mn), jnp.exp(m - mn)
            num = a * acc_ref[rows, :] + b * acc
            o_ref[0, 0, rows, :] = (num / (a * d_ref[rows, :] + b * den)).astype(BF16)
        return carry

    lax.fori_loop(0, n_items // CHAINS, body1, 0)


def _perm_matrix(dil):
    w = PERM_ROWS // dil
    i = jnp.arange(PERM_ROWS)
    src = (i % w) * dil + i // w
    return (src[:, None] == jnp.arange(PERM_ROWS)[None, :]).astype(BF16)


def _attn(zqk, zv_, batch, seq, n_ah):
    n_sub = n_ah // 2
    zqk = zqk.reshape(zqk.shape[0], batch, seq, 2 * LANES)
    zv_ = zv_.reshape(zv_.shape[0], batch, seq, 2 * LANES)
    blk = (1, 1, seq, ATTN_HEAD_DIM)
    i = jnp.arange(QT)[:, None]
    jj = jnp.arange(KW)[None, :]
    bias = jnp.stack([jnp.where(jnp.abs(i + o * BAND - jj) <= BAND, 0.0, MASK_VALUE) for o in range(3)]).astype(F32)
    const = lambda shape: pl.BlockSpec(shape, lambda b, h: (0,) * len(shape))
    o = pl.pallas_call(
        functools.partial(_attn_kernel, seq=seq),
        grid=(batch, n_ah),
        in_specs=[pl.BlockSpec(blk, lambda b, h: (h // 2, b, 0, h % 2)),
                  pl.BlockSpec(blk, lambda b, h: (n_sub + h // 2, b, 0, h % 2)),
                  pl.BlockSpec(blk, lambda b, h: (h // 2, b, 0, h % 2)),
                  const((3, QT, KW)), const((PERM_ROWS, PERM_ROWS)), const((PERM_ROWS, PERM_ROWS))],
        out_specs=pl.BlockSpec(blk, lambda b, h: (h, b, 0, 0)),
        out_shape=jax.ShapeDtypeStruct((n_ah, batch, seq, ATTN_HEAD_DIM), BF16),
        scratch_shapes=[pltpu.VMEM((seq, LANES), F32)] * 3
        + [pltpu.VMEM((2, seq, ATTN_HEAD_DIM), BF16)] * 3,
        compiler_params=_cparams(("arbitrary", "arbitrary")),
        name="attn",
    )(zqk, zqk, zv_, bias, _perm_matrix(4), _perm_matrix(16))
    return o.reshape(n_ah, batch * seq, ATTN_HEAD_DIM)


def _ret_kernel(lg_ref, q_ref, k_ref, v_ref, g_ref, o_ref, y_ref, sf_ref, sb_ref, dmat_ref, *, n_tiles, tile, heads):
    hp = pl.program_id(1)
    t = pl.program_id(2)
    c = RET_CHUNK
    lgf = [lg_ref[0, hp * heads + hd] for hd in range(heads)]
    lgb = [lg_ref[1, hp * heads + hd] for hd in range(heads)]
    col = lax.broadcasted_iota(I32, (c, 1), 0).astype(F32)
    row = lax.broadcasted_iota(I32, (1, c), 1).astype(F32)
    n_iter = tile // c // RET_UNROLL

    @pl.when(t == 0)
    def _():
        sf_ref[...] = jnp.zeros_like(sf_ref)
        sb_ref[...] = jnp.zeros_like(sb_ref)
        diff = (lax.broadcasted_iota(I32, (c, c), 0) - lax.broadcasted_iota(I32, (c, c), 1)).astype(F32)
        for hd in range(heads):
            dmat_ref[hd] = jnp.where(diff >= 0, jnp.exp(lgf[hd] * jnp.maximum(diff, 0.0)), 0.0) \
                + jnp.where(diff < 0, jnp.exp(lgb[hd] * jnp.maximum(-diff, 0.0)), 0.0)

    def chunk(hd, r0):
        rows = pl.ds(pl.multiple_of(r0, c), c)
        return rows, q_ref[hd, rows, :], k_ref[hd, rows, :], v_ref[hd, rows, :]

    @pl.when(t < n_tiles)
    def _():
        xi = [jnp.exp(lg * (col + 1.0)) for lg in lgf]
        zeta = [jnp.exp(lg * (c - 1.0 - row)) for lg in lgf]
        cdec = [jnp.exp(lg * jnp.full((1, 1), float(c), F32)) for lg in lgf]
        base = t * tile

        def body(it, carry):
            state = [sf_ref[hd] for hd in range(heads)]
            for u in range(RET_UNROLL):
                r0 = (it * RET_UNROLL + u) * c
                for hd in range(heads):
                    _, q, k, v = chunk(hd, r0)
                    s = lax.dot_general(q, k, (((1,), (1,)), ((), ())), preferred_element_type=F32) * dmat_ref[hd]
                    y = jnp.dot(s.astype(BF16), v, preferred_element_type=F32)
                    y = y + jnp.dot(q, state[hd].astype(BF16), preferred_element_type=F32) * xi[hd]
                    kt = (k.astype(F32).T * zeta[hd]).astype(BF16)
                    state[hd] = state[hd] * cdec[hd] + jnp.dot(kt, v, preferred_element_type=F32)
                    y_ref[hd, pl.ds(pl.multiple_of(base + r0, c), c), :] = y
            for hd in range(heads):
                sf_ref[hd] = state[hd]
            return carry

        lax.fori_loop(0, n_iter, body, 0)

    @pl.when(t >= n_tiles)
    def _():
        xi = [jnp.exp(lg * (c - col)) for lg in lgb]
        zeta = [jnp.exp(lg * row) for lg in lgb]
        cdec = [jnp.exp(lg * jnp.full((1, 1), float(c), F32)) for lg in lgb]
        base = (2 * n_tiles - 1 - t) * tile

        def body(it, carry):
            state = [sb_ref[hd] for hd in range(heads)]
            for u in range(RET_UNROLL):
                r0 = (tile // c - 1 - (it * RET_UNROLL + u)) * c
                for hd in range(heads):
                    rows, q, k, v = chunk(hd, r0)
                    y = y_ref[hd, pl.ds(pl.multiple_of(base + r0, c), c), :]
                    y = y + jnp.dot(q, state[hd].astype(BF16), preferred_element_type=F32) * xi[hd]
                    kt = (k.astype(F32).T * zeta[hd]).astype(BF16)
                    state[hd] = state[hd] * cdec[hd] + jnp.dot(kt, v, preferred_element_type=F32)
                    mu = jnp.mean(y, axis=-1, keepdims=True)
                    yc = y - mu
                    var = jnp.mean(yc * yc, axis=-1, keepdims=True)
                    yn = yc * lax.rsqrt(var + LN_EPS)
                    gate = jax.nn.silu(g_ref[hd, rows, :].astype(F32))
                    o_ref[hd, rows, :] = (gate * yn).astype(BF16)
            for hd in range(heads):
                sb_ref[hd] = state[hd]
            return carry

        lax.fori_loop(0, n_iter, body, 0)


def _ret(zqk, zvg, lg, batch, seq, n_rh, tile):
    n_tiles = seq // tile
    per_b = seq // tile
    heads = 2 if n_rh % 2 == 0 else 1
    groups = n_rh // heads

    def rows(t):
        return jnp.where(t < n_tiles, t, 2 * n_tiles - 1 - t)

    def spec(seg):
        return pl.BlockSpec((heads, tile, RET_HEAD_DIM), lambda b, h, t: (seg * groups + h, b * per_b + rows(t), 0))

    return pl.pallas_call(
        functools.partial(_ret_kernel, n_tiles=n_tiles, tile=tile, heads=heads),
        grid=(batch, groups, 2 * n_tiles),
        in_specs=[pl.BlockSpec(memory_space=pltpu.SMEM), spec(0), spec(1), spec(1), spec(2)],
        out_specs=pl.BlockSpec((heads, tile, RET_HEAD_DIM),
                               lambda b, h, t: (h, b * per_b + jnp.where(t < n_tiles, n_tiles - 1, 2 * n_tiles - 1 - t), 0)),
        out_shape=jax.ShapeDtypeStruct((n_rh, batch * seq, RET_HEAD_DIM), BF16),
        scratch_shapes=[pltpu.VMEM((heads, seq, RET_HEAD_DIM), F32),
                        pltpu.VMEM((heads, RET_HEAD_DIM, RET_HEAD_DIM), F32),
                        pltpu.VMEM((heads, RET_HEAD_DIM, RET_HEAD_DIM), F32),
                        pltpu.VMEM((heads, RET_CHUNK, RET_CHUNK), F32)],
        compiler_params=_cparams(("arbitrary", "arbitrary", "arbitrary")),
        name="ret",
    )(lg, zqk, zqk, zvg, zvg)


def _outproj_kernel(a_ref, r_ref, x_ref, mod_ref, g0_ref, b0_ref, g1_ref, b1_ref, w_ref, x1_ref, hp_ref,
                    *, n_ah, n_rh, alpha):
    mix = jnp.concatenate([a_ref[h] for h in range(n_ah)] + [r_ref[h] for h in range(n_rh)], axis=-1)
    acc = jnp.dot(mix, w_ref[...], preferred_element_type=F32)
    xn = _ln(x_ref[...], g0_ref[...], b0_ref[...])
    y = alpha * xn + (1.0 + mod_ref[0, 2:3, :]) * acc
    x1 = _ln(y, g1_ref[...], b1_ref[...])
    x1_ref[...] = x1
    _store_token_tiles(hp_ref, _pack_bf16_pairs(x1 * (1.0 + mod_ref[0, 4:5, :]) + mod_ref[0, 3:4, :]))


def _outproj(attn, r, x2, mod3, g0, b0, g1, b1, w_out_b, seq, tm, alpha):
    t, d = x2.shape
    n_ah, n_rh = attn.shape[0], r.shape[0]
    per_b = seq // tm
    per = d // 2 // LANES
    row = lambda i: (i, 0)
    vec = pl.BlockSpec((1, d), lambda i: (0, 0))
    return pl.pallas_call(
        functools.partial(_outproj_kernel, n_ah=n_ah, n_rh=n_rh, alpha=alpha),
        grid=(t // tm,),
        in_specs=[pl.BlockSpec((n_ah, tm, ATTN_HEAD_DIM), lambda i: (0, i, 0)),
                  pl.BlockSpec((n_rh, tm, RET_HEAD_DIM), lambda i: (0, i, 0)),
                  pl.BlockSpec((tm, d), row),
                  pl.BlockSpec((1, 6, d), lambda i: (i // per_b, 0, 0)),
                  vec, vec, vec, vec,
                  pl.BlockSpec((d, d), lambda i: (0, 0))],
        out_specs=[pl.BlockSpec((tm, d), row), pl.BlockSpec((tm * per, LANES), row)],
        out_shape=[jax.ShapeDtypeStruct((t, d), F32), jax.ShapeDtypeStruct((t * per, LANES), U32)],
        compiler_params=_cparams(("arbitrary",)),
        name="outproj",
    )(attn, r, x2, mod3, g0, b0, g1, b1, w_out_b)


def _first_argmax(rows):
    best, idx = rows[0], jnp.zeros(rows[0].shape, I32)
    for e in range(1, len(rows)):
        better = rows[e] > best
        idx = jnp.where(better, e, idx)
        best = jnp.maximum(best, rows[e])
    return best, idx


def _router_kernel(h_ref, w_ref, b_ref, u_ref, eid_ref, wt_ref, rank_ref, cnt_ref, carry_ref, *, per):
    i = pl.program_id(0)

    @pl.when(i == 0)
    def _():
        carry_ref[...] = jnp.zeros_like(carry_ref)

    tm = eid_ref.shape[2]
    lt = lax.dot_general(w_ref[...], _load_token_tiles(h_ref, tm, per), (((1,), (1,)), ((), ())),
                         preferred_element_type=F32) + b_ref[...]
    grow = [lt[g:g + 1, :] for g in range(N_GROUPS)]
    gmax, gsel = _first_argmax(grow)
    gsum = grow[0] * 0.0
    for g in range(N_GROUPS):
        gsum = gsum + jnp.exp(grow[g] - gmax)
    pg = 1.0 / gsum
    srow = []
    for e in range(EXPERTS_PER_GROUP):
        r = lt[N_GROUPS + e:N_GROUPS + e + 1, :]
        for g in range(1, N_GROUPS):
            o = N_GROUPS + g * EXPERTS_PER_GROUP + e
            r = jnp.where(gsel == g, lt[o:o + 1, :], r)
        srow.append(r)
    v1, i1 = _first_argmax(srow)
    v2, i2 = _first_argmax([jnp.where(i1 == e, -jnp.inf, srow[e]) for e in range(EXPERTS_PER_GROUP)])
    e2 = jnp.exp(v2 - v1)
    den = 1.0 + e2
    wt_ref[0:1, :] = (1.0 / den) * pg
    wt_ref[1:2, :] = (e2 / den) * pg
    eid0 = gsel * EXPERTS_PER_GROUP + i1
    eid1 = gsel * EXPERTS_PER_GROUP + i2
    eid_ref[0, 0:1, :] = eid0
    eid_ref[0, 1:2, :] = eid1
    erow = lax.broadcasted_iota(I32, (N_EXPERTS, tm), 0)
    oh0 = (erow == eid0).astype(F32)
    oh1 = (erow == eid1).astype(F32)
    oh = oh0 + oh1
    incl = jnp.dot(oh.astype(BF16), u_ref[...], preferred_element_type=F32)
    before = carry_ref[:, 0:1] + incl - oh
    rank_ref[0, 0:1, :] = jnp.sum(oh0 * before, axis=0, keepdims=True).astype(I32)
    rank_ref[0, 1:2, :] = jnp.sum(oh1 * before, axis=0, keepdims=True).astype(I32)
    carry = carry_ref[...] + jnp.sum(oh, axis=1, keepdims=True)
    carry_ref[...] = carry
    cnt_ref[...] = carry


def _router(hp, wr, br, tm):
    d = wr.shape[1]
    per = d // 2 // LANES
    t = hp.shape[0] // per
    tri = (lax.broadcasted_iota(I32, (tm, tm), 0) <= lax.broadcasted_iota(I32, (tm, tm), 1)).astype(BF16)
    tile3 = pl.BlockSpec((1, 2, tm), lambda i: (i, 0, 0))
    return pl.pallas_call(
        functools.partial(_router_kernel, per=per),
        grid=(t // tm,),
        in_specs=[pl.BlockSpec((tm * per, LANES), lambda i: (i, 0)),
                  pl.BlockSpec((ROUTER_ROWS, d), lambda i: (0, 0)),
                  pl.BlockSpec((ROUTER_ROWS, 1), lambda i: (0, 0)),
                  pl.BlockSpec((tm, tm), lambda i: (0, 0))],
        out_specs=[tile3, pl.BlockSpec((2, tm), lambda i: (0, i)), tile3,
                   pl.BlockSpec((N_EXPERTS, LANES), lambda i: (0, 0))],
        out_shape=[jax.ShapeDtypeStruct((t // tm, 2, tm), I32), jax.ShapeDtypeStruct((2, t), F32),
                   jax.ShapeDtypeStruct((t // tm, 2, tm), I32), jax.ShapeDtypeStruct((N_EXPERTS, LANES), F32)],
        scratch_shapes=[pltpu.VMEM((N_EXPERTS, LANES), F32)],
        compiler_params=_cparams(("arbitrary",)),
        name="router",
    )(hp, wr, br, tri)


def _tile_copy(src, s_row, dst, d_row, per, sem):
    return pltpu.make_async_copy(src.at[pl.ds(pl.multiple_of(s_row, per), per), :],
                                 dst.at[pl.ds(pl.multiple_of(d_row, per), per), :], sem)


def _dispatch_kernel(lo_ref, hi_ref, dst_ref, h_ref, xb_hbm, stage, zero, sem, *, per):
    i = pl.program_id(0)
    n = pl.num_programs(0)
    slot = i % 2
    tm = dst_ref.shape[2] // 2

    def wait_tile(s):
        pltpu.make_async_copy(xb_hbm.at[pl.ds(0, 2 * tm * per), :], xb_hbm.at[pl.ds(0, 2 * tm * per), :], sem.at[s]).wait()

    @pl.when(i >= 2)
    def _():
        wait_tile(slot)

    stage[slot] = h_ref[...]

    def body(r, carry):
        for k in range(2):
            _tile_copy(stage.at[slot], r * per, xb_hbm, dst_ref[0, 0, k * tm + r], per, sem.at[slot]).start(priority=k)
        return carry

    lax.fori_loop(0, tm, body, 0, unroll=8)

    @pl.when(i == n - 1)
    def _():
        @pl.when(n >= 2)
        def _():
            wait_tile(1 - slot)
        wait_tile(slot)
        zero[...] = jnp.zeros_like(zero)

        def fill(e, carry):
            def one(s, c):
                _tile_copy(zero, 0, xb_hbm, s * per, per, sem.at[2]).start()
                return c
            lax.fori_loop(lo_ref[e], hi_ref[e], one, 0)

            def one_wait(s, c):
                _tile_copy(zero, 0, xb_hbm, s * per, per, sem.at[2]).wait()
                return c
            lax.fori_loop(lo_ref[e], hi_ref[e], one_wait, 0)
            return carry

        lax.fori_loop(0, N_EXPERTS + 1, fill, 0)


def _dispatch(fill_lo, fill_hi, drow3, hp, n_slots, per):
    n, tm = drow3.shape[0], drow3.shape[2] // 2
    smem3 = pl.BlockSpec((1, 1, 2 * tm), lambda i, *_: (i, 0, 0), memory_space=pltpu.SMEM)
    grid_spec = pltpu.PrefetchScalarGridSpec(
        num_scalar_prefetch=2,
        grid=(n,),
        in_specs=[smem3, pl.BlockSpec((tm * per, LANES), lambda i, *_: (i, 0))],
        out_specs=pl.BlockSpec(memory_space=pl.ANY),
        scratch_shapes=[pltpu.VMEM((2, tm * per, LANES), U32), pltpu.VMEM((per, LANES), U32),
                        pltpu.SemaphoreType.DMA((3,))])
    return pl.pallas_call(
        functools.partial(_dispatch_kernel, per=per),
        grid_spec=grid_spec,
        out_shape=jax.ShapeDtypeStruct((n_slots * per, LANES), U32),
        compiler_params=_cparams(("arbitrary",)),
        name="dispatch",
    )(fill_lo, fill_hi, drow3, hp)


def _expert_kernel(be_ref, first_ref, par_ref, nxt_ref, has_ref, x_ref, w1_hbm, w3_hbm, w2_hbm, y_ref,
                   wf1, wf3, wf2, w1b, w3b, w2b, sem, *, per):
    i = pl.program_id(0)

    def weight_copies(e, s):
        return [pltpu.make_async_copy(src.at[e], dst.at[s], sem.at[s])
                for src, dst in ((w1_hbm, wf1), (w3_hbm, wf3), (w2_hbm, wf2))]

    @pl.when(i == 0)
    def _():
        for cp in weight_copies(be_ref[0], 0):
            cp.start()

    @pl.when(first_ref[i] == 1)
    def _():
        s = par_ref[i]
        for cp in weight_copies(be_ref[i], s):
            cp.wait()

        @pl.when(has_ref[i] == 1)
        def _():
            for cp in weight_copies(nxt_ref[i], 1 - s):
                cp.start()

        w1b[...] = wf1[s].astype(BF16)
        w3b[...] = wf3[s].astype(BF16)
        w2b[...] = wf2[s].astype(BF16)

    x = _load_token_tiles(x_ref, MOE_BLOCK, per)
    a = jnp.dot(x, w1b[...], preferred_element_type=F32)
    b = jnp.dot(x, w3b[...], preferred_element_type=F32)
    mid = (jax.nn.silu(a) * b).astype(BF16)
    _store_token_tiles(y_ref, _pack_bf16_pairs(jnp.dot(mid, w2b[...], preferred_element_type=F32)))


def _experts(blk_e, xb, w1, w3, w2):
    d, ff = w1.shape[1], w1.shape[2]
    per = d // 2 // LANES
    nblk = xb.shape[0] // per // MOE_BLOCK
    rows = pl.BlockSpec((MOE_BLOCK * per, LANES), lambda i, *_: (i, 0))
    first = jnp.concatenate([jnp.ones((1,), I32), (blk_e[1:] != blk_e[:-1]).astype(I32)])
    parity = (jnp.cumsum(first) - 1) % 2
    nxt = jnp.min(jnp.where(blk_e[None, :] > blk_e[:, None], blk_e[None, :], N_EXPERTS), axis=1)
    has_next = (nxt < N_EXPERTS).astype(I32)
    nxt = jnp.minimum(nxt, N_EXPERTS - 1)
    grid_spec = pltpu.PrefetchScalarGridSpec(
        num_scalar_prefetch=5,
        grid=(nblk,),
        in_specs=[rows] + [pl.BlockSpec(memory_space=pl.ANY)] * 3,
        out_specs=rows,
        scratch_shapes=[pltpu.VMEM((2, d, ff), F32), pltpu.VMEM((2, d, ff), F32), pltpu.VMEM((2, ff, d), F32),
                        pltpu.VMEM((d, ff), BF16), pltpu.VMEM((d, ff), BF16), pltpu.VMEM((ff, d), BF16),
                        pltpu.SemaphoreType.DMA((2,))])
    return pl.pallas_call(
        functools.partial(_expert_kernel, per=per),
        grid_spec=grid_spec,
        out_shape=jax.ShapeDtypeStruct(xb.shape, U32),
        compiler_params=_cparams(("arbitrary",)),
        name="experts",
    )(blk_e, first, parity.astype(I32), nxt.astype(I32), has_next, xb, w1, w3, w2)


def _combine_kernel(src_ref, srcn_ref, y_hbm, wt_ref, x1_ref, mod_ref, g_ref, b_ref, o_ref, ybuf, sem, *, alpha, per):
    i = pl.program_id(0)
    n = pl.num_programs(0)
    slot = i % 2
    tm = x1_ref.shape[0]

    def wait(s):
        for k in range(2):
            pltpu.make_async_copy(y_hbm.at[pl.ds(0, tm * per), :], ybuf.at[s, k], sem.at[s]).wait()

    @pl.when(i == 0)
    def _():
        def body(r, carry):
            for k in range(2):
                _tile_copy(y_hbm, src_ref[0, 0, k * tm + r], ybuf.at[0, k], r * per, per, sem.at[0]).start(priority=k)
            return carry
        lax.fori_loop(0, tm, body, 0, unroll=8)

    for r in range(tm):
        for k in range(2):
            _tile_copy(y_hbm, srcn_ref[0, 0, k * tm + r], ybuf.at[1 - slot, k], r * per, per, sem.at[1 - slot]).start(priority=k)

    wait(slot)
    wt = wt_ref[...]
    y0 = _load_token_tiles(ybuf.at[slot, 0], tm, per, F32)
    y1 = _load_token_tiles(ybuf.at[slot, 1], tm, per, F32)
    ffn = wt[:, 0:1] * y0 + wt[:, 1:2] * y1
    y = alpha * x1_ref[...] + (1.0 + mod_ref[0, 5:6, :]) * ffn
    o_ref[...] = _ln(y, g_ref[...], b_ref[...])

    @pl.when(i == n - 1)
    def _():
        wait(1 - slot)


def _combine(drow3, yb, wt_t, x1, mod3, g2, b2, seq, alpha):
    t, d = x1.shape
    per = d // 2 // LANES
    n, tm = drow3.shape[0], drow3.shape[2] // 2
    per_b = seq // tm
    cur = pl.BlockSpec((1, 1, 2 * tm), lambda i: (i, 0, 0), memory_space=pltpu.SMEM)
    nxt = pl.BlockSpec((1, 1, 2 * tm), lambda i: (jnp.minimum(i + 1, n - 1), 0, 0), memory_space=pltpu.SMEM)
    vec = pl.BlockSpec((1, d), lambda i: (0, 0))
    return pl.pallas_call(
        functools.partial(_combine_kernel, alpha=alpha, per=per),
        grid=(n,),
        in_specs=[cur, nxt,
                  pl.BlockSpec(memory_space=pl.ANY),
                  pl.BlockSpec((tm, 2), lambda i: (i, 0)),
                  pl.BlockSpec((tm, d), lambda i: (i, 0)),
                  pl.BlockSpec((1, 6, d), lambda i: (i // per_b, 0, 0)),
                  vec, vec],
        out_specs=pl.BlockSpec((tm, d), lambda i: (i, 0)),
        out_shape=jax.ShapeDtypeStruct((t, d), F32),
        scratch_shapes=[pltpu.VMEM((2, 2, tm * per, LANES), U32), pltpu.SemaphoreType.DMA((2,))],
        compiler_params=_cparams(("arbitrary",)),
        name="combine",
    )(drow3, drow3, yb, wt_t, x1, mod3, g2, b2)


def _pick_tile(n, want):
    tm = min(n, want)
    assert n % tm == 0
    return tm


def kernel(x, c, positions, ln0_g, ln0_b, w_ada, b_ada, w_in, w_out, ret_log_decay_f, ret_log_decay_b,
           ln1_g, ln1_b, w_group, b_group, w_sub, b_sub, w1, w3, w2, ln2_g, ln2_b):
    batch, seq, d = x.shape
    depth = w_ada.shape[0]
    t = batch * seq
    n_ah = d // 2 // ATTN_HEAD_DIM
    n_rh = d // 2 // RET_HEAD_DIM
    assert depth == 1 and d % (2 * RET_HEAD_DIM) == 0 and batch <= 8
    assert seq % (QT * CHAINS) == 0 and seq % (max(DILATIONS) * PERM_ROWS) == 0 and seq // max(DILATIONS) >= KW
    alpha = (2 * depth) ** 0.25

    inv_rope = ROPE_THETA ** (-jnp.arange(0, ROPE_DIM, 2, dtype=F32) / ROPE_DIM)
    inv_ret = RET_THETA ** (-jnp.linspace(0.0, 1.0, RET_HEAD_DIM // 2, dtype=F32))
    fac = jnp.tile(inv_rope, TOK_PER_ROW).reshape(1, LANES)
    fr = inv_ret.reshape(1, LANES)
    posb = jnp.broadcast_to(positions.astype(F32).reshape(t, 1), (t, LANES))
    posc = jnp.repeat(positions.astype(F32).reshape(t // TOK_PER_ROW, TOK_PER_ROW), ROPE_DIM // 2, axis=1)
    c8 = jnp.zeros((8, d), F32).at[:batch].set(c)
    row = lambda v: v.reshape(1, d)

    xs = x.reshape(t, d)
    mod = _ada(c8, w_ada[0], b_ada[0].reshape(1, -1))
    mod3 = mod[:batch].reshape(batch, 6, d)
    h, tabs = _prep(xs, posb, posc, mod3, row(ln0_g), row(ln0_b), fac, fr, seq, _pick_tile(seq, 512))
    w_in_b = w_in[0].astype(BF16)
    tmp = _pick_tile(seq, 2048)
    z_aqk = _proj(_proj_attn_kernel, lambda j: j, 2, h, w_in_b, tabs, tmp, "proj_attn")
    z_rqk = _proj(_proj_ret_kernel, lambda j: 3 + j, 2, h, w_in_b, tabs, _pick_tile(seq, 1024), "proj_ret")
    z_pl = _proj(_proj_plain_kernel, lambda j: 2 + 3 * jnp.minimum(j, 1) + jnp.maximum(j - 1, 0), 3,
                 h, w_in_b, None, tmp, "proj_plain")
    attn = _attn(z_aqk, z_pl, batch, seq, n_ah)
    lg = jnp.stack([ret_log_decay_f[0], ret_log_decay_b[0]]).astype(F32)
    r = _ret(z_rqk, z_pl, lg, batch, seq, n_rh, _pick_tile(seq, 2048))
    x1, hp = _outproj(attn, r, xs, mod3, row(ln0_g), row(ln0_b), row(ln1_g[0]), row(ln1_b[0]),
                      w_out[0].astype(BF16), seq, _pick_tile(seq, 512), alpha)
    wr = jnp.zeros((ROUTER_ROWS, d), F32)
    wr = wr.at[:N_GROUPS].set(w_group[0].T)
    wr = wr.at[N_GROUPS:N_GROUPS + N_EXPERTS].set(w_sub[0].transpose(0, 2, 1).reshape(N_EXPERTS, d))
    br = jnp.zeros((ROUTER_ROWS, 1), F32)
    br = br.at[:N_GROUPS, 0].set(b_group[0]).at[N_GROUPS:N_GROUPS + N_EXPERTS, 0].set(b_sub[0].reshape(-1))
    eid3, wt, rank3, cnt = _router(hp, wr.astype(BF16), br, _pick_tile(seq, 512))
    counts = cnt[:, 0].astype(I32)
    padded = (counts + MOE_BLOCK - 1) // MOE_BLOCK * MOE_BLOCK
    pend = jnp.cumsum(padded)
    pstart = pend - padded
    n_slots = 2 * t + N_EXPERTS * MOE_BLOCK
    nblk = n_slots // MOE_BLOCK
    starts = jnp.arange(nblk, dtype=I32) * MOE_BLOCK
    blk_e = jnp.minimum(jnp.sum((pend[None, :] <= starts[:, None]).astype(I32), axis=1), N_EXPERTS - 1)
    fill_lo = jnp.concatenate([pstart + counts, pend[-1:]])
    fill_hi = jnp.concatenate([pend, jnp.full((1,), n_slots, I32)])
    per = d // 2 // LANES
    onehot = eid3[..., None] == jnp.arange(N_EXPERTS, dtype=I32)
    drow3 = ((rank3 + jnp.sum(jnp.where(onehot, pstart, 0), axis=-1)) * per).reshape(eid3.shape[0], 1, -1)
    xb = _dispatch(fill_lo, fill_hi, drow3, hp, n_slots, per)
    yb = _experts(blk_e, xb, w1[0], w3[0], w2[0])
    out = _combine(drow3, yb, wt.T, x1, mod3, row(ln2_g[0]), row(ln2_b[0]), seq, alpha)
    return out.reshape(batch, seq, d)
```

```python
import functools

import jax
import jax.numpy as jnp
from jax import lax
from jax.experimental import pallas as pl
from jax.experimental.pallas import tpu as pltpu

F32 = jnp.float32
BF16 = jnp.bfloat16
I32 = jnp.int32
U32 = jnp.uint32

LANES = 128
ATTN_HEAD_DIM = 128
RET_HEAD_DIM = 256
DILATIONS = (1, 4, 16)
BAND = 64
QT = 128
KW = QT + 2 * BAND
CHAINS = 64
PERM_ROWS = 256
PERM_UNROLL = 8
ROPE_THETA = 500000.0
ROPE_DIM = ATTN_HEAD_DIM // 4
TOK_PER_ROW = LANES // (ROPE_DIM // 2)
RET_THETA = 10000.0
RET_CHUNK = 128
RET_UNROLL = 8
N_GROUPS = 4
EXPERTS_PER_GROUP = 8
N_EXPERTS = N_GROUPS * EXPERTS_PER_GROUP
MOE_BLOCK = 256
LN_EPS = 1e-5
MASK_VALUE = -1e30
ROUTER_ROWS = 48
HI_MASK = 0xFFFF0000
VMEM_LIMIT = 56 * 1024 * 1024


def _cparams(sem, vmem=VMEM_LIMIT):
    return pltpu.CompilerParams(dimension_semantics=sem, vmem_limit_bytes=vmem)


def _ln(x, g, b):
    mu = jnp.mean(x, axis=-1, keepdims=True)
    xc = x - mu
    var = jnp.mean(xc * xc, axis=-1, keepdims=True)
    return xc * lax.rsqrt(var + LN_EPS) * g + b


def _pack_bf16_pairs(h):
    bits = lax.bitcast_convert_type(h.astype(BF16).astype(F32), U32)
    n = h.shape[1] // 2
    return (bits[:, :n] >> 16) | (bits[:, n:] & jnp.uint32(HI_MASK))


def _store_token_tiles(ref, packed):
    m, n = packed.shape
    per = n // LANES
    for s in range(per):
        ref[pl.ds(s, m, stride=per), :] = packed[:, s * LANES:(s + 1) * LANES]


def _load_token_tiles(ref, m, per, dtype=BF16):
    slabs = [ref[pl.ds(s, m, stride=per), :] for s in range(per)]
    lo = [lax.bitcast_convert_type(p << 16, F32) for p in slabs]
    hi = [lax.bitcast_convert_type(p & jnp.uint32(HI_MASK), F32) for p in slabs]
    return jnp.concatenate(lo + hi, axis=-1).astype(dtype)


def _ada_kernel(c_ref, w_ref, b_ref, o_ref):
    cs = jax.nn.silu(c_ref[...])
    o_ref[...] = jnp.dot(cs.astype(BF16), w_ref[...].astype(BF16), preferred_element_type=F32) + b_ref[...]


def _ada(c8, w_ada, b_ada):
    d, n = w_ada.shape
    tn = min(n, 512)
    return pl.pallas_call(
        _ada_kernel,
        grid=(n // tn,),
        in_specs=[pl.BlockSpec((8, d), lambda j: (0, 0)),
                  pl.BlockSpec((d, tn), lambda j: (0, j)),
                  pl.BlockSpec((1, tn), lambda j: (0, j))],
        out_specs=pl.BlockSpec((8, tn), lambda j: (0, j)),
        out_shape=jax.ShapeDtypeStruct((8, n), F32),
        compiler_params=_cparams(("arbitrary",)),
        name="ada",
    )(c8, w_ada, b_ada)


def _prep_kernel(x_ref, pos_ref, posc_ref, mod_ref, g_ref, b_ref, fac_ref, fr_ref, xn_ref, h_ref, tab_ref):
    xn = _ln(x_ref[...], g_ref[...], b_ref[...])
    xn_ref[...] = xn
    h_ref[...] = (xn * (1.0 + mod_ref[0, 1:2, :]) + mod_ref[0, 0:1, :]).astype(BF16)
    half = ROPE_DIM // 2
    angc = posc_ref[...] * fac_ref[...]
    cc, sc = jnp.cos(angc), jnp.sin(angc)
    rows_c = angc.shape[0]
    lane = lax.broadcasted_iota(I32, angc.shape, 1)
    first, second = lane < half, (lane >= half) & (lane < 2 * half)
    for ts in range(TOK_PER_ROW):
        shift = (LANES - ts * half) % LANES
        c = pltpu.roll(cc, shift, 1) if shift else cc
        s = pltpu.roll(sc, shift, 1) if shift else sc
        rows = pl.ds(ts, rows_c, stride=TOK_PER_ROW)
        tab_ref.at[0][rows, :] = jnp.where(first, c, jnp.where(second, pltpu.roll(c, half, 1), 1.0))
        tab_ref.at[1][rows, :] = jnp.where(first, -s, 0.0)
        tab_ref.at[2][rows, :] = jnp.where(second, pltpu.roll(s, half, 1), 0.0)
    angr = pos_ref[...] * fr_ref[...]
    tab_ref[3] = jnp.cos(angr)
    tab_ref[4] = jnp.sin(angr)


def _prep(x2, posb, posc, mod3, ln0_g, ln0_b, fac, fr, seq, tm):
    t, d = x2.shape
    per_b = seq // tm
    vec = lambda n: pl.BlockSpec((1, n), lambda i: (0, 0))
    return pl.pallas_call(
        _prep_kernel,
        grid=(t // tm,),
        in_specs=[pl.BlockSpec((tm, d), lambda i: (i, 0)),
                  pl.BlockSpec((tm, LANES), lambda i: (i, 0)),
                  pl.BlockSpec((tm // TOK_PER_ROW, LANES), lambda i: (i, 0)),
                  pl.BlockSpec((1, 6, d), lambda i: (i // per_b, 0, 0)),
                  vec(d), vec(d), vec(LANES), vec(LANES)],
        out_specs=[pl.BlockSpec((tm, d), lambda i: (i, 0)), pl.BlockSpec((tm, d), lambda i: (i, 0)),
                   pl.BlockSpec((5, tm, LANES), lambda i: (0, i, 0))],
        out_shape=[jax.ShapeDtypeStruct((t, d), F32), jax.ShapeDtypeStruct((t, d), BF16),
                   jax.ShapeDtypeStruct((5, t, LANES), F32)],
        compiler_params=_cparams(("arbitrary",)),
        name="prep",
    )(x2, posb, posc, mod3, ln0_g, ln0_b, fac, fr)


def _proj_chunks(h_ref, w_ref, n_sub):
    for hc in range(n_sub):
        yield hc, jnp.dot(h_ref[...], w_ref[:, hc * 2 * LANES:(hc + 1) * 2 * LANES], preferred_element_type=F32)


def _proj_attn_kernel(h_ref, w_ref, tab_ref, z_ref, *, n_sub):
    scale = jnp.where(pl.program_id(0) == 0, ATTN_HEAD_DIM ** -0.5, 1.0).astype(F32)
    ca, s1, s2 = tab_ref[0] * scale, tab_ref[1] * scale, tab_ref[2] * scale
    half = ROPE_DIM // 2
    for hc, res in _proj_chunks(h_ref, w_ref, n_sub):
        for lanes in (slice(0, LANES), slice(LANES, 2 * LANES)):
            x = res[:, lanes]
            r = x * ca + pltpu.roll(x, LANES - half, 1) * s1 + pltpu.roll(x, half, 1) * s2
            z_ref[hc, :, lanes] = r.astype(BF16)


def _proj_ret_kernel(h_ref, w_ref, tab_ref, z_ref, *, n_sub):
    scale = jnp.where(pl.program_id(0) == 1, RET_HEAD_DIM ** -0.5, 1.0).astype(F32)
    cr, sr = tab_ref[3] * scale, tab_ref[4] * scale
    for hc, res in _proj_chunks(h_ref, w_ref, n_sub):
        x1, x2 = res[:, :LANES], res[:, LANES:]
        z_ref[hc, :, :LANES] = (x1 * cr - x2 * sr).astype(BF16)
        z_ref[hc, :, LANES:] = (x2 * cr + x1 * sr).astype(BF16)


def _proj_plain_kernel(h_ref, w_ref, z_ref, *, n_sub):
    for hc, res in _proj_chunks(h_ref, w_ref, n_sub):
        z_ref[hc] = res.astype(BF16)


def _proj(body, col_of, n_seg, h, w_in_b, tabs, tm, name):
    t, d = h.shape
    tn = d // 2
    n_sub = tn // (2 * LANES)
    in_specs = [pl.BlockSpec((tm, d), lambda j, i: (i, 0)),
                pl.BlockSpec((d, tn), lambda j, i: (0, col_of(j)))]
    args = [h, w_in_b]
    if tabs is not None:
        in_specs.append(pl.BlockSpec((5, tm, LANES), lambda j, i: (0, i, 0)))
        args.append(tabs)
    return pl.pallas_call(
        functools.partial(body, n_sub=n_sub),
        grid=(n_seg, t // tm),
        in_specs=in_specs,
        out_specs=pl.BlockSpec((n_sub, tm, 2 * LANES), lambda j, i: (j, i, 0)),
        out_shape=jax.ShapeDtypeStruct((n_seg * n_sub, t, 2 * LANES), BF16),
        compiler_params=_cparams(("arbitrary", "arbitrary")),
        name=name,
    )(*args)


def _attn_tile(q, k, v, bias):
    s = lax.dot_general(q, k, (((1,), (1,)), ((), ())), preferred_element_type=F32) + bias
    if s.shape[1] == 2 * LANES:
        m = jnp.max(jnp.maximum(s[:, :LANES], s[:, LANES:]), axis=-1, keepdims=True)
        p = jnp.exp(s - m)
        den = jnp.sum(p[:, :LANES] + p[:, LANES:], axis=-1, keepdims=True)
    else:
        m = jnp.max(s, axis=-1, keepdims=True)
        p = jnp.exp(s - m)
        den = jnp.sum(p, axis=-1, keepdims=True)
    acc = jnp.dot(p.astype(BF16), v, preferred_element_type=F32)
    return acc, jnp.broadcast_to(m, acc.shape), jnp.broadcast_to(den, acc.shape)


def _attn_kernel(q_ref, k_ref, v_ref, bias_ref, p4_ref, p16_ref, o_ref,
                 acc_ref, m_ref, d_ref, qp_ref, kp_ref, vp_ref, *, seq):
    n_items = seq // QT

    def window(n, length):
        qs = n * QT
        ks = jnp.clip(qs - BAND, 0, length - KW)
        return qs, ks, bias_ref[lax.div(qs - ks, BAND)]

    def perm_body(it, carry):
        for u in range(PERM_UNROLL):
            blk = it * PERM_UNROLL + u
            r0 = pl.multiple_of(blk * PERM_ROWS, PERM_ROWS)
            qk = jnp.concatenate([q_ref[0, 0, pl.ds(r0, PERM_ROWS), :], k_ref[0, 0, pl.ds(r0, PERM_ROWS), :]], axis=-1)
            v = v_ref[0, 0, pl.ds(r0, PERM_ROWS), :]
            for slot, (dil, p_ref) in enumerate(((4, p4_ref), (16, p16_ref))):
                length, w = seq // dil, PERM_ROWS // dil
                qkp = jnp.dot(p_ref[...], qk, preferred_element_type=F32).astype(BF16)
                vp = jnp.dot(p_ref[...], v, preferred_element_type=F32).astype(BF16)
                for r in range(dil):
                    rows = pl.ds(pl.multiple_of(r * length + blk * w, w), w)
                    qp_ref[slot, rows, :] = qkp[r * w:(r + 1) * w, :ATTN_HEAD_DIM]
                    kp_ref[slot, rows, :] = qkp[r * w:(r + 1) * w, ATTN_HEAD_DIM:]
                    vp_ref[slot, rows, :] = vp[r * w:(r + 1) * w, :]
        return carry

    lax.fori_loop(0, seq // PERM_ROWS // PERM_UNROLL, perm_body, 0)

    def dilated_tile(item, dil, slot):
        length = seq // dil
        n, r = lax.div(item, dil), lax.rem(item, dil)
        qs, ks, bias = window(n, length)
        qrow = pl.multiple_of(r * length + qs, QT)
        krow = pl.multiple_of(r * length + ks, BAND)
        out = _attn_tile(qp_ref[slot, pl.ds(qrow, QT), :], kp_ref[slot, pl.ds(krow, KW), :],
                         vp_ref[slot, pl.ds(krow, KW), :], bias)
        return out, pl.ds(qs * dil + r, QT, stride=dil)

    def body16(it, carry):
        for g in range(CHAINS):
            (acc, m, den), rows = dilated_tile(it * CHAINS + g, 16, 1)
            acc_ref[rows, :] = acc
            m_ref[rows, :] = m
            d_ref[rows, :] = den
        return carry

    lax.fori_loop(0, n_items // CHAINS, body16, 0)

    def body4(it, carry):
        for g in range(CHAINS):
            (acc, m, den), rows = dilated_tile(it * CHAINS + g, 4, 0)
            m0 = m_ref[rows, :]
            mn = jnp.maximum(m0, m)
            a, b = jnp.exp(m0 - mn), jnp.exp(m - mn)
            acc_ref[rows, :] = a * acc_ref[rows, :] + b * acc
            d_ref[rows, :] = a * d_ref[rows, :] + b * den
            m_ref[rows, :] = mn
        return carry

    lax.fori_loop(0, n_items // CHAINS, body4, 0)

    def body1(it, carry):
        for g in range(CHAINS):
            qs, ks, bias = window(it * CHAINS + g, seq)
            qs, ks = pl.multiple_of(qs, QT), pl.multiple_of(ks, BAND)
            acc, m, den = _attn_tile(q_ref[0, 0, pl.ds(qs, QT), :], k_ref[0, 0, pl.ds(ks, KW), :],
                                     v_ref[0, 0, pl.ds(ks, KW), :], bias)
            rows = pl.ds(qs, QT)
            m0 = m_ref[rows, :]
            mn = jnp.maximum(m0, m)
            a, b = jnp.exp(m0 - mn), jnp.exp(m - mn)
            num = a * acc_ref[rows, :] + b * acc
            o_ref[0, 0, rows, :] = (num / (a * d_ref[rows, :] + b * den)).astype(BF16)
        return carry

    lax.fori_loop(0, n_items // CHAINS, body1, 0)


def _perm_matrix(dil):
    w = PERM_ROWS // dil
    i = jnp.arange(PERM_ROWS)
    src = (i % w) * dil + i // w
    return (src[:, None] == jnp.arange(PERM_ROWS)[None, :]).astype(BF16)


def _attn(zqk, zv_, batch, seq, n_ah):
    n_sub = n_ah // 2
    zqk = zqk.reshape(zqk.shape[0], batch, seq, 2 * LANES)
    zv_ = zv_.reshape(zv_.shape[0], batch, seq, 2 * LANES)
    blk = (1, 1, seq, ATTN_HEAD_DIM)
    i = jnp.arange(QT)[:, None]
    jj = jnp.arange(KW)[None, :]
    bias = jnp.stack([jnp.where(jnp.abs(i + o * BAND - jj) <= BAND, 0.0, MASK_VALUE) for o in range(3)]).astype(F32)
    const = lambda shape: pl.BlockSpec(shape, lambda b, h: (0,) * len(shape))
    o = pl.pallas_call(
        functools.partial(_attn_kernel, seq=seq),
        grid=(batch, n_ah),
        in_specs=[pl.BlockSpec(blk, lambda b, h: (h // 2, b, 0, h % 2)),
                  pl.BlockSpec(blk, lambda b, h: (n_sub + h // 2, b, 0, h % 2)),
                  pl.BlockSpec(blk, lambda b, h: (h // 2, b, 0, h % 2)),
                  const((3, QT, KW)), const((PERM_ROWS, PERM_ROWS)), const((PERM_ROWS, PERM_ROWS))],
        out_specs=pl.BlockSpec(blk, lambda b, h: (h, b, 0, 0)),
        out_shape=jax.ShapeDtypeStruct((n_ah, batch, seq, ATTN_HEAD_DIM), BF16),
        scratch_shapes=[pltpu.VMEM((seq, LANES), F32)] * 3
        + [pltpu.VMEM((2, seq, ATTN_HEAD_DIM), BF16)] * 3,
        compiler_params=_cparams(("arbitrary", "arbitrary")),
        name="attn",
    )(zqk, zqk, zv_, bias, _perm_matrix(4), _perm_matrix(16))
    return o.reshape(n_ah, batch * seq, ATTN_HEAD_DIM)


def _ret_kernel(lg_ref, q_ref, k_ref, v_ref, g_ref, o_ref, y_ref, sf_ref, sb_ref, dmat_ref, *, n_tiles, tile, heads):
    hp = pl.program_id(1)
    t = pl.program_id(2)
    c = RET_CHUNK
    lgf = [lg_ref[0, hp * heads + hd] for hd in range(heads)]
    lgb = [lg_ref[1, hp * heads + hd] for hd in range(heads)]
    col = lax.broadcasted_iota(I32, (c, 1), 0).astype(F32)
    row = lax.broadcasted_iota(I32, (1, c), 1).astype(F32)
    n_iter = tile // c // RET_UNROLL

    @pl.when(t == 0)
    def _():
        sf_ref[...] = jnp.zeros_like(sf_ref)
        sb_ref[...] = jnp.zeros_like(sb_ref)
        diff = (lax.broadcasted_iota(I32, (c, c), 0) - lax.broadcasted_iota(I32, (c, c), 1)).astype(F32)
        for hd in range(heads):
            dmat_ref[hd] = jnp.where(diff >= 0, jnp.exp(lgf[hd] * jnp.maximum(diff, 0.0)), 0.0) \
                + jnp.where(diff < 0, jnp.exp(lgb[hd] * jnp.maximum(-diff, 0.0)), 0.0)

    def chunk(hd, r0):
        rows = pl.ds(pl.multiple_of(r0, c), c)
        return rows, q_ref[hd, rows, :], k_ref[hd, rows, :], v_ref[hd, rows, :]

    @pl.when(t < n_tiles)
    def _():
        xi = [jnp.exp(lg * (col + 1.0)) for lg in lgf]
        zeta = [jnp.exp(lg * (c - 1.0 - row)) for lg in lgf]
        cdec = [jnp.exp(lg * jnp.full((1, 1), float(c), F32)) for lg in lgf]
        base = t * tile

        def body(it, carry):
            state = [sf_ref[hd] for hd in range(heads)]
            for u in range(RET_UNROLL):
                r0 = (it * RET_UNROLL + u) * c
                for hd in range(heads):
                    _, q, k, v = chunk(hd, r0)
                    s = lax.dot_general(q, k, (((1,), (1,)), ((), ())), preferred_element_type=F32) * dmat_ref[hd]
                    y = jnp.dot(s.astype(BF16), v, preferred_element_type=F32)
                    y = y + jnp.dot(q, state[hd].astype(BF16), preferred_element_type=F32) * xi[hd]
                    kt = (k.astype(F32).T * zeta[hd]).astype(BF16)
                    state[hd] = state[hd] * cdec[hd] + jnp.dot(kt, v, preferred_element_type=F32)
                    y_ref[hd, pl.ds(pl.multiple_of(base + r0, c), c), :] = y
            for hd in range(heads):
                sf_ref[hd] = state[hd]
            return carry

        lax.fori_loop(0, n_iter, body, 0)

    @pl.when(t >= n_tiles)
    def _():
        xi = [jnp.exp(lg * (c - col)) for lg in lgb]
        zeta = [jnp.exp(lg * row) for lg in lgb]
        cdec = [jnp.exp(lg * jnp.full((1, 1), float(c), F32)) for lg in lgb]
        base = (2 * n_tiles - 1 - t) * tile

        def body(it, carry):
            state = [sb_ref[hd] for hd in range(heads)]
            for u in range(RET_UNROLL):
                r0 = (tile // c - 1 - (it * RET_UNROLL + u)) * c
                for hd in range(heads):
                    rows, q, k, v = chunk(hd, r0)
                    y = y_ref[hd, pl.ds(pl.multiple_of(base + r0, c), c), :]
                    y = y + jnp.dot(q, state[hd].astype(BF16), preferred_element_type=F32) * xi[hd]
                    kt = (k.astype(F32).T * zeta[hd]).astype(BF16)
                    state[hd] = state[hd] * cdec[hd] + jnp.dot(kt, v, preferred_element_type=F32)
                    mu = jnp.mean(y, axis=-1, keepdims=True)
                    yc = y - mu
                    var = jnp.mean(yc * yc, axis=-1, keepdims=True)
                    yn = yc * lax.rsqrt(var + LN_EPS)
                    gate = jax.nn.silu(g_ref[hd, rows, :].astype(F32))
                    o_ref[hd, rows, :] = (gate * yn).astype(BF16)
            for hd in range(heads):
                sb_ref[hd] = state[hd]
            return carry

        lax.fori_loop(0, n_iter, body, 0)


def _ret(zqk, zvg, lg, batch, seq, n_rh, tile):
    n_tiles = seq // tile
    per_b = seq // tile
    heads = 2 if n_rh % 2 == 0 else 1
    groups = n_rh // heads

    def rows(t):
        return jnp.where(t < n_tiles, t, 2 * n_tiles - 1 - t)

    def spec(seg):
        return pl.BlockSpec((heads, tile, RET_HEAD_DIM), lambda b, h, t: (seg * groups + h, b * per_b + rows(t), 0))

    return pl.pallas_call(
        functools.partial(_ret_kernel, n_tiles=n_tiles, tile=tile, heads=heads),
        grid=(batch, groups, 2 * n_tiles),
        in_specs=[pl.BlockSpec(memory_space=pltpu.SMEM), spec(0), spec(1), spec(1), spec(2)],
        out_specs=pl.BlockSpec((heads, tile, RET_HEAD_DIM),
                               lambda b, h, t: (h, b * per_b + jnp.where(t < n_tiles, n_tiles - 1, 2 * n_tiles - 1 - t), 0)),
        out_shape=jax.ShapeDtypeStruct((n_rh, batch * seq, RET_HEAD_DIM), BF16),
        scratch_shapes=[pltpu.VMEM((heads, seq, RET_HEAD_DIM), F32),
                        pltpu.VMEM((heads, RET_HEAD_DIM, RET_HEAD_DIM), F32),
                        pltpu.VMEM((heads, RET_HEAD_DIM, RET_HEAD_DIM), F32),
                        pltpu.VMEM((heads, RET_CHUNK, RET_CHUNK), F32)],
        compiler_params=_cparams(("arbitrary", "arbitrary", "arbitrary")),
        name="ret",
    )(lg, zqk, zqk, zvg, zvg)


def _outproj_kernel(a_ref, r_ref, xn_ref, mod_ref, g1_ref, b1_ref, w_ref, x1_ref, hp_ref, *, n_ah, n_rh, alpha):
    mix = jnp.concatenate([a_ref[h] for h in range(n_ah)] + [r_ref[h] for h in range(n_rh)], axis=-1)
    acc = jnp.dot(mix, w_ref[...], preferred_element_type=F32)
    y = alpha * xn_ref[...] + (1.0 + mod_ref[0, 2:3, :]) * acc
    x1 = _ln(y, g1_ref[...], b1_ref[...])
    x1_ref[...] = x1
    _store_token_tiles(hp_ref, _pack_bf16_pairs(x1 * (1.0 + mod_ref[0, 4:5, :]) + mod_ref[0, 3:4, :]))


def _outproj(attn, r, xn, mod3, g1, b1, w_out_b, seq, tm, alpha):
    t, d = xn.shape
    n_ah, n_rh = attn.shape[0], r.shape[0]
    per_b = seq // tm
    per = d // 2 // LANES
    row = lambda i: (i, 0)
    vec = pl.BlockSpec((1, d), lambda i: (0, 0))
    return pl.pallas_call(
        functools.partial(_outproj_kernel, n_ah=n_ah, n_rh=n_rh, alpha=alpha),
        grid=(t // tm,),
        in_specs=[pl.BlockSpec((n_ah, tm, ATTN_HEAD_DIM), lambda i: (0, i, 0)),
                  pl.BlockSpec((n_rh, tm, RET_HEAD_DIM), lambda i: (0, i, 0)),
                  pl.BlockSpec((tm, d), row),
                  pl.BlockSpec((1, 6, d), lambda i: (i // per_b, 0, 0)),
                  vec, vec,
                  pl.BlockSpec((d, d), lambda i: (0, 0))],
        out_specs=[pl.BlockSpec((tm, d), row), pl.BlockSpec((tm * per, LANES), row)],
        out_shape=[jax.ShapeDtypeStruct((t, d), F32), jax.ShapeDtypeStruct((t * per, LANES), U32)],
        compiler_params=_cparams(("arbitrary",)),
        name="outproj",
    )(attn, r, xn, mod3, g1, b1, w_out_b)


def _first_argmax(rows):
    best, idx = rows[0], jnp.zeros(rows[0].shape, I32)
    for e in range(1, len(rows)):
        better = rows[e] > best
        idx = jnp.where(better, e, idx)
        best = jnp.maximum(best, rows[e])
    return best, idx


def _router_kernel(h_ref, w_ref, b_ref, u_ref, eid_ref, wt_ref, rank_ref, cnt_ref, carry_ref, *, per):
    i = pl.program_id(0)

    @pl.when(i == 0)
    def _():
        carry_ref[...] = jnp.zeros_like(carry_ref)

    tm = eid_ref.shape[2]
    lt = lax.dot_general(w_ref[...], _load_token_tiles(h_ref, tm, per), (((1,), (1,)), ((), ())),
                         preferred_element_type=F32) + b_ref[...]
    grow = [lt[g:g + 1, :] for g in range(N_GROUPS)]
    gmax, gsel = _first_argmax(grow)
    gsum = grow[0] * 0.0
    for g in range(N_GROUPS):
        gsum = gsum + jnp.exp(grow[g] - gmax)
    pg = 1.0 / gsum
    srow = []
    for e in range(EXPERTS_PER_GROUP):
        r = lt[N_GROUPS + e:N_GROUPS + e + 1, :]
        for g in range(1, N_GROUPS):
            o = N_GROUPS + g * EXPERTS_PER_GROUP + e
            r = jnp.where(gsel == g, lt[o:o + 1, :], r)
        srow.append(r)
    v1, i1 = _first_argmax(srow)
    v2, i2 = _first_argmax([jnp.where(i1 == e, -jnp.inf, srow[e]) for e in range(EXPERTS_PER_GROUP)])
    e2 = jnp.exp(v2 - v1)
    den = 1.0 + e2
    wt_ref[0:1, :] = (1.0 / den) * pg
    wt_ref[1:2, :] = (e2 / den) * pg
    eid0 = gsel * EXPERTS_PER_GROUP + i1
    eid1 = gsel * EXPERTS_PER_GROUP + i2
    eid_ref[0, 0:1, :] = eid0
    eid_ref[0, 1:2, :] = eid1
    erow = lax.broadcasted_iota(I32, (N_EXPERTS, tm), 0)
    oh0 = (erow == eid0).astype(F32)
    oh1 = (erow == eid1).astype(F32)
    oh = oh0 + oh1
    incl = jnp.dot(oh.astype(BF16), u_ref[...], preferred_element_type=F32)
    before = carry_ref[:, 0:1] + incl - oh
    rank_ref[0, 0:1, :] = jnp.sum(oh0 * before, axis=0, keepdims=True).astype(I32)
    rank_ref[0, 1:2, :] = jnp.sum(oh1 * before, axis=0, keepdims=True).astype(I32)
    carry = carry_ref[...] + jnp.sum(oh, axis=1, keepdims=True)
    carry_ref[...] = carry
    cnt_ref[...] = carry


def _router(hp, wr, br, tm):
    d = wr.shape[1]
    per = d // 2 // LANES
    t = hp.shape[0] // per
    tri = (lax.broadcasted_iota(I32, (tm, tm), 0) <= lax.broadcasted_iota(I32, (tm, tm), 1)).astype(BF16)
    tile3 = pl.BlockSpec((1, 2, tm), lambda i: (i, 0, 0))
    return pl.pallas_call(
        functools.partial(_router_kernel, per=per),
        grid=(t // tm,),
        in_specs=[pl.BlockSpec((tm * per, LANES), lambda i: (i, 0)),
                  pl.BlockSpec((ROUTER_ROWS, d), lambda i: (0, 0)),
                  pl.BlockSpec((ROUTER_ROWS, 1), lambda i: (0, 0)),
                  pl.BlockSpec((tm, tm), lambda i: (0, 0))],
        out_specs=[tile3, pl.BlockSpec((2, tm), lambda i: (0, i)), tile3,
                   pl.BlockSpec((N_EXPERTS, LANES), lambda i: (0, 0))],
        out_shape=[jax.ShapeDtypeStruct((t // tm, 2, tm), I32), jax.ShapeDtypeStruct((2, t), F32),
                   jax.ShapeDtypeStruct((t // tm, 2, tm), I32), jax.ShapeDtypeStruct((N_EXPERTS, LANES), F32)],
        scratch_shapes=[pltpu.VMEM((N_EXPERTS, LANES), F32)],
        compiler_params=_cparams(("arbitrary",)),
        name="router",
    )(hp, wr, br, tri)


def _tile_copy(src, s_row, dst, d_row, per, sem):
    return pltpu.make_async_copy(src.at[pl.ds(pl.multiple_of(s_row, per), per), :],
                                 dst.at[pl.ds(pl.multiple_of(d_row, per), per), :], sem)


def _dispatch_kernel(lo_ref, hi_ref, dst_ref, h_ref, xb_hbm, stage, zero, sem, *, per):
    i = pl.program_id(0)
    n = pl.num_programs(0)
    slot = i % 2
    tm = dst_ref.shape[2] // 2

    def wait_tile(s):
        pltpu.make_async_copy(xb_hbm.at[pl.ds(0, 2 * tm * per), :], xb_hbm.at[pl.ds(0, 2 * tm * per), :], sem.at[s]).wait()

    @pl.when(i >= 2)
    def _():
        wait_tile(slot)

    stage[slot] = h_ref[...]

    def body(r, carry):
        for k in range(2):
            _tile_copy(stage.at[slot], r * per, xb_hbm, dst_ref[0, 0, k * tm + r], per, sem.at[slot]).start(priority=k)
        return carry

    lax.fori_loop(0, tm, body, 0, unroll=8)

    @pl.when(i == n - 1)
    def _():
        @pl.when(n >= 2)
        def _():
            wait_tile(1 - slot)
        wait_tile(slot)
        zero[...] = jnp.zeros_like(zero)

        def fill(e, carry):
            def one(s, c):
                _tile_copy(zero, 0, xb_hbm, s * per, per, sem.at[2]).start()
                return c
            lax.fori_loop(lo_ref[e], hi_ref[e], one, 0)

            def one_wait(s, c):
                _tile_copy(zero, 0, xb_hbm, s * per, per, sem.at[2]).wait()
                return c
            lax.fori_loop(lo_ref[e], hi_ref[e], one_wait, 0)
            return carry

        lax.fori_loop(0, N_EXPERTS + 1, fill, 0)


def _dispatch(fill_lo, fill_hi, drow3, hp, n_slots, per):
    n, tm = drow3.shape[0], drow3.shape[2] // 2
    smem3 = pl.BlockSpec((1, 1, 2 * tm), lambda i, *_: (i, 0, 0), memory_space=pltpu.SMEM)
    grid_spec = pltpu.PrefetchScalarGridSpec(
        num_scalar_prefetch=2,
        grid=(n,),
        in_specs=[smem3, pl.BlockSpec((tm * per, LANES), lambda i, *_: (i, 0))],
        out_specs=pl.BlockSpec(memory_space=pl.ANY),
        scratch_shapes=[pltpu.VMEM((2, tm * per, LANES), U32), pltpu.VMEM((per, LANES), U32),
                        pltpu.SemaphoreType.DMA((3,))])
    return pl.pallas_call(
        functools.partial(_dispatch_kernel, per=per),
        grid_spec=grid_spec,
        out_shape=jax.ShapeDtypeStruct((n_slots * per, LANES), U32),
        compiler_params=_cparams(("arbitrary",)),
        name="dispatch",
    )(fill_lo, fill_hi, drow3, hp)


def _expert_kernel(be_ref, first_ref, par_ref, nxt_ref, has_ref, x_ref, w1_hbm, w3_hbm, w2_hbm, y_ref,
                   wf1, wf3, wf2, w1b, w3b, w2b, sem, *, per):
    i = pl.program_id(0)

    def weight_copies(e, s):
        return [pltpu.make_async_copy(src.at[e], dst.at[s], sem.at[s])
                for src, dst in ((w1_hbm, wf1), (w3_hbm, wf3), (w2_hbm, wf2))]

    @pl.when(i == 0)
    def _():
        for cp in weight_copies(be_ref[0], 0):
            cp.start()

    @pl.when(first_ref[i] == 1)
    def _():
        s = par_ref[i]
        for cp in weight_copies(be_ref[i], s):
            cp.wait()

        @pl.when(has_ref[i] == 1)
        def _():
            for cp in weight_copies(nxt_ref[i], 1 - s):
                cp.start()

        w1b[...] = wf1[s].astype(BF16)
        w3b[...] = wf3[s].astype(BF16)
        w2b[...] = wf2[s].astype(BF16)

    x = _load_token_tiles(x_ref, MOE_BLOCK, per)
    a = jnp.dot(x, w1b[...], preferred_element_type=F32)
    b = jnp.dot(x, w3b[...], preferred_element_type=F32)
    mid = (jax.nn.silu(a) * b).astype(BF16)
    _store_token_tiles(y_ref, _pack_bf16_pairs(jnp.dot(mid, w2b[...], preferred_element_type=F32)))


def _experts(blk_e, xb, w1, w3, w2):
    d, ff = w1.shape[1], w1.shape[2]
    per = d // 2 // LANES
    nblk = xb.shape[0] // per // MOE_BLOCK
    rows = pl.BlockSpec((MOE_BLOCK * per, LANES), lambda i, *_: (i, 0))
    first = jnp.concatenate([jnp.ones((1,), I32), (blk_e[1:] != blk_e[:-1]).astype(I32)])
    parity = (jnp.cumsum(first) - 1) % 2
    nxt = jnp.min(jnp.where(blk_e[None, :] > blk_e[:, None], blk_e[None, :], N_EXPERTS), axis=1)
    has_next = (nxt < N_EXPERTS).astype(I32)
    nxt = jnp.minimum(nxt, N_EXPERTS - 1)
    grid_spec = pltpu.PrefetchScalarGridSpec(
        num_scalar_prefetch=5,
        grid=(nblk,),
        in_specs=[rows] + [pl.BlockSpec(memory_space=pl.ANY)] * 3,
        out_specs=rows,
        scratch_shapes=[pltpu.VMEM((2, d, ff), F32), pltpu.VMEM((2, d, ff), F32), pltpu.VMEM((2, ff, d), F32),
                        pltpu.VMEM((d, ff), BF16), pltpu.VMEM((d, ff), BF16), pltpu.VMEM((ff, d), BF16),
                        pltpu.SemaphoreType.DMA((2,))])
    return pl.pallas_call(
        functools.partial(_expert_kernel, per=per),
        grid_spec=grid_spec,
        out_shape=jax.ShapeDtypeStruct(xb.shape, U32),
        compiler_params=_cparams(("arbitrary",)),
        name="experts",
    )(blk_e, first, parity.astype(I32), nxt.astype(I32), has_next, xb, w1, w3, w2)


def _combine_kernel(src_ref, srcn_ref, y_hbm, wt_ref, x1_ref, mod_ref, g_ref, b_ref, o_ref, ybuf, sem, *, alpha, per):
    i = pl.program_id(0)
    n = pl.num_programs(0)
    slot = i % 2
    tm = x1_ref.shape[0]

    def wait(s):
        for k in range(2):
            pltpu.make_async_copy(y_hbm.at[pl.ds(0, tm * per), :], ybuf.at[s, k], sem.at[s]).wait()

    @pl.when(i == 0)
    def _():
        def body(r, carry):
            for k in range(2):
                _tile_copy(y_hbm, src_ref[0, 0, k * tm + r], ybuf.at[0, k], r * per, per, sem.at[0]).start(priority=k)
            return carry
        lax.fori_loop(0, tm, body, 0, unroll=8)

    for r in range(tm):
        for k in range(2):
            _tile_copy(y_hbm, srcn_ref[0, 0, k * tm + r], ybuf.at[1 - slot, k], r * per, per, sem.at[1 - slot]).start(priority=k)

    wait(slot)
    wt = wt_ref[...]
    y0 = _load_token_tiles(ybuf.at[slot, 0], tm, per, F32)
    y1 = _load_token_tiles(ybuf.at[slot, 1], tm, per, F32)
    ffn = wt[:, 0:1] * y0 + wt[:, 1:2] * y1
    y = alpha * x1_ref[...] + (1.0 + mod_ref[0, 5:6, :]) * ffn
    o_ref[...] = _ln(y, g_ref[...], b_ref[...])

    @pl.when(i == n - 1)
    def _():
        wait(1 - slot)


def _combine(drow3, yb, wt_t, x1, mod3, g2, b2, seq, alpha):
    t, d = x1.shape
    per = d // 2 // LANES
    n, tm = drow3.shape[0], drow3.shape[2] // 2
    per_b = seq // tm
    cur = pl.BlockSpec((1, 1, 2 * tm), lambda i: (i, 0, 0), memory_space=pltpu.SMEM)
    nxt = pl.BlockSpec((1, 1, 2 * tm), lambda i: (jnp.minimum(i + 1, n - 1), 0, 0), memory_space=pltpu.SMEM)
    vec = pl.BlockSpec((1, d), lambda i: (0, 0))
    return pl.pallas_call(
        functools.partial(_combine_kernel, alpha=alpha, per=per),
        grid=(n,),
        in_specs=[cur, nxt,
                  pl.BlockSpec(memory_space=pl.ANY),
                  pl.BlockSpec((tm, 2), lambda i: (i, 0)),
                  pl.BlockSpec((tm, d), lambda i: (i, 0)),
                  pl.BlockSpec((1, 6, d), lambda i: (i // per_b, 0, 0)),
                  vec, vec],
        out_specs=pl.BlockSpec((tm, d), lambda i: (i, 0)),
        out_shape=jax.ShapeDtypeStruct((t, d), F32),
        scratch_shapes=[pltpu.VMEM((2, 2, tm * per, LANES), U32), pltpu.SemaphoreType.DMA((2,))],
        compiler_params=_cparams(("arbitrary",)),
        name="combine",
    )(drow3, drow3, yb, wt_t, x1, mod3, g2, b2)


def _pick_tile(n, want):
    tm = min(n, want)
    assert n % tm == 0
    return tm


def kernel(x, c, positions, ln0_g, ln0_b, w_ada, b_ada, w_in, w_out, ret_log_decay_f, ret_log_decay_b,
           ln1_g, ln1_b, w_group, b_group, w_sub, b_sub, w1, w3, w2, ln2_g, ln2_b):
    batch, seq, d = x.shape
    depth = w_ada.shape[0]
    t = batch * seq
    n_ah = d // 2 // ATTN_HEAD_DIM
    n_rh = d // 2 // RET_HEAD_DIM
    assert depth == 1 and d % (2 * RET_HEAD_DIM) == 0 and batch <= 8
    assert seq % (QT * CHAINS) == 0 and seq % (max(DILATIONS) * PERM_ROWS) == 0 and seq // max(DILATIONS) >= KW
    alpha = (2 * depth) ** 0.25

    inv_rope = ROPE_THETA ** (-jnp.arange(0, ROPE_DIM, 2, dtype=F32) / ROPE_DIM)
    inv_ret = RET_THETA ** (-jnp.linspace(0.0, 1.0, RET_HEAD_DIM // 2, dtype=F32))
    fac = jnp.tile(inv_rope, TOK_PER_ROW).reshape(1, LANES)
    fr = inv_ret.reshape(1, LANES)
    posb = jnp.broadcast_to(positions.astype(F32).reshape(t, 1), (t, LANES))
    posc = jnp.repeat(positions.astype(F32).reshape(t // TOK_PER_ROW, TOK_PER_ROW), ROPE_DIM // 2, axis=1)
    c8 = jnp.zeros((8, d), F32).at[:batch].set(c)
    row = lambda v: v.reshape(1, d)

    xs = x.reshape(t, d)
    mod = _ada(c8, w_ada[0], b_ada[0].reshape(1, -1))
    mod3 = mod[:batch].reshape(batch, 6, d)
    xn, h, tabs = _prep(xs, posb, posc, mod3, row(ln0_g), row(ln0_b), fac, fr, seq, _pick_tile(seq, 512))
    w_in_b = w_in[0].astype(BF16)
    tmp = _pick_tile(seq, 2048)
    z_aqk = _proj(_proj_attn_kernel, lambda j: j, 2, h, w_in_b, tabs, tmp, "proj_attn")
    z_rqk = _proj(_proj_ret_kernel, lambda j: 3 + j, 2, h, w_in_b, tabs, _pick_tile(seq, 1024), "proj_ret")
    z_pl = _proj(_proj_plain_kernel, lambda j: 2 + 3 * jnp.minimum(j, 1) + jnp.maximum(j - 1, 0), 3,
                 h, w_in_b, None, tmp, "proj_plain")
    attn = _attn(z_aqk, z_pl, batch, seq, n_ah)
    lg = jnp.stack([ret_log_decay_f[0], ret_log_decay_b[0]]).astype(F32)
    r = _ret(z_rqk, z_pl, lg, batch, seq, n_rh, _pick_tile(seq, 2048))
    x1, hp = _outproj(attn, r, xn, mod3, row(ln1_g[0]), row(ln1_b[0]), w_out[0].astype(BF16),
                      seq, _pick_tile(seq, 512), alpha)
    wr = jnp.zeros((ROUTER_ROWS, d), F32)
    wr = wr.at[:N_GROUPS].set(w_group[0].T)
    wr = wr.at[N_GROUPS:N_GROUPS + N_EXPERTS].set(w_sub[0].transpose(0, 2, 1).reshape(N_EXPERTS, d))
    br = jnp.zeros((ROUTER_ROWS, 1), F32)
    br = br.at[:N_GROUPS, 0].set(b_group[0]).at[N_GROUPS:N_GROUPS + N_EXPERTS, 0].set(b_sub[0].reshape(-1))
    eid3, wt, rank3, cnt = _router(hp, wr.astype(BF16), br, _pick_tile(seq, 512))
    counts = cnt[:, 0].astype(I32)
    padded = (counts + MOE_BLOCK - 1) // MOE_BLOCK * MOE_BLOCK
    pend = jnp.cumsum(padded)
    pstart = pend - padded
    n_slots = 2 * t + N_EXPERTS * MOE_BLOCK
    nblk = n_slots // MOE_BLOCK
    starts = jnp.arange(nblk, dtype=I32) * MOE_BLOCK
    blk_e = jnp.minimum(jnp.sum((pend[None, :] <= starts[:, None]).astype(I32), axis=1), N_EXPERTS - 1)
    fill_lo = jnp.concatenate([pstart + counts, pend[-1:]])
    fill_hi = jnp.concatenate([pend, jnp.full((1,), n_slots, I32)])
    per = d // 2 // LANES
    onehot = eid3[..., None] == jnp.arange(N_EXPERTS, dtype=I32)
    drow3 = ((rank3 + jnp.sum(jnp.where(onehot, pstart, 0), axis=-1)) * per).reshape(eid3.shape[0], 1, -1)
    xb = _dispatch(fill_lo, fill_hi, drow3, hp, n_slots, per)
    yb = _experts(blk_e, xb, w1[0], w3[0], w2[0])
    out = _combine(drow3, yb, wt.T, x1, mod3, row(ln2_g[0]), row(ln2_b[0]), seq, alpha)
    return out.reshape(batch, seq, d)
```

```python
import functools

import jax
import jax.numpy as jnp
from jax import lax
from jax.experimental import pallas as pl
from jax.experimental.pallas import tpu as pltpu

F32 = jnp.float32
BF16 = jnp.bfloat16
I32 = jnp.int32
U32 = jnp.uint32

LANES = 128
ATTN_HEAD_DIM = 128
RET_HEAD_DIM = 256
DILATIONS = (1, 4, 16)
BAND = 64
QT = 128
KW = QT + 2 * BAND
CHAINS = 64
PERM_ROWS = 256
PERM_UNROLL = 16
ROPE_THETA = 500000.0
ROPE_DIM = ATTN_HEAD_DIM // 4
TOK_PER_ROW = LANES // (ROPE_DIM // 2)
RET_THETA = 10000.0
RET_CHUNK = 128
RET_UNROLL = 8
N_GROUPS = 4
EXPERTS_PER_GROUP = 8
N_EXPERTS = N_GROUPS * EXPERTS_PER_GROUP
MOE_BLOCK = 256
LN_EPS = 1e-5
MASK_VALUE = -1e30
ROUTER_ROWS = 48
HI_MASK = 0xFFFF0000
VMEM_LIMIT = 56 * 1024 * 1024


def _cparams(sem, vmem=VMEM_LIMIT):
    return pltpu.CompilerParams(dimension_semantics=sem, vmem_limit_bytes=vmem)


def _ln(x, g, b):
    mu = jnp.mean(x, axis=-1, keepdims=True)
    xc = x - mu
    var = jnp.mean(xc * xc, axis=-1, keepdims=True)
    return xc * lax.rsqrt(var + LN_EPS) * g + b


def _pack_bf16_pairs(h):
    bits = lax.bitcast_convert_type(h.astype(BF16).astype(F32), U32)
    n = h.shape[1] // 2
    return (bits[:, :n] >> 16) | (bits[:, n:] & jnp.uint32(HI_MASK))


def _store_token_tiles(ref, packed):
    m, n = packed.shape
    per = n // LANES
    for s in range(per):
        ref[pl.ds(s, m, stride=per), :] = packed[:, s * LANES:(s + 1) * LANES]


def _load_token_tiles(ref, m, per, dtype=BF16):
    slabs = [ref[pl.ds(s, m, stride=per), :] for s in range(per)]
    lo = [lax.bitcast_convert_type(p << 16, F32) for p in slabs]
    hi = [lax.bitcast_convert_type(p & jnp.uint32(HI_MASK), F32) for p in slabs]
    return jnp.concatenate(lo + hi, axis=-1).astype(dtype)


def _ada_kernel(c_ref, w_ref, b_ref, o_ref):
    cs = jax.nn.silu(c_ref[...])
    o_ref[...] = jnp.dot(cs.astype(BF16), w_ref[...].astype(BF16), preferred_element_type=F32) + b_ref[...]


def _ada(c8, w_ada, b_ada):
    d, n = w_ada.shape
    tn = min(n, 512)
    return pl.pallas_call(
        _ada_kernel,
        grid=(n // tn,),
        in_specs=[pl.BlockSpec((8, d), lambda j: (0, 0)),
                  pl.BlockSpec((d, tn), lambda j: (0, j)),
                  pl.BlockSpec((1, tn), lambda j: (0, j))],
        out_specs=pl.BlockSpec((8, tn), lambda j: (0, j)),
        out_shape=jax.ShapeDtypeStruct((8, n), F32),
        compiler_params=_cparams(("arbitrary",)),
        name="ada",
    )(c8, w_ada, b_ada)


def _prep_kernel(x_ref, pos_ref, posc_ref, mod_ref, g_ref, b_ref, fac_ref, fr_ref, h_ref, tab_ref):
    xn = _ln(x_ref[...], g_ref[...], b_ref[...])
    h_ref[...] = (xn * (1.0 + mod_ref[0, 1:2, :]) + mod_ref[0, 0:1, :]).astype(BF16)
    half = ROPE_DIM // 2
    angc = posc_ref[...] * fac_ref[...]
    cc, sc = jnp.cos(angc), jnp.sin(angc)
    rows_c = angc.shape[0]
    lane = lax.broadcasted_iota(I32, angc.shape, 1)
    first, second = lane < half, (lane >= half) & (lane < 2 * half)
    for ts in range(TOK_PER_ROW):
        shift = (LANES - ts * half) % LANES
        c = pltpu.roll(cc, shift, 1) if shift else cc
        s = pltpu.roll(sc, shift, 1) if shift else sc
        rows = pl.ds(ts, rows_c, stride=TOK_PER_ROW)
        tab_ref.at[0][rows, :] = jnp.where(first, c, jnp.where(second, pltpu.roll(c, half, 1), 1.0))
        tab_ref.at[1][rows, :] = jnp.where(first, -s, 0.0)
        tab_ref.at[2][rows, :] = jnp.where(second, pltpu.roll(s, half, 1), 0.0)
    angr = pos_ref[...] * fr_ref[...]
    tab_ref[3] = jnp.cos(angr)
    tab_ref[4] = jnp.sin(angr)


def _prep(x2, posb, posc, mod3, ln0_g, ln0_b, fac, fr, seq, tm):
    t, d = x2.shape
    per_b = seq // tm
    vec = lambda n: pl.BlockSpec((1, n), lambda i: (0, 0))
    return pl.pallas_call(
        _prep_kernel,
        grid=(t // tm,),
        in_specs=[pl.BlockSpec((tm, d), lambda i: (i, 0)),
                  pl.BlockSpec((tm, LANES), lambda i: (i, 0)),
                  pl.BlockSpec((tm // TOK_PER_ROW, LANES), lambda i: (i, 0)),
                  pl.BlockSpec((1, 6, d), lambda i: (i // per_b, 0, 0)),
                  vec(d), vec(d), vec(LANES), vec(LANES)],
        out_specs=[pl.BlockSpec((tm, d), lambda i: (i, 0)), pl.BlockSpec((5, tm, LANES), lambda i: (0, i, 0))],
        out_shape=[jax.ShapeDtypeStruct((t, d), BF16), jax.ShapeDtypeStruct((5, t, LANES), F32)],
        compiler_params=_cparams(("arbitrary",)),
        name="prep",
    )(x2, posb, posc, mod3, ln0_g, ln0_b, fac, fr)


def _proj_chunks(h_ref, w_ref, n_sub):
    for hc in range(n_sub):
        yield hc, jnp.dot(h_ref[...], w_ref[:, hc * 2 * LANES:(hc + 1) * 2 * LANES], preferred_element_type=F32)


def _proj_attn_kernel(h_ref, w_ref, tab_ref, z_ref, *, n_sub):
    scale = jnp.where(pl.program_id(0) == 0, ATTN_HEAD_DIM ** -0.5, 1.0).astype(F32)
    ca, s1, s2 = tab_ref[0] * scale, tab_ref[1] * scale, tab_ref[2] * scale
    half = ROPE_DIM // 2
    for hc, res in _proj_chunks(h_ref, w_ref, n_sub):
        for lanes in (slice(0, LANES), slice(LANES, 2 * LANES)):
            x = res[:, lanes]
            r = x * ca + pltpu.roll(x, LANES - half, 1) * s1 + pltpu.roll(x, half, 1) * s2
            z_ref[hc, :, lanes] = r.astype(BF16)


def _proj_ret_kernel(h_ref, w_ref, tab_ref, z_ref, *, n_sub):
    scale = jnp.where(pl.program_id(0) == 1, RET_HEAD_DIM ** -0.5, 1.0).astype(F32)
    cr, sr = tab_ref[3] * scale, tab_ref[4] * scale
    for hc, res in _proj_chunks(h_ref, w_ref, n_sub):
        x1, x2 = res[:, :LANES], res[:, LANES:]
        z_ref[hc, :, :LANES] = (x1 * cr - x2 * sr).astype(BF16)
        z_ref[hc, :, LANES:] = (x2 * cr + x1 * sr).astype(BF16)


def _proj_plain_kernel(h_ref, w_ref, z_ref, *, n_sub):
    for hc, res in _proj_chunks(h_ref, w_ref, n_sub):
        z_ref[hc] = res.astype(BF16)


def _proj(body, col_of, n_seg, h, w_in_b, tabs, tm, name):
    t, d = h.shape
    tn = d // 2
    n_sub = tn // (2 * LANES)
    in_specs = [pl.BlockSpec((tm, d), lambda j, i: (i, 0)),
                pl.BlockSpec((d, tn), lambda j, i: (0, col_of(j)))]
    args = [h, w_in_b]
    if tabs is not None:
        in_specs.append(pl.BlockSpec((5, tm, LANES), lambda j, i: (0, i, 0)))
        args.append(tabs)
    return pl.pallas_call(
        functools.partial(body, n_sub=n_sub),
        grid=(n_seg, t // tm),
        in_specs=in_specs,
        out_specs=pl.BlockSpec((n_sub, tm, 2 * LANES), lambda j, i: (j, i, 0)),
        out_shape=jax.ShapeDtypeStruct((n_seg * n_sub, t, 2 * LANES), BF16),
        compiler_params=_cparams(("arbitrary", "arbitrary")),
        name=name,
    )(*args)


def _attn_tile(q, k, v, bias):
    s = lax.dot_general(q, k, (((1,), (1,)), ((), ())), preferred_element_type=F32) + bias
    if s.shape[1] == 2 * LANES:
        m = jnp.max(jnp.maximum(s[:, :LANES], s[:, LANES:]), axis=-1, keepdims=True)
        p = jnp.exp(s - m)
        den = jnp.sum(p[:, :LANES] + p[:, LANES:], axis=-1, keepdims=True)
    else:
        m = jnp.max(s, axis=-1, keepdims=True)
        p = jnp.exp(s - m)
        den = jnp.sum(p, axis=-1, keepdims=True)
    acc = jnp.dot(p.astype(BF16), v, preferred_element_type=F32)
    return acc, jnp.broadcast_to(m, acc.shape), jnp.broadcast_to(den, acc.shape)


def _attn_kernel(q_ref, k_ref, v_ref, bias_ref, p4_ref, p16_ref, o_ref,
                 acc_ref, m_ref, d_ref, qp_ref, kp_ref, vp_ref, *, seq):
    n_items = seq // QT

    def window(n, length):
        qs = n * QT
        ks = jnp.clip(qs - BAND, 0, length - KW)
        return qs, ks, bias_ref[lax.div(qs - ks, BAND)]

    def perm_body(it, carry):
        for u in range(PERM_UNROLL):
            blk = it * PERM_UNROLL + u
            r0 = pl.multiple_of(blk * PERM_ROWS, PERM_ROWS)
            qk = jnp.concatenate([q_ref[0, 0, pl.ds(r0, PERM_ROWS), :], k_ref[0, 0, pl.ds(r0, PERM_ROWS), :]], axis=-1)
            v = v_ref[0, 0, pl.ds(r0, PERM_ROWS), :]
            for slot, (dil, p_ref) in enumerate(((4, p4_ref), (16, p16_ref))):
                length, w = seq // dil, PERM_ROWS // dil
                qkp = jnp.dot(p_ref[...], qk, preferred_element_type=F32).astype(BF16)
                vp = jnp.dot(p_ref[...], v, preferred_element_type=F32).astype(BF16)
                for r in range(dil):
                    rows = pl.ds(pl.multiple_of(r * length + blk * w, w), w)
                    qp_ref[slot, rows, :] = qkp[r * w:(r + 1) * w, :ATTN_HEAD_DIM]
                    kp_ref[slot, rows, :] = qkp[r * w:(r + 1) * w, ATTN_HEAD_DIM:]
                    vp_ref[slot, rows, :] = vp[r * w:(r + 1) * w, :]
        return carry

    lax.fori_loop(0, seq // PERM_ROWS // PERM_UNROLL, perm_body, 0)

    def dilated_tile(item, dil, slot):
        length = seq // dil
        n, r = lax.div(item, dil), lax.rem(item, dil)
        qs, ks, bias = window(n, length)
        qrow = pl.multiple_of(r * length + qs, QT)
        krow = pl.multiple_of(r * length + ks, BAND)
        out = _attn_tile(qp_ref[slot, pl.ds(qrow, QT), :], kp_ref[slot, pl.ds(krow, KW), :],
                         vp_ref[slot, pl.ds(krow, KW), :], bias)
        return out, pl.ds(qs * dil + r, QT, stride=dil)

    def body16(it, carry):
        for g in range(CHAINS):
            (acc, m, den), rows = dilated_tile(it * CHAINS + g, 16, 1)
            acc_ref[rows, :] = acc
            m_ref[rows, :] = m
            d_ref[rows, :] = den
        return carry

    lax.fori_loop(0, n_items // CHAINS, body16, 0)

    def body4(it, carry):
        for g in range(CHAINS):
            (acc, m, den), rows = dilated_tile(it * CHAINS + g, 4, 0)
            m0 = m_ref[rows, :]
            mn = jnp.maximum(m0, m)
            a, b = jnp.exp(m0 - mn), jnp.exp(m - mn)
            acc_ref[rows, :] = a * acc_ref[rows, :] + b * acc
            d_ref[rows, :] = a * d_ref[rows, :] + b * den
            m_ref[rows, :] = mn
        return carry

    lax.fori_loop(0, n_items // CHAINS, body4, 0)

    def body1(it, carry):
        for g in range(CHAINS):
            qs, ks, bias = window(it * CHAINS + g, seq)
            qs, ks = pl.multiple_of(qs, QT), pl.multiple_of(ks, BAND)
            acc, m, den = _attn_tile(q_ref[0, 0, pl.ds(qs, QT), :], k_ref[0, 0, pl.ds(ks, KW), :],
                                     v_ref[0, 0, pl.ds(ks, KW), :], bias)
            rows = pl.ds(qs, QT)
            m0 = m_ref[rows, :]
            mn = jnp.maximum(m0, m)
            a, b = jnp.exp(m0 - mn), jnp.exp(m - mn)
            num = a * acc_ref[rows, :] + b * acc
            o_ref[0, 0, rows, :] = (num / (a * d_ref[rows, :] + b * den)).astype(BF16)
        return carry

    lax.fori_loop(0, n_items // CHAINS, body1, 0)


def _perm_matrix(dil):
    w = PERM_ROWS // dil
    i = jnp.arange(PERM_ROWS)
    src = (i % w) * dil + i // w
    return (src[:, None] == jnp.arange(PERM_ROWS)[None, :]).astype(BF16)


def _attn(zqk, zv_, batch, seq, n_ah):
    n_sub = n_ah // 2
    zqk = zqk.reshape(zqk.shape[0], batch, seq, 2 * LANES)
    zv_ = zv_.reshape(zv_.shape[0], batch, seq, 2 * LANES)
    blk = (1, 1, seq, ATTN_HEAD_DIM)
    i = jnp.arange(QT)[:, None]
    jj = jnp.arange(KW)[None, :]
    bias = jnp.stack([jnp.where(jnp.abs(i + o * BAND - jj) <= BAND, 0.0, MASK_VALUE) for o in range(3)]).astype(F32)
    const = lambda shape: pl.BlockSpec(shape, lambda b, h: (0,) * len(shape))
    o = pl.pallas_call(
        functools.partial(_attn_kernel, seq=seq),
        grid=(batch, n_ah),
        in_specs=[pl.BlockSpec(blk, lambda b, h: (h // 2, b, 0, h % 2)),
                  pl.BlockSpec(blk, lambda b, h: (n_sub + h // 2, b, 0, h % 2)),
                  pl.BlockSpec(blk, lambda b, h: (h // 2, b, 0, h % 2)),
                  const((3, QT, KW)), const((PERM_ROWS, PERM_ROWS)), const((PERM_ROWS, PERM_ROWS))],
        out_specs=pl.BlockSpec(blk, lambda b, h: (h, b, 0, 0)),
        out_shape=jax.ShapeDtypeStruct((n_ah, batch, seq, ATTN_HEAD_DIM), BF16),
        scratch_shapes=[pltpu.VMEM((seq, LANES), F32)] * 3
        + [pltpu.VMEM((2, seq, ATTN_HEAD_DIM), BF16)] * 3,
        compiler_params=_cparams(("arbitrary", "arbitrary")),
        name="attn",
    )(zqk, zqk, zv_, bias, _perm_matrix(4), _perm_matrix(16))
    return o.reshape(n_ah, batch * seq, ATTN_HEAD_DIM)


def _ret_kernel(lg_ref, q_ref, k_ref, v_ref, g_ref, o_ref, y_ref, sf_ref, sb_ref, dmat_ref, *, n_tiles, tile, heads):
    hp = pl.program_id(1)
    t = pl.program_id(2)
    c = RET_CHUNK
    lgf = [lg_ref[0, hp * heads + hd] for hd in range(heads)]
    lgb = [lg_ref[1, hp * heads + hd] for hd in range(heads)]
    col = lax.broadcasted_iota(I32, (c, 1), 0).astype(F32)
    row = lax.broadcasted_iota(I32, (1, c), 1).astype(F32)
    n_iter = tile // c // RET_UNROLL

    @pl.when(t == 0)
    def _():
        sf_ref[...] = jnp.zeros_like(sf_ref)
        sb_ref[...] = jnp.zeros_like(sb_ref)
        diff = (lax.broadcasted_iota(I32, (c, c), 0) - lax.broadcasted_iota(I32, (c, c), 1)).astype(F32)
        for hd in range(heads):
            dmat_ref[hd] = jnp.where(diff >= 0, jnp.exp(lgf[hd] * jnp.maximum(diff, 0.0)), 0.0) \
                + jnp.where(diff < 0, jnp.exp(lgb[hd] * jnp.maximum(-diff, 0.0)), 0.0)

    def chunk(hd, r0):
        rows = pl.ds(pl.multiple_of(r0, c), c)
        return rows, q_ref[hd, rows, :], k_ref[hd, rows, :], v_ref[hd, rows, :]

    @pl.when(t < n_tiles)
    def _():
        xi = [jnp.exp(lg * (col + 1.0)) for lg in lgf]
        zeta = [jnp.exp(lg * (c - 1.0 - row)) for lg in lgf]
        cdec = [jnp.exp(lg * jnp.full((1, 1), float(c), F32)) for lg in lgf]
        base = t * tile

        def body(it, carry):
            state = [sf_ref[hd] for hd in range(heads)]
            for u in range(RET_UNROLL):
                r0 = (it * RET_UNROLL + u) * c
                for hd in range(heads):
                    _, q, k, v = chunk(hd, r0)
                    s = lax.dot_general(q, k, (((1,), (1,)), ((), ())), preferred_element_type=F32) * dmat_ref[hd]
                    y = jnp.dot(s.astype(BF16), v, preferred_element_type=F32)
                    y = y + jnp.dot(q, state[hd].astype(BF16), preferred_element_type=F32) * xi[hd]
                    kt = (k.astype(F32).T * zeta[hd]).astype(BF16)
                    state[hd] = state[hd] * cdec[hd] + jnp.dot(kt, v, preferred_element_type=F32)
                    y_ref[hd, pl.ds(pl.multiple_of(base + r0, c), c), :] = y
            for hd in range(heads):
                sf_ref[hd] = state[hd]
            return carry

        lax.fori_loop(0, n_iter, body, 0)

    @pl.when(t >= n_tiles)
    def _():
        xi = [jnp.exp(lg * (c - col)) for lg in lgb]
        zeta = [jnp.exp(lg * row) for lg in lgb]
        cdec = [jnp.exp(lg * jnp.full((1, 1), float(c), F32)) for lg in lgb]
        base = (2 * n_tiles - 1 - t) * tile

        def body(it, carry):
            state = [sb_ref[hd] for hd in range(heads)]
            for u in range(RET_UNROLL):
                r0 = (tile // c - 1 - (it * RET_UNROLL + u)) * c
                for hd in range(heads):
                    rows, q, k, v = chunk(hd, r0)
                    y = y_ref[hd, pl.ds(pl.multiple_of(base + r0, c), c), :]
                    y = y + jnp.dot(q, state[hd].astype(BF16), preferred_element_type=F32) * xi[hd]
                    kt = (k.astype(F32).T * zeta[hd]).astype(BF16)
                    state[hd] = state[hd] * cdec[hd] + jnp.dot(kt, v, preferred_element_type=F32)
                    mu = jnp.mean(y, axis=-1, keepdims=True)
                    yc = y - mu
                    var = jnp.mean(yc * yc, axis=-1, keepdims=True)
                    yn = yc * lax.rsqrt(var + LN_EPS)
                    gate = jax.nn.silu(g_ref[hd, rows, :].astype(F32))
                    o_ref[hd, rows, :] = (gate * yn).astype(BF16)
            for hd in range(heads):
                sb_ref[hd] = state[hd]
            return carry

        lax.fori_loop(0, n_iter, body, 0)


def _ret(zqk, zvg, lg, batch, seq, n_rh, tile):
    n_tiles = seq // tile
    per_b = seq // tile
    heads = 2 if n_rh % 2 == 0 else 1
    groups = n_rh // heads

    def rows(t):
        return jnp.where(t < n_tiles, t, 2 * n_tiles - 1 - t)

    def spec(seg):
        return pl.BlockSpec((heads, tile, RET_HEAD_DIM), lambda b, h, t: (seg * groups + h, b * per_b + rows(t), 0))

    return pl.pallas_call(
        functools.partial(_ret_kernel, n_tiles=n_tiles, tile=tile, heads=heads),
        grid=(batch, groups, 2 * n_tiles),
        in_specs=[pl.BlockSpec(memory_space=pltpu.SMEM), spec(0), spec(1), spec(1), spec(2)],
        out_specs=pl.BlockSpec((heads, tile, RET_HEAD_DIM),
                               lambda b, h, t: (h, b * per_b + jnp.where(t < n_tiles, n_tiles - 1, 2 * n_tiles - 1 - t), 0)),
        out_shape=jax.ShapeDtypeStruct((n_rh, batch * seq, RET_HEAD_DIM), BF16),
        scratch_shapes=[pltpu.VMEM((heads, seq, RET_HEAD_DIM), F32),
                        pltpu.VMEM((heads, RET_HEAD_DIM, RET_HEAD_DIM), F32),
                        pltpu.VMEM((heads, RET_HEAD_DIM, RET_HEAD_DIM), F32),
                        pltpu.VMEM((heads, RET_CHUNK, RET_CHUNK), F32)],
        compiler_params=_cparams(("arbitrary", "arbitrary", "arbitrary")),
        name="ret",
    )(lg, zqk, zqk, zvg, zvg)


def _outproj_kernel(a_ref, r_ref, x_ref, mod_ref, g0_ref, b0_ref, g1_ref, b1_ref, w_ref, x1_ref, hp_ref,
                    *, n_ah, n_rh, alpha):
    mix = jnp.concatenate([a_ref[h] for h in range(n_ah)] + [r_ref[h] for h in range(n_rh)], axis=-1)
    acc = jnp.dot(mix, w_ref[...], preferred_element_type=F32)
    xn = _ln(x_ref[...], g0_ref[...], b0_ref[...])
    y = alpha * xn + (1.0 + mod_ref[0, 2:3, :]) * acc
    x1 = _ln(y, g1_ref[...], b1_ref[...])
    x1_ref[...] = x1
    _store_token_tiles(hp_ref, _pack_bf16_pairs(x1 * (1.0 + mod_ref[0, 4:5, :]) + mod_ref[0, 3:4, :]))


def _outproj(attn, r, x2, mod3, g0, b0, g1, b1, w_out_b, seq, tm, alpha):
    t, d = x2.shape
    n_ah, n_rh = attn.shape[0], r.shape[0]
    per_b = seq // tm
    per = d // 2 // LANES
    row = lambda i: (i, 0)
    vec = pl.BlockSpec((1, d), lambda i: (0, 0))
    return pl.pallas_call(
        functools.partial(_outproj_kernel, n_ah=n_ah, n_rh=n_rh, alpha=alpha),
        grid=(t // tm,),
        in_specs=[pl.BlockSpec((n_ah, tm, ATTN_HEAD_DIM), lambda i: (0, i, 0)),
                  pl.BlockSpec((n_rh, tm, RET_HEAD_DIM), lambda i: (0, i, 0)),
                  pl.BlockSpec((tm, d), row),
                  pl.BlockSpec((1, 6, d), lambda i: (i // per_b, 0, 0)),
                  vec, vec, vec, vec,
                  pl.BlockSpec((d, d), lambda i: (0, 0))],
        out_specs=[pl.BlockSpec((tm, d), row), pl.BlockSpec((tm * per, LANES), row)],
        out_shape=[jax.ShapeDtypeStruct((t, d), F32), jax.ShapeDtypeStruct((t * per, LANES), U32)],
        compiler_params=_cparams(("arbitrary",)),
        name="outproj",
    )(attn, r, x2, mod3, g0, b0, g1, b1, w_out_b)


def _first_argmax(rows):
    best, idx = rows[0], jnp.zeros(rows[0].shape, I32)
    for e in range(1, len(rows)):
        better = rows[e] > best
        idx = jnp.where(better, e, idx)
        best = jnp.maximum(best, rows[e])
    return best, idx


def _router_kernel(h_ref, w_ref, b_ref, u_ref, eid_ref, wt_ref, rank_ref, cnt_ref, carry_ref, *, per):
    i = pl.program_id(0)

    @pl.when(i == 0)
    def _():
        carry_ref[...] = jnp.zeros_like(carry_ref)

    tm = eid_ref.shape[2]
    lt = lax.dot_general(w_ref[...], _load_token_tiles(h_ref, tm, per), (((1,), (1,)), ((), ())),
                         preferred_element_type=F32) + b_ref[...]
    grow = [lt[g:g + 1, :] for g in range(N_GROUPS)]
    gmax, gsel = _first_argmax(grow)
    gsum = grow[0] * 0.0
    for g in range(N_GROUPS):
        gsum = gsum + jnp.exp(grow[g] - gmax)
    pg = 1.0 / gsum
    srow = []
    for e in range(EXPERTS_PER_GROUP):
        r = lt[N_GROUPS + e:N_GROUPS + e + 1, :]
        for g in range(1, N_GROUPS):
            o = N_GROUPS + g * EXPERTS_PER_GROUP + e
            r = jnp.where(gsel == g, lt[o:o + 1, :], r)
        srow.append(r)
    v1, i1 = _first_argmax(srow)
    v2, i2 = _first_argmax([jnp.where(i1 == e, -jnp.inf, srow[e]) for e in range(EXPERTS_PER_GROUP)])
    e2 = jnp.exp(v2 - v1)
    den = 1.0 + e2
    wt_ref[0:1, :] = (1.0 / den) * pg
    wt_ref[1:2, :] = (e2 / den) * pg
    eid0 = gsel * EXPERTS_PER_GROUP + i1
    eid1 = gsel * EXPERTS_PER_GROUP + i2
    eid_ref[0, 0:1, :] = eid0
    eid_ref[0, 1:2, :] = eid1
    erow = lax.broadcasted_iota(I32, (N_EXPERTS, tm), 0)
    oh0 = (erow == eid0).astype(F32)
    oh1 = (erow == eid1).astype(F32)
    oh = oh0 + oh1
    incl = jnp.dot(oh.astype(BF16), u_ref[...], preferred_element_type=F32)
    before = carry_ref[:, 0:1] + incl - oh
    rank_ref[0, 0:1, :] = jnp.sum(oh0 * before, axis=0, keepdims=True).astype(I32)
    rank_ref[0, 1:2, :] = jnp.sum(oh1 * before, axis=0, keepdims=True).astype(I32)
    carry = carry_ref[...] + jnp.sum(oh, axis=1, keepdims=True)
    carry_ref[...] = carry
    cnt_ref[...] = carry


def _router(hp, wr, br, tm):
    d = wr.shape[1]
    per = d // 2 // LANES
    t = hp.shape[0] // per
    tri = (lax.broadcasted_iota(I32, (tm, tm), 0) <= lax.broadcasted_iota(I32, (tm, tm), 1)).astype(BF16)
    tile3 = pl.BlockSpec((1, 2, tm), lambda i: (i, 0, 0))
    return pl.pallas_call(
        functools.partial(_router_kernel, per=per),
        grid=(t // tm,),
        in_specs=[pl.BlockSpec((tm * per, LANES), lambda i: (i, 0)),
                  pl.BlockSpec((ROUTER_ROWS, d), lambda i: (0, 0)),
                  pl.BlockSpec((ROUTER_ROWS, 1), lambda i: (0, 0)),
                  pl.BlockSpec((tm, tm), lambda i: (0, 0))],
        out_specs=[tile3, pl.BlockSpec((2, tm), lambda i: (0, i)), tile3,
                   pl.BlockSpec((N_EXPERTS, LANES), lambda i: (0, 0))],
        out_shape=[jax.ShapeDtypeStruct((t // tm, 2, tm), I32), jax.ShapeDtypeStruct((2, t), F32),
                   jax.ShapeDtypeStruct((t // tm, 2, tm), I32), jax.ShapeDtypeStruct((N_EXPERTS, LANES), F32)],
        scratch_shapes=[pltpu.VMEM((N_EXPERTS, LANES), F32)],
        compiler_params=_cparams(("arbitrary",)),
        name="router",
    )(hp, wr, br, tri)


def _tile_copy(src, s_row, dst, d_row, per, sem):
    return pltpu.make_async_copy(src.at[pl.ds(pl.multiple_of(s_row, per), per), :],
                                 dst.at[pl.ds(pl.multiple_of(d_row, per), per), :], sem)


def _dispatch_kernel(lo_ref, hi_ref, dst_ref, h_ref, xb_hbm, stage, zero, sem, *, per):
    i = pl.program_id(0)
    n = pl.num_programs(0)
    slot = i % 2
    tm = dst_ref.shape[2] // 2

    def wait_tile(s):
        pltpu.make_async_copy(xb_hbm.at[pl.ds(0, 2 * tm * per), :], xb_hbm.at[pl.ds(0, 2 * tm * per), :], sem.at[s]).wait()

    @pl.when(i >= 2)
    def _():
        wait_tile(slot)

    stage[slot] = h_ref[...]

    def body(r, carry):
        for k in range(2):
            _tile_copy(stage.at[slot], r * per, xb_hbm, dst_ref[0, 0, k * tm + r], per, sem.at[slot]).start(priority=k)
        return carry

    lax.fori_loop(0, tm, body, 0, unroll=8)

    @pl.when(i == n - 1)
    def _():
        @pl.when(n >= 2)
        def _():
            wait_tile(1 - slot)
        wait_tile(slot)
        zero[...] = jnp.zeros_like(zero)

        def fill(e, carry):
            def one(s, c):
                _tile_copy(zero, 0, xb_hbm, s * per, per, sem.at[2]).start()
                return c
            lax.fori_loop(lo_ref[e], hi_ref[e], one, 0)

            def one_wait(s, c):
                _tile_copy(zero, 0, xb_hbm, s * per, per, sem.at[2]).wait()
                return c
            lax.fori_loop(lo_ref[e], hi_ref[e], one_wait, 0)
            return carry

        lax.fori_loop(0, N_EXPERTS + 1, fill, 0)


def _dispatch(fill_lo, fill_hi, drow3, hp, n_slots, per):
    n, tm = drow3.shape[0], drow3.shape[2] // 2
    smem3 = pl.BlockSpec((1, 1, 2 * tm), lambda i, *_: (i, 0, 0), memory_space=pltpu.SMEM)
    grid_spec = pltpu.PrefetchScalarGridSpec(
        num_scalar_prefetch=2,
        grid=(n,),
        in_specs=[smem3, pl.BlockSpec((tm * per, LANES), lambda i, *_: (i, 0))],
        out_specs=pl.BlockSpec(memory_space=pl.ANY),
        scratch_shapes=[pltpu.VMEM((2, tm * per, LANES), U32), pltpu.VMEM((per, LANES), U32),
                        pltpu.SemaphoreType.DMA((3,))])
    return pl.pallas_call(
        functools.partial(_dispatch_kernel, per=per),
        grid_spec=grid_spec,
        out_shape=jax.ShapeDtypeStruct((n_slots * per, LANES), U32),
        compiler_params=_cparams(("arbitrary",)),
        name="dispatch",
    )(fill_lo, fill_hi, drow3, hp)


def _expert_kernel(be_ref, first_ref, par_ref, nxt_ref, has_ref, x_ref, w1_hbm, w3_hbm, w2_hbm, y_ref,
                   wf1, wf3, wf2, w1b, w3b, w2b, sem, *, per):
    i = pl.program_id(0)

    def weight_copies(e, s):
        return [pltpu.make_async_copy(src.at[e], dst.at[s], sem.at[s])
                for src, dst in ((w1_hbm, wf1), (w3_hbm, wf3), (w2_hbm, wf2))]

    @pl.when(i == 0)
    def _():
        for cp in weight_copies(be_ref[0], 0):
            cp.start()

    @pl.when(first_ref[i] == 1)
    def _():
        s = par_ref[i]
        for cp in weight_copies(be_ref[i], s):
            cp.wait()

        @pl.when(has_ref[i] == 1)
        def _():
            for cp in weight_copies(nxt_ref[i], 1 - s):
                cp.start()

        w1b[...] = wf1[s].astype(BF16)
        w3b[...] = wf3[s].astype(BF16)
        w2b[...] = wf2[s].astype(BF16)

    x = _load_token_tiles(x_ref, MOE_BLOCK, per)
    a = jnp.dot(x, w1b[...], preferred_element_type=F32)
    b = jnp.dot(x, w3b[...], preferred_element_type=F32)
    mid = (jax.nn.silu(a) * b).astype(BF16)
    _store_token_tiles(y_ref, _pack_bf16_pairs(jnp.dot(mid, w2b[...], preferred_element_type=F32)))


def _experts(blk_e, xb, w1, w3, w2):
    d, ff = w1.shape[1], w1.shape[2]
    per = d // 2 // LANES
    nblk = xb.shape[0] // per // MOE_BLOCK
    rows = pl.BlockSpec((MOE_BLOCK * per, LANES), lambda i, *_: (i, 0))
    first = jnp.concatenate([jnp.ones((1,), I32), (blk_e[1:] != blk_e[:-1]).astype(I32)])
    parity = (jnp.cumsum(first) - 1) % 2
    nxt = jnp.min(jnp.where(blk_e[None, :] > blk_e[:, None], blk_e[None, :], N_EXPERTS), axis=1)
    has_next = (nxt < N_EXPERTS).astype(I32)
    nxt = jnp.minimum(nxt, N_EXPERTS - 1)
    grid_spec = pltpu.PrefetchScalarGridSpec(
        num_scalar_prefetch=5,
        grid=(nblk,),
        in_specs=[rows] + [pl.BlockSpec(memory_space=pl.ANY)] * 3,
        out_specs=rows,
        scratch_shapes=[pltpu.VMEM((2, d, ff), F32), pltpu.VMEM((2, d, ff), F32), pltpu.VMEM((2, ff, d), F32),
                        pltpu.VMEM((d, ff), BF16), pltpu.VMEM((d, ff), BF16), pltpu.VMEM((ff, d), BF16),
                        pltpu.SemaphoreType.DMA((2,))])
    return pl.pallas_call(
        functools.partial(_expert_kernel, per=per),
        grid_spec=grid_spec,
        out_shape=jax.ShapeDtypeStruct(xb.shape, U32),
        compiler_params=_cparams(("arbitrary",)),
        name="experts",
    )(blk_e, first, parity.astype(I32), nxt.astype(I32), has_next, xb, w1, w3, w2)


def _combine_kernel(src_ref, srcn_ref, y_hbm, wt_ref, x1_ref, mod_ref, g_ref, b_ref, o_ref, ybuf, sem, *, alpha, per):
    i = pl.program_id(0)
    n = pl.num_programs(0)
    slot = i % 2
    tm = x1_ref.shape[0]

    def wait(s):
        for k in range(2):
            pltpu.make_async_copy(y_hbm.at[pl.ds(0, tm * per), :], ybuf.at[s, k], sem.at[s]).wait()

    @pl.when(i == 0)
    def _():
        def body(r, carry):
            for k in range(2):
                _tile_copy(y_hbm, src_ref[0, 0, k * tm + r], ybuf.at[0, k], r * per, per, sem.at[0]).start(priority=k)
            return carry
        lax.fori_loop(0, tm, body, 0, unroll=8)

    for r in range(tm):
        for k in range(2):
            _tile_copy(y_hbm, srcn_ref[0, 0, k * tm + r], ybuf.at[1 - slot, k], r * per, per, sem.at[1 - slot]).start(priority=k)

    wait(slot)
    wt = wt_ref[...]
    y0 = _load_token_tiles(ybuf.at[slot, 0], tm, per, F32)
    y1 = _load_token_tiles(ybuf.at[slot, 1], tm, per, F32)
    ffn = wt[:, 0:1] * y0 + wt[:, 1:2] * y1
    y = alpha * x1_ref[...] + (1.0 + mod_ref[0, 5:6, :]) * ffn
    o_ref[...] = _ln(y, g_ref[...], b_ref[...])

    @pl.when(i == n - 1)
    def _():
        wait(1 - slot)


def _combine(drow3, yb, wt_t, x1, mod3, g2, b2, seq, alpha):
    t, d = x1.shape
    per = d // 2 // LANES
    n, tm = drow3.shape[0], drow3.shape[2] // 2
    per_b = seq // tm
    cur = pl.BlockSpec((1, 1, 2 * tm), lambda i: (i, 0, 0), memory_space=pltpu.SMEM)
    nxt = pl.BlockSpec((1, 1, 2 * tm), lambda i: (jnp.minimum(i + 1, n - 1), 0, 0), memory_space=pltpu.SMEM)
    vec = pl.BlockSpec((1, d), lambda i: (0, 0))
    return pl.pallas_call(
        functools.partial(_combine_kernel, alpha=alpha, per=per),
        grid=(n,),
        in_specs=[cur, nxt,
                  pl.BlockSpec(memory_space=pl.ANY),
                  pl.BlockSpec((tm, 2), lambda i: (i, 0)),
                  pl.BlockSpec((tm, d), lambda i: (i, 0)),
                  pl.BlockSpec((1, 6, d), lambda i: (i // per_b, 0, 0)),
                  vec, vec],
        out_specs=pl.BlockSpec((tm, d), lambda i: (i, 0)),
        out_shape=jax.ShapeDtypeStruct((t, d), F32),
        scratch_shapes=[pltpu.VMEM((2, 2, tm * per, LANES), U32), pltpu.SemaphoreType.DMA((2,))],
        compiler_params=_cparams(("arbitrary",)),
        name="combine",
    )(drow3, drow3, yb, wt_t, x1, mod3, g2, b2)


def _pick_tile(n, want):
    tm = min(n, want)
    assert n % tm == 0
    return tm


def kernel(x, c, positions, ln0_g, ln0_b, w_ada, b_ada, w_in, w_out, ret_log_decay_f, ret_log_decay_b,
           ln1_g, ln1_b, w_group, b_group, w_sub, b_sub, w1, w3, w2, ln2_g, ln2_b):
    batch, seq, d = x.shape
    depth = w_ada.shape[0]
    t = batch * seq
    n_ah = d // 2 // ATTN_HEAD_DIM
    n_rh = d // 2 // RET_HEAD_DIM
    assert depth == 1 and d % (2 * RET_HEAD_DIM) == 0 and batch <= 8
    assert seq % (QT * CHAINS) == 0 and seq % (max(DILATIONS) * PERM_ROWS) == 0 and seq // max(DILATIONS) >= KW
    alpha = (2 * depth) ** 0.25

    inv_rope = ROPE_THETA ** (-jnp.arange(0, ROPE_DIM, 2, dtype=F32) / ROPE_DIM)
    inv_ret = RET_THETA ** (-jnp.linspace(0.0, 1.0, RET_HEAD_DIM // 2, dtype=F32))
    fac = jnp.tile(inv_rope, TOK_PER_ROW).reshape(1, LANES)
    fr = inv_ret.reshape(1, LANES)
    posb = jnp.broadcast_to(positions.astype(F32).reshape(t, 1), (t, LANES))
    posc = jnp.repeat(positions.astype(F32).reshape(t // TOK_PER_ROW, TOK_PER_ROW), ROPE_DIM // 2, axis=1)
    c8 = jnp.zeros((8, d), F32).at[:batch].set(c)
    row = lambda v: v.reshape(1, d)

    xs = x.reshape(t, d)
    mod = _ada(c8, w_ada[0], b_ada[0].reshape(1, -1))
    mod3 = mod[:batch].reshape(batch, 6, d)
    h, tabs = _prep(xs, posb, posc, mod3, row(ln0_g), row(ln0_b), fac, fr, seq, _pick_tile(seq, 512))
    w_in_b = w_in[0].astype(BF16)
    tmp = _pick_tile(seq, 2048)
    z_aqk = _proj(_proj_attn_kernel, lambda j: j, 2, h, w_in_b, tabs, tmp, "proj_attn")
    z_rqk = _proj(_proj_ret_kernel, lambda j: 3 + j, 2, h, w_in_b, tabs, _pick_tile(seq, 1024), "proj_ret")
    z_pl = _proj(_proj_plain_kernel, lambda j: 2 + 3 * jnp.minimum(j, 1) + jnp.maximum(j - 1, 0), 3,
                 h, w_in_b, None, tmp, "proj_plain")
    attn = _attn(z_aqk, z_pl, batch, seq, n_ah)
    lg = jnp.stack([ret_log_decay_f[0], ret_log_decay_b[0]]).astype(F32)
    r = _ret(z_rqk, z_pl, lg, batch, seq, n_rh, _pick_tile(seq, 2048))
    x1, hp = _outproj(attn, r, xs, mod3, row(ln0_g), row(ln0_b), row(ln1_g[0]), row(ln1_b[0]),
                      w_out[0].astype(BF16), seq, _pick_tile(seq, 512), alpha)
    wr = jnp.zeros((ROUTER_ROWS, d), F32)
    wr = wr.at[:N_GROUPS].set(w_group[0].T)
    wr = wr.at[N_GROUPS:N_GROUPS + N_EXPERTS].set(w_sub[0].transpose(0, 2, 1).reshape(N_EXPERTS, d))
    br = jnp.zeros((ROUTER_ROWS, 1), F32)
    br = br.at[:N_GROUPS, 0].set(b_group[0]).at[N_GROUPS:N_GROUPS + N_EXPERTS, 0].set(b_sub[0].reshape(-1))
    eid3, wt, rank3, cnt = _router(hp, wr.astype(BF16), br, _pick_tile(seq, 512))
    counts = cnt[:, 0].astype(I32)
    padded = (counts + MOE_BLOCK - 1) // MOE_BLOCK * MOE_BLOCK
    pend = jnp.cumsum(padded)
    pstart = pend - padded
    n_slots = 2 * t + N_EXPERTS * MOE_BLOCK
    nblk = n_slots // MOE_BLOCK
    starts = jnp.arange(nblk, dtype=I32) * MOE_BLOCK
    blk_e = jnp.minimum(jnp.sum((pend[None, :] <= starts[:, None]).astype(I32), axis=1), N_EXPERTS - 1)
    fill_lo = jnp.concatenate([pstart + counts, pend[-1:]])
    fill_hi = jnp.concatenate([pend, jnp.full((1,), n_slots, I32)])
    per = d // 2 // LANES
    onehot = eid3[..., None] == jnp.arange(N_EXPERTS, dtype=I32)
    drow3 = ((rank3 + jnp.sum(jnp.where(onehot, pstart, 0), axis=-1)) * per).reshape(eid3.shape[0], 1, -1)
    xb = _dispatch(fill_lo, fill_hi, drow3, hp, n_slots, per)
    yb = _experts(blk_e, xb, w1[0], w3[0], w2[0])
    out = _combine(drow3, yb, wt.T, x1, mod3, row(ln2_g[0]), row(ln2_b[0]), seq, alpha)
    return out.reshape(batch, seq, d)
```
